```python
import jax, jax.numpy as jnp
from jax import lax
import numpy as np

D_MODEL = 1024
BATCH = 32
SEQ = 2048
DEPTH = 1

HEAD_DIM = 64
ATTN_HEADS = 8
ATTN_KV_HEADS = 2
ATTN_GROUP = ATTN_HEADS // ATTN_KV_HEADS
ATTN_DIM = ATTN_HEADS * HEAD_DIM
KV_DIM = ATTN_KV_HEADS * HEAD_DIM
WINDOW = 128
BLOCK = WINDOW
ROPE_THETA = 10000.0

RWKV_HEADS = 8
RWKV_HEAD = 64
RWKV_DIM = RWKV_HEADS * RWKV_HEAD
DECAY_LORA = 64
ICLR_LORA = 64
GATE_LORA = 160
RWKV_GN_EPS = 64e-5
RWKV_IN = 3 * RWKV_DIM + DECAY_LORA + ICLR_LORA + GATE_LORA

IN_COLS = ATTN_DIM + 2 * KV_DIM + RWKV_IN + 2 * D_MODEL

PEER_HEADS = 8
N_KEYS = 128
N_EXPERTS = N_KEYS * N_KEYS
PEER_QDIM = 128
PEER_HALF = PEER_QDIM // 2
PEER_TOPK = 16
PEER_SEL = PEER_HEADS * PEER_TOPK
PEER_CHUNK = 128

NORM_EPS = 1e-5
NEG_INF = -1e30

kernel_name = 'hybrid_swa_rwkv7_peer_block'


def rms_norm(x, w):
    xf = x.astype(jnp.float32)
    y = xf * lax.rsqrt(jnp.mean(xf * xf, axis=-1, keepdims=True) + NORM_EPS)
    return (y * w.astype(jnp.float32)).astype(x.dtype)


def rope(x, pos):
    inv = ROPE_THETA ** (-jnp.arange(0, HEAD_DIM, 2, dtype=jnp.float32) / HEAD_DIM)
    ang = pos.astype(jnp.float32)[:, None] * inv[None, :]
    cos = jnp.cos(ang)[None, :, None, :]
    sin = jnp.sin(ang)[None, :, None, :]
    x1, x2 = jnp.split(x, 2, axis=-1)
    return jnp.concatenate([x1 * cos - x2 * sin, x2 * cos + x1 * sin], axis=-1)


def prev_block(t):
    return jnp.concatenate([jnp.zeros_like(t[:, :1]), t[:, :-1]], axis=1)


def sliding_window_attention(q, k, v, sinks):
    b, s = q.shape[0], q.shape[1]
    nb = s // BLOCK
    qb = q.reshape(b, nb, BLOCK, ATTN_KV_HEADS, ATTN_GROUP, HEAD_DIM)
    kb = k.reshape(b, nb, BLOCK, ATTN_KV_HEADS, HEAD_DIM)
    vb = v.reshape(b, nb, BLOCK, ATTN_KV_HEADS, HEAD_DIM)
    kb = jnp.concatenate([prev_block(kb), kb], axis=2)
    vb = jnp.concatenate([prev_block(vb), vb], axis=2)
    logits = jnp.einsum('bnqhgd,bnkhd->bnhgqk', qb, kb) * (HEAD_DIM ** -0.5)
    qi = jnp.arange(BLOCK)[:, None] + BLOCK
    kj = jnp.arange(2 * BLOCK)[None, :]
    diff = qi - kj
    band = (diff >= 0) & (diff < WINDOW)
    valid = (jnp.arange(nb)[:, None] * BLOCK + kj - BLOCK) >= 0
    mask = band[None] & valid[:, None, :]
    logits = jnp.where(mask[None, :, None, None], logits, NEG_INF)
    sink = sinks.astype(jnp.float32).reshape(ATTN_KV_HEADS, ATTN_GROUP)[None, None, :, :, None, None]
    m = jnp.maximum(jnp.max(logits, axis=-1, keepdims=True), sink)
    p = jnp.exp(logits - m)
    probs = p / (jnp.sum(p, axis=-1, keepdims=True) + jnp.exp(sink - m))
    out = jnp.einsum('bnhgqk,bnkhd->bnqhgd', probs, vb)
    return out.reshape(b, s, ATTN_DIM)


def token_shift(z, mu):
    z_prev = jnp.concatenate([jnp.zeros_like(z[:, :1]), z[:, :-1]], axis=1)
    return z + mu * (z_prev - z)


def to_heads(t):
    return t.reshape(t.shape[0], t.shape[1], RWKV_HEADS, RWKV_HEAD)


def rwkv7_time_mix(zr, w0, w2, a0, a2, g2, k_k, k_a, r_k, ln_w, ln_b):
    f32 = jnp.float32
    b, s = zr.shape[0], zr.shape[1]
    cuts = [RWKV_DIM, 2 * RWKV_DIM, 3 * RWKV_DIM, 3 * RWKV_DIM + DECAY_LORA,
            3 * RWKV_DIM + DECAY_LORA + ICLR_LORA]
    r, k, v, w_lo, a_lo, g_lo = jnp.split(zr, cuts, axis=-1)
    r, k, v = r.astype(f32), k.astype(f32), v.astype(f32)
    w = -jax.nn.softplus(-(w0 + jnp.tanh(w_lo) @ w2).astype(f32)) - 0.5
    decay = jnp.exp(-jnp.exp(w))
    a = jax.nn.sigmoid((a0 + a_lo @ a2).astype(f32))
    g = (jax.nn.sigmoid(g_lo) @ g2).astype(f32)
    kk = to_heads(k * k_k.astype(f32))
    kk = kk / jnp.maximum(jnp.sqrt(jnp.sum(kk * kk, axis=-1, keepdims=True)), 1e-12)
    k = k * (1.0 + (a - 1.0) * k_a.astype(f32))
    rh, kh, vh, wh, ah = to_heads(r), to_heads(k), to_heads(v), to_heads(decay), to_heads(a)

    def step(state, inp):
        r_t, w_t, k_t, v_t, kk_t, a_t = inp
        sa = jnp.einsum('bhvk,bhk->bhv', state, -kk_t)
        state = (state * w_t[:, :, None, :]
                 + sa[..., None] * (kk_t * a_t)[:, :, None, :]
                 + v_t[..., None] * k_t[:, :, None, :])
        return state, jnp.einsum('bhvk,bhk->bhv', state, r_t)

    tm = lambda t: jnp.swapaxes(t, 0, 1)
    state0 = jnp.zeros((b, RWKV_HEADS, RWKV_HEAD, RWKV_HEAD), f32)
    _, y = lax.scan(step, state0, (tm(rh), tm(wh), tm(kh), tm(vh), tm(kk), tm(ah)))
    y = tm(y)
    mean = jnp.mean(y, axis=-1, keepdims=True)
    var = jnp.mean(jnp.square(y - mean), axis=-1, keepdims=True)
    y = ((y - mean) * lax.rsqrt(var + RWKV_GN_EPS)).reshape(b, s, RWKV_DIM)
    y = y * ln_w.astype(f32) + ln_b.astype(f32)
    bonus = jnp.sum(rh * kh * r_k.astype(f32), axis=-1, keepdims=True) * vh
    y = y + bonus.reshape(b, s, RWKV_DIM)
    return (y * g).astype(zr.dtype)


def peer_ffn(xn, wq, subkeys, u_tab, v_tab):
    f32 = jnp.float32
    b, s, d = xn.shape
    t = b * s
    xt = xn.reshape(t, d)
    q = (xt @ wq).reshape(t, PEER_HEADS, 2, PEER_HALF)
    sc = jnp.einsum('thpd,hpnd->thpn', q, subkeys).astype(f32)
    s1, i1 = lax.top_k(sc[:, :, 0], PEER_TOPK)
    s2, i2 = lax.top_k(sc[:, :, 1], PEER_TOPK)
    cand = (s1[..., :, None] + s2[..., None, :]).reshape(t, PEER_HEADS, PEER_TOPK * PEER_TOPK)
    cidx = (i1[..., :, None] * N_KEYS + i2[..., None, :]).reshape(t, PEER_HEADS, PEER_TOPK * PEER_TOPK)
    top, pos = lax.top_k(cand, PEER_TOPK)
    idx = jnp.take_along_axis(cidx, pos, axis=-1)
    gate = jax.nn.softmax(top, axis=-1)
    nc = t // PEER_CHUNK

    def expert_block(args):
        xc, ic, gc = args
        hpre = jnp.einsum('ckd,cd->ck', u_tab[ic], xc).astype(f32)
        act = jax.nn.gelu(hpre, approximate=False) * gc
        return jnp.einsum('ck,ckd->cd', act.astype(v_tab.dtype), v_tab[ic])

    out = lax.map(expert_block, (xt.reshape(nc, PEER_CHUNK, d),
                                 idx.reshape(nc, PEER_CHUNK, PEER_SEL),
                                 gate.reshape(nc, PEER_CHUNK, PEER_SEL)))
    return out.reshape(b, s, d).astype(xn.dtype)


def setup_inputs(seed: int = 0) -> dict:
    key = jax.random.key(seed)
    ks = jax.random.split(key, 24)
    L = DEPTH
    f32 = jnp.float32
    nrm = lambda k, shape, scale: jax.random.normal(k, shape, f32) * scale
    return {
        'x': nrm(ks[0], (BATCH, SEQ, D_MODEL), 1.0),
        'norm_mix_w': 1.0 + nrm(ks[1], (L, D_MODEL), 0.02),
        'w_in': nrm(ks[2], (L, D_MODEL, IN_COLS), D_MODEL ** -0.5),
        'shift_mu': jax.random.uniform(ks[3], (L, RWKV_IN), f32),
        'attn_sinks': nrm(ks[4], (L, ATTN_HEADS), 0.5),
        'decay_w0': -1.0 + nrm(ks[5], (L, RWKV_DIM), 0.5),
        'decay_w2': nrm(ks[6], (L, DECAY_LORA, RWKV_DIM), 0.1 * DECAY_LORA ** -0.5),
        'iclr_a0': nrm(ks[7], (L, RWKV_DIM), 0.1),
        'iclr_a2': nrm(ks[8], (L, ICLR_LORA, RWKV_DIM), 0.5 * ICLR_LORA ** -0.5),
        'gate_g2': nrm(ks[9], (L, GATE_LORA, RWKV_DIM), GATE_LORA ** -0.5),
        'k_k': 0.85 + nrm(ks[10], (L, RWKV_DIM), 0.02),
        'k_a': 1.0 + nrm(ks[11], (L, RWKV_DIM), 0.02),
        'r_k': nrm(ks[12], (L, RWKV_HEADS, RWKV_HEAD), 0.1),
        'ln_x_w': 1.0 + nrm(ks[13], (L, RWKV_DIM), 0.02),
        'ln_x_b': nrm(ks[14], (L, RWKV_DIM), 0.02),
        'proj_attn': nrm(ks[15], (L, ATTN_DIM, D_MODEL), ATTN_DIM ** -0.5),
        'proj_rwkv': nrm(ks[16], (L, RWKV_DIM, D_MODEL), RWKV_DIM ** -0.5),
        'w_out': nrm(ks[17], (L, D_MODEL, D_MODEL), D_MODEL ** -0.5),
        'norm_ffn_w': 1.0 + nrm(ks[18], (L, D_MODEL), 0.02),
        'peer_wq': nrm(ks[19], (L, D_MODEL, PEER_HEADS * PEER_QDIM), D_MODEL ** -0.5),
        'peer_subkeys': nrm(ks[20], (L, PEER_HEADS, 2, N_KEYS, PEER_HALF), PEER_HALF ** -0.5),
        'peer_u': nrm(ks[21], (L, N_EXPERTS, D_MODEL), D_MODEL ** -0.5),
        'peer_v': nrm(ks[22], (L, N_EXPERTS, D_MODEL), PEER_HEADS ** -0.5),
        'norm_final_w': 1.0 + nrm(ks[23], (D_MODEL,), 0.02),
    }


def reference(x, norm_mix_w, w_in, shift_mu, attn_sinks, decay_w0, decay_w2, iclr_a0, iclr_a2,
              gate_g2, k_k, k_a, r_k, ln_x_w, ln_x_b, proj_attn, proj_rwkv, w_out, norm_ffn_w,
              peer_wq, peer_subkeys, peer_u, peer_v, norm_final_w):
    f32 = jnp.float32
    b, s = x.shape[0], x.shape[1]
    pos = jnp.arange(s, dtype=jnp.int32)
    o1 = ATTN_DIM
    o2 = o1 + KV_DIM
    o3 = o2 + KV_DIM
    o4 = o3 + RWKV_IN
    o5 = o4 + D_MODEL
    h = x
    for l in range(DEPTH):
        xn = rms_norm(h, norm_mix_w[l])
        z = xn @ w_in[l]
        q = rope(z[..., :o1].reshape(b, s, ATTN_HEADS, HEAD_DIM).astype(f32), pos)
        k = rope(z[..., o1:o2].reshape(b, s, ATTN_KV_HEADS, HEAD_DIM).astype(f32), pos)
        v = z[..., o2:o3].reshape(b, s, ATTN_KV_HEADS, HEAD_DIM).astype(f32)
        y_attn = sliding_window_attention(q, k, v, attn_sinks[l]).astype(h.dtype)
        zr = token_shift(z[..., o3:o4], shift_mu[l])
        y_rwkv = rwkv7_time_mix(zr, decay_w0[l], decay_w2[l], iclr_a0[l], iclr_a2[l], gate_g2[l],
                                k_k[l], k_a[l], r_k[l], ln_x_w[l], ln_x_b[l]).astype(h.dtype)
        g_attn = jax.nn.sigmoid(z[..., o4:o5])
        g_rwkv = jax.nn.sigmoid(z[..., o5:])
        merged = g_attn * (y_attn @ proj_attn[l]) + g_rwkv * (y_rwkv @ proj_rwkv[l])
        h = h + merged @ w_out[l]
        h = h + peer_ffn(rms_norm(h, norm_ffn_w[l]), peer_wq[l], peer_subkeys[l], peer_u[l], peer_v[l])
    return rms_norm(h, norm_final_w)
```

```python
import functools

import jax, jax.numpy as jnp
from jax import lax
from jax.experimental import pallas as pl
from jax.experimental.pallas import tpu as pltpu

D_MODEL = 1024
DEPTH = 1

HEAD_DIM = 64
ATTN_HEADS = 8
ATTN_KV_HEADS = 2
ATTN_GROUP = ATTN_HEADS // ATTN_KV_HEADS
ATTN_DIM = ATTN_HEADS * HEAD_DIM
KV_DIM = ATTN_KV_HEADS * HEAD_DIM
WINDOW = 128
BLOCK = WINDOW
ROPE_THETA = 10000.0

RWKV_HEADS = 8
RWKV_HEAD = 64
RWKV_DIM = RWKV_HEADS * RWKV_HEAD
DECAY_LORA = 64
ICLR_LORA = 64
GATE_LORA = 160
RWKV_GN_EPS = 64e-5
RWKV_IN = 3 * RWKV_DIM + DECAY_LORA + ICLR_LORA + GATE_LORA

PEER_HEADS = 8
N_KEYS = 128
PEER_HALF = 64
PEER_TOPK = 16
PEER_SEL = PEER_HEADS * PEER_TOPK

NORM_EPS = 1e-5
NEG_INF = -1e30

LANES = 128
SUBLANES = 8
VMEM_LIMIT_BYTES = 56 * 1024 * 1024

ROW_WORDS = D_MODEL // 2
ROW_SUBLANES = ROW_WORDS // LANES
NEG_FILL = -3.0e38


def rms_norm(x, w):
    xf = x.astype(jnp.float32)
    y = xf * lax.rsqrt(jnp.mean(xf * xf, axis=-1, keepdims=True) + NORM_EPS)
    return (y * w.astype(jnp.float32)).astype(x.dtype)


def rope(x, pos):
    inv = ROPE_THETA ** (-jnp.arange(0, HEAD_DIM, 2, dtype=jnp.float32) / HEAD_DIM)
    ang = pos.astype(jnp.float32)[:, None] * inv[None, :]
    cos = jnp.cos(ang)[None, :, None, :]
    sin = jnp.sin(ang)[None, :, None, :]
    x1, x2 = jnp.split(x, 2, axis=-1)
    return jnp.concatenate([x1 * cos - x2 * sin, x2 * cos + x1 * sin], axis=-1)


def prev_block(t):
    return jnp.concatenate([jnp.zeros_like(t[:, :1]), t[:, :-1]], axis=1)


def sliding_window_attention(q, k, v, sinks):
    b, s = q.shape[0], q.shape[1]
    nb = s // BLOCK
    qb = q.reshape(b, nb, BLOCK, ATTN_KV_HEADS, ATTN_GROUP, HEAD_DIM)
    kb = k.reshape(b, nb, BLOCK, ATTN_KV_HEADS, HEAD_DIM)
    vb = v.reshape(b, nb, BLOCK, ATTN_KV_HEADS, HEAD_DIM)
    kb = jnp.concatenate([prev_block(kb), kb], axis=2)
    vb = jnp.concatenate([prev_block(vb), vb], axis=2)
    logits = jnp.einsum('bnqhgd,bnkhd->bnhgqk', qb, kb) * (HEAD_DIM ** -0.5)
    qi = jnp.arange(BLOCK)[:, None] + BLOCK
    kj = jnp.arange(2 * BLOCK)[None, :]
    diff = qi - kj
    band = (diff >= 0) & (diff < WINDOW)
    valid = (jnp.arange(nb)[:, None] * BLOCK + kj - BLOCK) >= 0
    mask = band[None] & valid[:, None, :]
    logits = jnp.where(mask[None, :, None, None], logits, NEG_INF)
    sink = sinks.astype(jnp.float32).reshape(ATTN_KV_HEADS, ATTN_GROUP)[None, None, :, :, None, None]
    m = jnp.maximum(jnp.max(logits, axis=-1, keepdims=True), sink)
    p = jnp.exp(logits - m)
    probs = p / (jnp.sum(p, axis=-1, keepdims=True) + jnp.exp(sink - m))
    out = jnp.einsum('bnhgqk,bnkhd->bnqhgd', probs, vb)
    return out.reshape(b, s, ATTN_DIM)


def token_shift(z, mu):
    z_prev = jnp.concatenate([jnp.zeros_like(z[:, :1]), z[:, :-1]], axis=1)
    return z + mu * (z_prev - z)


def to_heads(t):
    return t.reshape(t.shape[0], t.shape[1], RWKV_HEADS, RWKV_HEAD)


def rwkv7_time_mix(zr, w0, w2, a0, a2, g2, k_k, k_a, r_k, ln_w, ln_b):
    f32 = jnp.float32
    b, s = zr.shape[0], zr.shape[1]
    cuts = [RWKV_DIM, 2 * RWKV_DIM, 3 * RWKV_DIM, 3 * RWKV_DIM + DECAY_LORA,
            3 * RWKV_DIM + DECAY_LORA + ICLR_LORA]
    r, k, v, w_lo, a_lo, g_lo = jnp.split(zr, cuts, axis=-1)
    w = -jax.nn.softplus(-(w0 + jnp.tanh(w_lo) @ w2).astype(f32)) - 0.5
    decay = jnp.exp(-jnp.exp(w))
    a = jax.nn.sigmoid((a0 + a_lo @ a2).astype(f32))
    g = (jax.nn.sigmoid(g_lo) @ g2).astype(f32)
    kk = to_heads(k * k_k.astype(f32))
    kk = kk / jnp.maximum(jnp.sqrt(jnp.sum(kk * kk, axis=-1, keepdims=True)), 1e-12)
    k = k * (1.0 + (a - 1.0) * k_a.astype(f32))
    rh, kh, vh, wh, ah = to_heads(r), to_heads(k), to_heads(v), to_heads(decay), to_heads(a)

    def step(state, inp):
        r_t, w_t, k_t, v_t, kk_t, a_t = inp
        sa = jnp.einsum('bhvk,bhk->bhv', state, -kk_t)
        state = (state * w_t[:, :, None, :]
                 + sa[..., None] * (kk_t * a_t)[:, :, None, :]
                 + v_t[..., None] * k_t[:, :, None, :])
        return state, jnp.einsum('bhvk,bhk->bhv', state, r_t)

    tm = lambda t: jnp.swapaxes(t, 0, 1)
    state0 = jnp.zeros((b, RWKV_HEADS, RWKV_HEAD, RWKV_HEAD), f32)
    _, y = lax.scan(step, state0, (tm(rh), tm(wh), tm(kh), tm(vh), tm(kk), tm(ah)))
    y = tm(y)
    mean = jnp.mean(y, axis=-1, keepdims=True)
    var = jnp.mean(jnp.square(y - mean), axis=-1, keepdims=True)
    y = ((y - mean) * lax.rsqrt(var + RWKV_GN_EPS)).reshape(b, s, RWKV_DIM)
    y = y * ln_w.astype(f32) + ln_b.astype(f32)
    bonus = jnp.sum(rh * kh * r_k.astype(f32), axis=-1, keepdims=True) * vh
    y = y + bonus.reshape(b, s, RWKV_DIM)
    return (y * g).astype(zr.dtype)


def _top16(a, n_rows):
    rid = lax.broadcasted_iota(jnp.int32, (n_rows, LANES), 0)
    vals, rows = [], []
    for _ in range(PEER_TOPK):
        m = jnp.max(a, axis=0, keepdims=True)
        r = jnp.min(jnp.where(a == m, rid, n_rows), axis=0, keepdims=True)
        vals.append(m)
        rows.append(r)
        a = jnp.where(rid == r, NEG_FILL, a)
    return jnp.concatenate(vals, axis=0), jnp.concatenate(rows, axis=0)


def _select_rows(pos, table):
    out = jnp.zeros(pos.shape, table.dtype)
    for a in range(PEER_TOPK):
        out = jnp.where(pos == a, table[a:a + 1, :], out)
    return out


def _peer_route_body(h_ref, nw_ref, wq_ref, sk_ref, xn_ref, idx_ref, gate_ref, q_ref, idx_t_ref, gate_t_ref, *, tm):
    xf = h_ref[...]
    xn = xf * lax.rsqrt(jnp.mean(xf * xf, axis=-1, keepdims=True) + NORM_EPS) * nw_ref[...]
    xn_ref[...] = xn
    q_ref[...] = jnp.dot(xn, wq_ref[...], preferred_element_type=jnp.float32, precision=lax.Precision.HIGHEST)

    def head(i, carry):
        c = i // PEER_HEADS
        h = i % PEER_HEADS
        tok0 = pl.multiple_of(c * LANES, LANES)
        col0 = pl.multiple_of(h * 2 * PEER_HALF, 2 * PEER_HALF)
        qh = q_ref[pl.ds(tok0, LANES), pl.ds(col0, 2 * PEER_HALF)]
        sc = lax.dot_general(sk_ref[h], qh, (((1,), (1,)), ((), ())),
                             preferred_element_type=jnp.float32, precision=lax.Precision.HIGHEST)
        s1, i1 = _top16(sc[:N_KEYS], N_KEYS)
        s2, i2 = _top16(sc[N_KEYS:], N_KEYS)
        cand = jnp.concatenate([s1[a:a + 1, :] + s2 for a in range(PEER_TOPK)], axis=0)
        top, pos = _top16(cand, PEER_TOPK * PEER_TOPK)
        e = _select_rows(pos >> 4, i1) * N_KEYS + _select_rows(pos & 15, i2)
        p = jnp.exp(top - top[0:1, :])
        row0 = pl.multiple_of(h * PEER_TOPK, PEER_TOPK)
        idx_t_ref[pl.ds(row0, PEER_TOPK), pl.ds(tok0, LANES)] = e * ROW_SUBLANES
        gate_t_ref[pl.ds(row0, PEER_TOPK), pl.ds(tok0, LANES)] = p / jnp.sum(p, axis=0, keepdims=True)
        return carry

    lax.fori_loop(0, (tm // LANES) * PEER_HEADS, head, 0)
    for c in range(tm // LANES):
        idx_ref[c * LANES:(c + 1) * LANES, :] = idx_t_ref[:, c * LANES:(c + 1) * LANES].T
        gate_ref[c * LANES:(c + 1) * LANES, :] = gate_t_ref[:, c * LANES:(c + 1) * LANES].T


def _block_diag_subkeys(subkeys):
    z = jnp.zeros_like(subkeys[:, 0])
    top = jnp.concatenate([subkeys[:, 0], z], axis=-1)
    bot = jnp.concatenate([z, subkeys[:, 1]], axis=-1)
    return jnp.concatenate([top, bot], axis=1)


def peer_route(h, norm_w, wq, subkeys, *, tm=256):
    t, d = h.shape
    sk = _block_diag_subkeys(subkeys)
    return pl.pallas_call(
        functools.partial(_peer_route_body, tm=tm),
        grid=(t // tm,),
        in_specs=[
            pl.BlockSpec((tm, d), lambda i: (i, 0)),
            pl.BlockSpec((1, d), lambda i: (0, 0)),
            pl.BlockSpec(wq.shape, lambda i: (0, 0)),
            pl.BlockSpec(sk.shape, lambda i: (0, 0, 0)),
        ],
        out_specs=[pl.BlockSpec((tm, d), lambda i: (i, 0)),
                   pl.BlockSpec((tm, PEER_SEL), lambda i: (i, 0)),
                   pl.BlockSpec((tm, PEER_SEL), lambda i: (i, 0))],
        out_shape=[jax.ShapeDtypeStruct((t, d), jnp.float32),
                   jax.ShapeDtypeStruct((t, PEER_SEL), jnp.int32),
                   jax.ShapeDtypeStruct((t, PEER_SEL), jnp.float32)],
        scratch_shapes=[pltpu.VMEM((tm, d), jnp.float32), pltpu.VMEM((PEER_SEL, tm), jnp.int32),
                        pltpu.VMEM((PEER_SEL, tm), jnp.float32)],
        compiler_params=pltpu.CompilerParams(dimension_semantics=("arbitrary",), vmem_limit_bytes=VMEM_LIMIT_BYTES),
        name="peer_route",
    )(h, norm_w.reshape(1, d), wq, sk)


def pack_table(tab):
    n, d = tab.shape
    b = lax.bitcast_convert_type(tab.astype(jnp.bfloat16), jnp.uint16).astype(jnp.uint32)
    w = b[:, : d // 2] | (b[:, d // 2:] << 16)
    return lax.bitcast_convert_type(w, jnp.int32).reshape(n * ROW_SUBLANES, LANES)


def _unpack(w):
    lo = lax.bitcast_convert_type(w << 16, jnp.float32)
    hi = lax.bitcast_convert_type(w & jnp.int32(-65536), jnp.float32)
    return lo, hi


def _load_table(tab_hbm, tab_vmem, sem):
    @pl.when(pl.program_id(0) == 0)
    def _():
        cp = pltpu.make_async_copy(tab_hbm, tab_vmem, sem)
        cp.start()
        cp.wait()


def _rowsum8(ps, sub):
    f = [jnp.concatenate([ps[0], ps[4]], axis=0), jnp.concatenate([ps[2], ps[6]], axis=0),
         jnp.concatenate([ps[1], ps[5]], axis=0), jnp.concatenate([ps[3], ps[7]], axis=0)]
    even2 = (sub & 2) == 0
    even1 = (sub & 1) == 0

    def merge(x, y, stride, keep_x):
        u = x + pltpu.roll(x, SUBLANES - stride, 0)
        v = y + pltpu.roll(y, stride, 0)
        return jnp.where(keep_x, u, v)

    m01 = merge(f[0], f[1], 2, even2)
    m23 = merge(f[2], f[3], 2, even2)
    return merge(m01, m23, 1, even1)


def _peer_u_body(idx_ref, x_ref, gate_ref, tab_hbm, act_ref, tab_vmem, q_ref, sem, *, tb):
    _load_table(tab_hbm, tab_vmem, sem)
    sub = lax.broadcasted_iota(jnp.int32, (SUBLANES, LANES), 0)

    def tok(t, carry):
        xt = x_ref[t]
        xl = xt[0:ROW_SUBLANES]
        xh = xt[ROW_SUBLANES:]
        rows = []
        for g in range(PEER_SEL // 8):
            ps = []
            for j in range(8):
                e = idx_ref[t, g * 8 + j]
                lo, hi = _unpack(tab_vmem[pl.ds(pl.multiple_of(e, ROW_SUBLANES), ROW_SUBLANES), :])
                ps.append(lo * xl + hi * xh)
            rows.append(_rowsum8(ps, sub))
        q_ref[t] = jnp.concatenate(rows, axis=0)
        return carry

    lax.fori_loop(0, tb, tok, 0)

    def lane_sums(c, carry):
        for j in range(4):
            t = c * 4 + j
            act_ref[pl.ds(t, 1), :] = jnp.sum(q_ref[t].T, axis=0, keepdims=True)
        return carry

    lax.fori_loop(0, tb // 4, lane_sums, 0)
    hp = act_ref[...]
    act_ref[...] = 0.5 * hp * (1.0 + lax.erf(hp * (2.0 ** -0.5))) * gate_ref[...]


def peer_u(idx, x, gate, tab, *, tb=64):
    t = idx.shape[0]
    return pl.pallas_call(
        functools.partial(_peer_u_body, tb=tb),
        grid=(t // tb,),
        in_specs=[
            pl.BlockSpec((tb, PEER_SEL), lambda i: (i, 0), memory_space=pltpu.SMEM),
            pl.BlockSpec((tb, SUBLANES, LANES), lambda i: (i, 0, 0)),
            pl.BlockSpec((tb, PEER_SEL), lambda i: (i, 0)),
            pl.BlockSpec(memory_space=pl.ANY),
        ],
        out_specs=pl.BlockSpec((tb, PEER_SEL), lambda i: (i, 0)),
        out_shape=jax.ShapeDtypeStruct((t, PEER_SEL), jnp.float32),
        scratch_shapes=[pltpu.VMEM(tab.shape, jnp.int32), pltpu.VMEM((tb, PEER_SEL, LANES), jnp.float32),
                        pltpu.SemaphoreType.DMA],
        compiler_params=pltpu.CompilerParams(dimension_semantics=("arbitrary",), vmem_limit_bytes=VMEM_LIMIT_BYTES),
        name="peer_u",
    )(idx, x, gate, tab)


def _peer_v_body(idx_ref, act_ref, tab_hbm, o_ref, tab_vmem, sem, *, tb):
    _load_table(tab_hbm, tab_vmem, sem)

    def tok(t, carry):
        alo = jnp.zeros((ROW_SUBLANES, LANES), jnp.float32)
        ahi = jnp.zeros((ROW_SUBLANES, LANES), jnp.float32)
        for k in range(PEER_SEL):
            e = idx_ref[t, k]
            s = act_ref[t, k]
            lo, hi = _unpack(tab_vmem[pl.ds(pl.multiple_of(e, ROW_SUBLANES), ROW_SUBLANES), :])
            alo = alo + s * lo
            ahi = ahi + s * hi
        o_ref[t] = jnp.concatenate([alo, ahi], axis=0)
        return carry

    lax.fori_loop(0, tb, tok, 0)


def peer_v(idx, act, tab, *, tb=64):
    t = idx.shape[0]
    return pl.pallas_call(
        functools.partial(_peer_v_body, tb=tb),
        grid=(t // tb,),
        in_specs=[
            pl.BlockSpec((tb, PEER_SEL), lambda i: (i, 0), memory_space=pltpu.SMEM),
            pl.BlockSpec((tb, PEER_SEL), lambda i: (i, 0), memory_space=pltpu.SMEM),
            pl.BlockSpec(memory_space=pl.ANY),
        ],
        out_specs=pl.BlockSpec((tb, SUBLANES, LANES), lambda i: (i, 0, 0)),
        out_shape=jax.ShapeDtypeStruct((t, SUBLANES, LANES), jnp.float32),
        scratch_shapes=[pltpu.VMEM(tab.shape, jnp.int32), pltpu.SemaphoreType.DMA],
        compiler_params=pltpu.CompilerParams(dimension_semantics=("arbitrary",), vmem_limit_bytes=VMEM_LIMIT_BYTES),
        name="peer_v",
    )(idx, act, tab)


def _add_norm_body(h_ref, p_ref, w_ref, o_ref):
    xf = h_ref[...] + p_ref[...]
    y = xf * lax.rsqrt(jnp.mean(xf * xf, axis=-1, keepdims=True) + NORM_EPS)
    o_ref[...] = y * w_ref[...]


def add_norm(h, p, w, *, tm=1024):
    t, d = h.shape
    return pl.pallas_call(
        _add_norm_body,
        grid=(t // tm,),
        in_specs=[pl.BlockSpec((tm, d), lambda i: (i, 0)), pl.BlockSpec((tm, d), lambda i: (i, 0)),
                  pl.BlockSpec((1, d), lambda i: (0, 0))],
        out_specs=pl.BlockSpec((tm, d), lambda i: (i, 0)),
        out_shape=jax.ShapeDtypeStruct((t, d), h.dtype),
        compiler_params=pltpu.CompilerParams(dimension_semantics=("arbitrary",)),
        name="add_norm",
    )(h, p, w.reshape(1, d))


def kernel(x, norm_mix_w, w_in, shift_mu, attn_sinks, decay_w0, decay_w2, iclr_a0, iclr_a2,
           gate_g2, k_k, k_a, r_k, ln_x_w, ln_x_b, proj_attn, proj_rwkv, w_out, norm_ffn_w,
           peer_wq, peer_subkeys, peer_u_tab, peer_v_tab, norm_final_w):
    f32 = jnp.float32
    b, s = x.shape[0], x.shape[1]
    t = b * s
    pos = jnp.arange(s, dtype=jnp.int32)
    o1 = ATTN_DIM
    o2 = o1 + KV_DIM
    o3 = o2 + KV_DIM
    o4 = o3 + RWKV_IN
    o5 = o4 + D_MODEL
    l = 0
    h = x
    xn = rms_norm(h, norm_mix_w[l])
    z = xn @ w_in[l]
    q = rope(z[..., :o1].reshape(b, s, ATTN_HEADS, HEAD_DIM).astype(f32), pos)
    k = rope(z[..., o1:o2].reshape(b, s, ATTN_KV_HEADS, HEAD_DIM).astype(f32), pos)
    v = z[..., o2:o3].reshape(b, s, ATTN_KV_HEADS, HEAD_DIM).astype(f32)
    y_attn = sliding_window_attention(q, k, v, attn_sinks[l]).astype(h.dtype)
    zr = token_shift(z[..., o3:o4], shift_mu[l])
    y_rwkv = rwkv7_time_mix(zr, decay_w0[l], decay_w2[l], iclr_a0[l], iclr_a2[l], gate_g2[l],
                            k_k[l], k_a[l], r_k[l], ln_x_w[l], ln_x_b[l]).astype(h.dtype)
    g_attn = jax.nn.sigmoid(z[..., o4:o5])
    g_rwkv = jax.nn.sigmoid(z[..., o5:])
    merged = g_attn * (y_attn @ proj_attn[l]) + g_rwkv * (y_rwkv @ proj_rwkv[l])
    h = (h + merged @ w_out[l]).reshape(t, D_MODEL)

    xn2, idx, gate = peer_route(h, norm_ffn_w[l], peer_wq[l], peer_subkeys[l])
    act = peer_u(idx, xn2.reshape(t, SUBLANES, LANES), gate, pack_table(peer_u_tab[l]))
    peer_out = peer_v(idx, act, pack_table(peer_v_tab[l])).reshape(t, D_MODEL)
    return add_norm(h, peer_out, norm_final_w).reshape(b, s, D_MODEL)
```

```python
import functools

import jax, jax.numpy as jnp
from jax import lax
from jax.experimental import pallas as pl
from jax.experimental.pallas import tpu as pltpu

D_MODEL = 1024
DEPTH = 1

HEAD_DIM = 64
ATTN_HEADS = 8
ATTN_KV_HEADS = 2
ATTN_GROUP = ATTN_HEADS // ATTN_KV_HEADS
ATTN_DIM = ATTN_HEADS * HEAD_DIM
KV_DIM = ATTN_KV_HEADS * HEAD_DIM
WINDOW = 128
BLOCK = WINDOW
ROPE_THETA = 10000.0

RWKV_HEADS = 8
RWKV_HEAD = 64
RWKV_DIM = RWKV_HEADS * RWKV_HEAD
DECAY_LORA = 64
ICLR_LORA = 64
GATE_LORA = 160
RWKV_GN_EPS = 64e-5

PEER_HEADS = 8
N_KEYS = 128
PEER_HALF = 64
PEER_TOPK = 16
PEER_SEL = PEER_HEADS * PEER_TOPK

NORM_EPS = 1e-5
NEG_INF = -1e30

LANES = 128
SUBLANES = 8
VMEM_LIMIT_BYTES = 56 * 1024 * 1024

ROW_WORDS = D_MODEL // 2
ROW_SUBLANES = ROW_WORDS // LANES
NEG_FILL = -3.0e38
HI = lax.Precision.HIGHEST
RWKV_VGROUPS = RWKV_HEAD // SUBLANES

QKV_COLS = ATTN_DIM + 2 * KV_DIM
LORA_PAD = (LANES, LANES, 2 * LANES)
RWKV_COLS = 3 * RWKV_DIM + sum(LORA_PAD)
GATE_COLS = 2 * D_MODEL


def _pad_cols(w, width):
    return jnp.pad(w, ((0, 0), (0, width - w.shape[1])))


def _pad_rows(w, height):
    return jnp.pad(w, ((0, height - w.shape[0]), (0, 0)))


def layout_in_proj(w_in, shift_mu):
    o1 = QKV_COLS
    o_r = o1 + 3 * RWKV_DIM
    o_w = o_r + DECAY_LORA
    o_a = o_w + ICLR_LORA
    o_g = o_a + GATE_LORA
    parts = [w_in[:, :o_r], _pad_cols(w_in[:, o_r:o_w], LORA_PAD[0]), _pad_cols(w_in[:, o_w:o_a], LORA_PAD[1]),
             _pad_cols(w_in[:, o_a:o_g], LORA_PAD[2]), w_in[:, o_g:]]
    mu = shift_mu.reshape(1, -1)
    m_r = 3 * RWKV_DIM
    mu_parts = [mu[:, :m_r], _pad_cols(mu[:, m_r:m_r + DECAY_LORA], LORA_PAD[0]),
                _pad_cols(mu[:, m_r + DECAY_LORA:m_r + DECAY_LORA + ICLR_LORA], LORA_PAD[1]),
                _pad_cols(mu[:, m_r + DECAY_LORA + ICLR_LORA:], LORA_PAD[2])]
    return jnp.concatenate(parts, axis=1), jnp.concatenate(mu_parts, axis=1)


def _in_proj_body(x_ref, nw_ref, w_ref, qkv_ref, zr_ref, gate_ref):
    xf = x_ref[...]
    xn = xf * lax.rsqrt(jnp.mean(xf * xf, axis=-1, keepdims=True) + NORM_EPS) * nw_ref[...]
    z = jnp.dot(xn.astype(jnp.bfloat16), w_ref[...], preferred_element_type=jnp.float32)
    qkv_ref[...] = z[:, :QKV_COLS]
    zr_ref[...] = z[:, QKV_COLS:QKV_COLS + RWKV_COLS]
    gate_ref[...] = jax.nn.sigmoid(z[:, QKV_COLS + RWKV_COLS:]).astype(jnp.bfloat16)


def in_proj(x, norm_w, w_lay, *, tm=256):
    t, d = x.shape
    n = w_lay.shape[1]
    return pl.pallas_call(
        _in_proj_body,
        grid=(t // tm,),
        in_specs=[pl.BlockSpec((tm, d), lambda i: (i, 0)), pl.BlockSpec((1, d), lambda i: (0, 0)),
                  pl.BlockSpec((d, n), lambda i: (0, 0))],
        out_specs=[pl.BlockSpec((tm, QKV_COLS), lambda i: (i, 0)), pl.BlockSpec((tm, RWKV_COLS), lambda i: (i, 0)),
                   pl.BlockSpec((tm, GATE_COLS), lambda i: (i, 0))],
        out_shape=[jax.ShapeDtypeStruct((t, QKV_COLS), jnp.float32), jax.ShapeDtypeStruct((t, RWKV_COLS), jnp.float32),
                   jax.ShapeDtypeStruct((t, GATE_COLS), jnp.bfloat16)],
        compiler_params=pltpu.CompilerParams(dimension_semantics=("arbitrary",), vmem_limit_bytes=VMEM_LIMIT_BYTES),
        name="in_proj",
    )(x, norm_w.reshape(1, d), w_lay.astype(jnp.bfloat16))


def rope_tables(s):
    inv = ROPE_THETA ** (-jnp.arange(0, HEAD_DIM, 2, dtype=jnp.float32) / HEAD_DIM)
    ang = jnp.arange(s, dtype=jnp.float32)[:, None] * inv[None, :]
    cos = jnp.concatenate([jnp.cos(ang), jnp.cos(ang)], axis=-1)
    sin = jnp.concatenate([-jnp.sin(ang), jnp.sin(ang)], axis=-1)
    reps = LANES // HEAD_DIM
    return jnp.tile(cos, (1, reps)), jnp.tile(sin, (1, reps))


def _rope(x, cos, sin, first_half):
    rot = jnp.where(first_half, pltpu.roll(x, LANES - HEAD_DIM // 2, 1), pltpu.roll(x, HEAD_DIM // 2, 1))
    return x * cos + rot * sin


def _attn_body(cur_ref, prev_ref, cos_ref, sin_ref, cosp_ref, sinp_ref, sink_ref, o_ref):
    n = pl.program_id(1)
    lane = lax.broadcasted_iota(jnp.int32, (BLOCK, LANES), 1)
    first_half = (lane % HEAD_DIM) < HEAD_DIM // 2
    low = lane < HEAD_DIM
    cos, sin = cos_ref[...], sin_ref[...]
    cur = cur_ref[0]
    prev = prev_ref[0]
    k_cur = _rope(cur[:, ATTN_DIM:ATTN_DIM + KV_DIM], cos, sin, first_half)
    k_prev = _rope(prev[:, ATTN_DIM:ATTN_DIM + KV_DIM], cosp_ref[...], sinp_ref[...], first_half)
    keys = jnp.concatenate([k_prev, k_cur], axis=0)
    vals = jnp.concatenate([prev[:, ATTN_DIM + KV_DIM:], cur[:, ATTN_DIM + KV_DIM:]], axis=0)
    keys_sw = pltpu.roll(keys, HEAD_DIM, 1)
    vals_sw = pltpu.roll(vals, HEAD_DIM, 1)
    low2 = lax.broadcasted_iota(jnp.int32, (2 * BLOCK, LANES), 1) < HEAD_DIM
    qi = lax.broadcasted_iota(jnp.int32, (BLOCK, 2 * BLOCK), 0) + BLOCK
    kj = lax.broadcasted_iota(jnp.int32, (BLOCK, 2 * BLOCK), 1)
    diff = qi - kj
    mask = (diff >= 0) & (diff < WINDOW) & ((kj >= BLOCK) | (n > 0))
    scale = HEAD_DIM ** -0.5
    outs = []
    for j in range(ATTN_HEADS // 2):
        g = (2 * j) // ATTN_GROUP
        kg = (jnp.where(low2, keys, keys_sw) if g == 0 else jnp.where(low2, keys_sw, keys)).astype(jnp.bfloat16)
        vg = (jnp.where(low2, vals, vals_sw) if g == 0 else jnp.where(low2, vals_sw, vals)).astype(jnp.bfloat16)
        qg = _rope(cur[:, j * LANES:(j + 1) * LANES], cos, sin, first_half) * scale
        halves = []
        for half in range(2):
            qh = jnp.where(low if half == 0 else ~low, qg, 0.0).astype(jnp.bfloat16)
            logits = lax.dot_general(qh, kg, (((1,), (1,)), ((), ())), preferred_element_type=jnp.float32)
            logits = jnp.where(mask, logits, NEG_INF)
            sink = sink_ref[2 * j + half]
            m = jnp.maximum(jnp.max(logits, axis=-1, keepdims=True), sink)
            p = jnp.exp(logits - m)
            denom = jnp.sum(p, axis=-1, keepdims=True) + jnp.exp(sink - m)
            o = jnp.dot(p.astype(jnp.bfloat16), vg, preferred_element_type=jnp.float32)
            halves.append(o / denom)
        outs.append(jnp.where(low, halves[0], halves[1]))
    o_ref[0] = jnp.concatenate(outs, axis=1).astype(o_ref.dtype)


def swa_attention(qkv, sinks, b, s):
    nb = s // BLOCK
    cos, sin = rope_tables(s)
    qkv3 = qkv.reshape(b, s, QKV_COLS)
    cur_spec = pl.BlockSpec((1, BLOCK, QKV_COLS), lambda bi, n: (bi, n, 0))
    prev_spec = pl.BlockSpec((1, BLOCK, QKV_COLS), lambda bi, n: (bi, jnp.maximum(n - 1, 0), 0))
    tab = pl.BlockSpec((BLOCK, LANES), lambda bi, n: (n, 0))
    tabp = pl.BlockSpec((BLOCK, LANES), lambda bi, n: (jnp.maximum(n - 1, 0), 0))
    out = pl.pallas_call(
        _attn_body,
        grid=(b, nb),
        in_specs=[cur_spec, prev_spec, tab, tab, tabp, tabp, pl.BlockSpec(memory_space=pltpu.SMEM)],
        out_specs=pl.BlockSpec((1, BLOCK, ATTN_DIM), lambda bi, n: (bi, n, 0)),
        out_shape=jax.ShapeDtypeStruct((b, s, ATTN_DIM), jnp.bfloat16),
        compiler_params=pltpu.CompilerParams(dimension_semantics=("arbitrary", "arbitrary"),
                                             vmem_limit_bytes=VMEM_LIMIT_BYTES),
        name="swa_attention",
    )(qkv3, qkv3, cos, sin, cos, sin, sinks.astype(jnp.float32))
    return out.reshape(b * s, ATTN_DIM)


def head_block_ones():
    i = jnp.arange(RWKV_DIM) // RWKV_HEAD
    return (i[:, None] == i[None, :]).astype(jnp.float32)


def _softplus(x):
    return jnp.maximum(x, 0.0) + jnp.log1p(jnp.exp(-jnp.abs(x)))


def _rwkv_prep_body(z_ref, zp_ref, mu_ref, w0_ref, w2_ref, a0_ref, a2_ref, g2_ref, kk_w_ref, ka_ref, rk_ref, ones_ref,
                    r_ref, dec_ref, k_ref, v_ref, kk_ref, nb_ref, g_ref, bonus_ref, *, ts):
    z = z_ref[0]
    row = lax.broadcasted_iota(jnp.int32, (ts, 1), 0)
    last_prev = jnp.where(pl.program_id(1) > 0, zp_ref[0][SUBLANES - 1:SUBLANES, :], 0.0)
    z_prev = jnp.where(row == 0, last_prev, pltpu.roll(z, 1, 0))
    zs = z + mu_ref[...] * (z_prev - z)
    c = RWKV_DIM
    r, k, v = zs[:, :c], zs[:, c:2 * c], zs[:, 2 * c:3 * c]
    o = 3 * c
    w_lo = zs[:, o:o + LORA_PAD[0]]
    a_lo = zs[:, o + LORA_PAD[0]:o + LORA_PAD[0] + LORA_PAD[1]]
    g_lo = zs[:, o + LORA_PAD[0] + LORA_PAD[1]:]
    dot = lambda x, w: jnp.dot(x, w, preferred_element_type=jnp.float32, precision=HI)
    w = -_softplus(-(w0_ref[...] + dot(jnp.tanh(w_lo), w2_ref[...]))) - 0.5
    dec_ref[...] = jnp.exp(-jnp.exp(w))
    a = jax.nn.sigmoid(a0_ref[...] + dot(a_lo, a2_ref[...]))
    g_ref[...] = dot(jax.nn.sigmoid(g_lo), g2_ref[...])
    kk = k * kk_w_ref[...]
    kk = kk / jnp.maximum(jnp.sqrt(dot(kk * kk, ones_ref[...])), 1e-12)
    kmod = k * (1.0 + (a - 1.0) * ka_ref[...])
    r_ref[...] = r
    k_ref[...] = kmod
    v_ref[...] = v
    kk_ref[...] = kk
    nb_ref[...] = -kk * a
    bonus_ref[...] = dot(r * kmod * rk_ref[...], ones_ref[...]) * v


def rwkv_prep(zr, mu_lay, w0, w2, a0, a2, g2, k_k, k_a, r_k, b, s, *, ts=256):
    t = b * s
    zr3 = zr.reshape(b, s, RWKV_COLS)
    row = lambda x: x.reshape(1, -1)
    w2p, a2p, g2p = _pad_rows(w2, LORA_PAD[0]), _pad_rows(a2, LORA_PAD[1]), _pad_rows(g2, LORA_PAD[2])
    const = lambda shape: pl.BlockSpec(shape, lambda bi, si: (0,) * len(shape))
    nblk = s // ts
    out_spec = pl.BlockSpec((ts, RWKV_DIM), lambda bi, si: (bi * nblk + si, 0))
    return pl.pallas_call(
        functools.partial(_rwkv_prep_body, ts=ts),
        grid=(b, nblk),
        in_specs=[pl.BlockSpec((1, ts, RWKV_COLS), lambda bi, si: (bi, si, 0)),
                  pl.BlockSpec((1, SUBLANES, RWKV_COLS), lambda bi, si: (bi, jnp.maximum(si * (ts // SUBLANES) - 1, 0), 0)),
                  const((1, RWKV_COLS)), const((1, RWKV_DIM)), const(w2p.shape), const((1, RWKV_DIM)), const(a2p.shape),
                  const(g2p.shape), const((1, RWKV_DIM)), const((1, RWKV_DIM)), const((1, RWKV_DIM)),
                  const((RWKV_DIM, RWKV_DIM))],
        out_specs=[out_spec] * 8,
        out_shape=[jax.ShapeDtypeStruct((t, RWKV_DIM), jnp.float32)] * 8,
        compiler_params=pltpu.CompilerParams(dimension_semantics=("arbitrary", "arbitrary"),
                                             vmem_limit_bytes=VMEM_LIMIT_BYTES),
        name="rwkv_prep",
    )(zr3, zr3, mu_lay, row(w0), w2p, row(a0), a2p, g2p, row(k_k), row(k_a), row(r_k), head_block_ones())


SCAN_PARTIALS = 4


def _tree_sum(parts):
    while len(parts) > 1:
        parts = [parts[i] + parts[i + 1] for i in range(0, len(parts), 2)]
    return parts[0]


def _scan_body(r_ref, w_ref, k_ref, v_ref, kk_ref, nb_ref, y_ref, *s_refs, tc):
    @pl.when(pl.program_id(1) == 0)
    def _():
        for s_ref in s_refs:
            s_ref[...] = jnp.zeros_like(s_ref)

    def step(t, carry):
        for g, s_ref in enumerate(s_refs):
            rows = slice(g * SUBLANES, (g + 1) * SUBLANES)
            acc = [None] * SCAN_PARTIALS
            for k in range(RWKV_HEAD):
                p = s_ref[k] * kk_ref[t, pl.ds(k, 1), :]
                acc[k % SCAN_PARTIALS] = p if acc[k % SCAN_PARTIALS] is None else acc[k % SCAN_PARTIALS] + p
            sa = _tree_sum(acc)
            vv = v_ref[t, rows, :]
            yacc = [None] * SCAN_PARTIALS
            for k in range(RWKV_HEAD):
                s_new = s_ref[k] * w_ref[t, pl.ds(k, 1), :] + sa * nb_ref[t, pl.ds(k, 1), :] + vv * k_ref[t, pl.ds(k, 1), :]
                s_ref[k] = s_new
                p = s_new * r_ref[t, pl.ds(k, 1), :]
                yacc[k % SCAN_PARTIALS] = p if yacc[k % SCAN_PARTIALS] is None else yacc[k % SCAN_PARTIALS] + p
            y_ref[t, rows, :] = _tree_sum(yacc)
        return carry

    lax.fori_loop(0, tc, step, 0)


def rwkv_scan(r, w, k, v, kk, nb, *, tc=32):
    s, n, c = r.shape
    spec = pl.BlockSpec((tc, n, LANES), lambda ci, ti: (ti, 0, ci))
    return pl.pallas_call(
        functools.partial(_scan_body, tc=tc),
        grid=(c // LANES, s // tc),
        in_specs=[spec] * 6,
        out_specs=spec,
        out_shape=jax.ShapeDtypeStruct((s, n, c), jnp.float32),
        scratch_shapes=[pltpu.VMEM((n, SUBLANES, LANES), jnp.float32)] * RWKV_VGROUPS,
        compiler_params=pltpu.CompilerParams(dimension_semantics=("arbitrary", "arbitrary"),
                                             vmem_limit_bytes=VMEM_LIMIT_BYTES),
        name="rwkv_scan",
    )(r, w, k, v, kk, nb)


def _merge_body(x_ref, y_ref, g_ref, bonus_ref, ya_ref, gate_ref, lnw_ref, lnb_ref, ones_ref, pa_ref, pb_ref, wo_ref, h_ref):
    dot_hi = lambda x, w: jnp.dot(x, w, preferred_element_type=jnp.float32, precision=HI)
    dot_bf = lambda x, w: jnp.dot(x.astype(jnp.bfloat16), w, preferred_element_type=jnp.float32)
    y = y_ref[...]
    inv_n = 1.0 / RWKV_HEAD
    mean = dot_hi(y, ones_ref[...]) * inv_n
    yc = y - mean
    var = dot_hi(yc * yc, ones_ref[...]) * inv_n
    yn = yc * lax.rsqrt(var + RWKV_GN_EPS) * lnw_ref[...] + lnb_ref[...] + bonus_ref[...]
    y_rwkv = yn * g_ref[...]
    gates = gate_ref[...].astype(jnp.float32)
    merged = gates[:, :D_MODEL] * dot_bf(ya_ref[...], pa_ref[...]) + gates[:, D_MODEL:] * dot_bf(y_rwkv, pb_ref[...])
    h_ref[...] = x_ref[...] + dot_bf(merged, wo_ref[...])


def merge_out(x, y, g, bonus, y_attn, gates, ln_w, ln_b, proj_attn, proj_rwkv, w_out, *, tm=256):
    t, d = x.shape
    row = lambda v: v.reshape(1, -1)
    tok = lambda w: pl.BlockSpec((tm, w), lambda i: (i, 0))
    const = lambda shape: pl.BlockSpec(shape, lambda i: (0,) * len(shape))
    bf = lambda w: w.astype(jnp.bfloat16)
    return pl.pallas_call(
        _merge_body,
        grid=(t // tm,),
        in_specs=[tok(d), tok(RWKV_DIM), tok(RWKV_DIM), tok(RWKV_DIM), tok(ATTN_DIM), tok(GATE_COLS),
                  const((1, RWKV_DIM)), const((1, RWKV_DIM)), const((RWKV_DIM, RWKV_DIM)),
                  const(proj_attn.shape), const(proj_rwkv.shape), const(w_out.shape)],
        out_specs=tok(d),
        out_shape=jax.ShapeDtypeStruct((t, d), jnp.float32),
        compiler_params=pltpu.CompilerParams(dimension_semantics=("arbitrary",), vmem_limit_bytes=VMEM_LIMIT_BYTES),
        name="merge_out",
    )(x, y, g, bonus, y_attn, gates, row(ln_w), row(ln_b), head_block_ones(), bf(proj_attn), bf(proj_rwkv), bf(w_out))


def _top16(a, n_rows):
    rid = lax.broadcasted_iota(jnp.int32, (n_rows, LANES), 0)
    vals, rows = [], []
    for _ in range(PEER_TOPK):
        m = jnp.max(a, axis=0, keepdims=True)
        r = jnp.min(jnp.where(a == m, rid, n_rows), axis=0, keepdims=True)
        vals.append(m)
        rows.append(r)
        a = jnp.where(rid == r, NEG_FILL, a)
    return jnp.concatenate(vals, axis=0), jnp.concatenate(rows, axis=0)


def _select_rows(pos, table):
    out = jnp.zeros(pos.shape, table.dtype)
    for a in range(PEER_TOPK):
        out = jnp.where(pos == a, table[a:a + 1, :], out)
    return out


def _peer_route_body(h_ref, nw_ref, wq_ref, sk_ref, xn_ref, idx_ref, gate_ref, q_ref, idx_t_ref, gate_t_ref, *, tm):
    xf = h_ref[...]
    xn = xf * lax.rsqrt(jnp.mean(xf * xf, axis=-1, keepdims=True) + NORM_EPS) * nw_ref[...]
    xn_ref[...] = xn
    q_ref[...] = jnp.dot(xn, wq_ref[...], preferred_element_type=jnp.float32, precision=HI)

    def head(i, carry):
        c = i // PEER_HEADS
        h = i % PEER_HEADS
        tok0 = pl.multiple_of(c * LANES, LANES)
        col0 = pl.multiple_of(h * 2 * PEER_HALF, 2 * PEER_HALF)
        qh = q_ref[pl.ds(tok0, LANES), pl.ds(col0, 2 * PEER_HALF)]
        sc = lax.dot_general(sk_ref[h], qh, (((1,), (1,)), ((), ())),
                             preferred_element_type=jnp.float32, precision=HI)
        s1, i1 = _top16(sc[:N_KEYS], N_KEYS)
        s2, i2 = _top16(sc[N_KEYS:], N_KEYS)
        cand = jnp.concatenate([s1[a:a + 1, :] + s2 for a in range(PEER_TOPK)], axis=0)
        top, pos = _top16(cand, PEER_TOPK * PEER_TOPK)
        e = _select_rows(pos >> 4, i1) * N_KEYS + _select_rows(pos & 15, i2)
        p = jnp.exp(top - top[0:1, :])
        row0 = pl.multiple_of(h * PEER_TOPK, PEER_TOPK)
        idx_t_ref[pl.ds(row0, PEER_TOPK), pl.ds(tok0, LANES)] = e * ROW_SUBLANES
        gate_t_ref[pl.ds(row0, PEER_TOPK), pl.ds(tok0, LANES)] = p / jnp.sum(p, axis=0, keepdims=True)
        return carry

    lax.fori_loop(0, (tm // LANES) * PEER_HEADS, head, 0)
    for c in range(tm // LANES):
        idx_ref[c * LANES:(c + 1) * LANES, :] = idx_t_ref[:, c * LANES:(c + 1) * LANES].T
        gate_ref[c * LANES:(c + 1) * LANES, :] = gate_t_ref[:, c * LANES:(c + 1) * LANES].T


def _block_diag_subkeys(subkeys):
    z = jnp.zeros_like(subkeys[:, 0])
    top = jnp.concatenate([subkeys[:, 0], z], axis=-1)
    bot = jnp.concatenate([z, subkeys[:, 1]], axis=-1)
    return jnp.concatenate([top, bot], axis=1)


def peer_route(h, norm_w, wq, subkeys, *, tm=256):
    t, d = h.shape
    sk = _block_diag_subkeys(subkeys)
    return pl.pallas_call(
        functools.partial(_peer_route_body, tm=tm),
        grid=(t // tm,),
        in_specs=[
            pl.BlockSpec((tm, d), lambda i: (i, 0)),
            pl.BlockSpec((1, d), lambda i: (0, 0)),
            pl.BlockSpec(wq.shape, lambda i: (0, 0)),
            pl.BlockSpec(sk.shape, lambda i: (0, 0, 0)),
        ],
        out_specs=[pl.BlockSpec((tm, d), lambda i: (i, 0)),
                   pl.BlockSpec((tm, PEER_SEL), lambda i: (i, 0)),
                   pl.BlockSpec((tm, PEER_SEL), lambda i: (i, 0))],
        out_shape=[jax.ShapeDtypeStruct((t, d), jnp.float32),
                   jax.ShapeDtypeStruct((t, PEER_SEL), jnp.int32),
                   jax.ShapeDtypeStruct((t, PEER_SEL), jnp.float32)],
        scratch_shapes=[pltpu.VMEM((tm, d), jnp.float32), pltpu.VMEM((PEER_SEL, tm), jnp.int32),
                        pltpu.VMEM((PEER_SEL, tm), jnp.float32)],
        compiler_params=pltpu.CompilerParams(dimension_semantics=("arbitrary",), vmem_limit_bytes=VMEM_LIMIT_BYTES),
        name="peer_route",
    )(h, norm_w.reshape(1, d), wq, sk)


def pack_table(tab):
    n, d = tab.shape
    b = lax.bitcast_convert_type(tab.astype(jnp.bfloat16), jnp.uint16).astype(jnp.uint32)
    w = b[:, : d // 2] | (b[:, d // 2:] << 16)
    return lax.bitcast_convert_type(w, jnp.int32).reshape(n * ROW_SUBLANES, LANES)


def _unpack(w):
    lo = lax.bitcast_convert_type(w << 16, jnp.float32)
    hi = lax.bitcast_convert_type(w & jnp.int32(-65536), jnp.float32)
    return lo, hi


def _load_table(tab_hbm, tab_vmem, sem):
    @pl.when(pl.program_id(0) == 0)
    def _():
        cp = pltpu.make_async_copy(tab_hbm, tab_vmem, sem)
        cp.start()
        cp.wait()


def _rowsum8(ps, sub):
    f = [jnp.concatenate([ps[0], ps[4]], axis=0), jnp.concatenate([ps[2], ps[6]], axis=0),
         jnp.concatenate([ps[1], ps[5]], axis=0), jnp.concatenate([ps[3], ps[7]], axis=0)]
    even2 = (sub & 2) == 0
    even1 = (sub & 1) == 0

    def merge(x, y, stride, keep_x):
        u = x + pltpu.roll(x, SUBLANES - stride, 0)
        v = y + pltpu.roll(y, stride, 0)
        return jnp.where(keep_x, u, v)

    m01 = merge(f[0], f[1], 2, even2)
    m23 = merge(f[2], f[3], 2, even2)
    return merge(m01, m23, 1, even1)


def _peer_u_body(idx_ref, x_ref, gate_ref, tab_hbm, act_ref, tab_vmem, q_ref, sem, *, tb):
    _load_table(tab_hbm, tab_vmem, sem)
    sub = lax.broadcasted_iota(jnp.int32, (SUBLANES, LANES), 0)

    def tok(t, carry):
        xt = x_ref[t]
        xl = xt[0:ROW_SUBLANES]
        xh = xt[ROW_SUBLANES:]
        rows = []
        for g in range(PEER_SEL // 8):
            ps = []
            for j in range(8):
                e = idx_ref[t, g * 8 + j]
                lo, hi = _unpack(tab_vmem[pl.ds(pl.multiple_of(e, ROW_SUBLANES), ROW_SUBLANES), :])
                ps.append(lo * xl + hi * xh)
            rows.append(_rowsum8(ps, sub))
        q_ref[t] = jnp.concatenate(rows, axis=0)
        return carry

    lax.fori_loop(0, tb, tok, 0)

    def lane_sums(c, carry):
        for j in range(4):
            t = c * 4 + j
            act_ref[pl.ds(t, 1), :] = jnp.sum(q_ref[t].T, axis=0, keepdims=True)
        return carry

    lax.fori_loop(0, tb // 4, lane_sums, 0)
    hp = act_ref[...]
    act_ref[...] = 0.5 * hp * (1.0 + lax.erf(hp * (2.0 ** -0.5))) * gate_ref[...]


def peer_u(idx, x, gate, tab, *, tb=64):
    t = idx.shape[0]
    return pl.pallas_call(
        functools.partial(_peer_u_body, tb=tb),
        grid=(t // tb,),
        in_specs=[
            pl.BlockSpec((tb, PEER_SEL), lambda i: (i, 0), memory_space=pltpu.SMEM),
            pl.BlockSpec((tb, SUBLANES, LANES), lambda i: (i, 0, 0)),
            pl.BlockSpec((tb, PEER_SEL), lambda i: (i, 0)),
            pl.BlockSpec(memory_space=pl.ANY),
        ],
        out_specs=pl.BlockSpec((tb, PEER_SEL), lambda i: (i, 0)),
        out_shape=jax.ShapeDtypeStruct((t, PEER_SEL), jnp.float32),
        scratch_shapes=[pltpu.VMEM(tab.shape, jnp.int32), pltpu.VMEM((tb, PEER_SEL, LANES), jnp.float32),
                        pltpu.SemaphoreType.DMA],
        compiler_params=pltpu.CompilerParams(dimension_semantics=("arbitrary",), vmem_limit_bytes=VMEM_LIMIT_BYTES),
        name="peer_u",
    )(idx, x, gate, tab)


def _peer_v_body(idx_ref, act_ref, tab_hbm, o_ref, tab_vmem, sem, *, tb):
    _load_table(tab_hbm, tab_vmem, sem)

    def tok(t, carry):
        alo = jnp.zeros((ROW_SUBLANES, LANES), jnp.float32)
        ahi = jnp.zeros((ROW_SUBLANES, LANES), jnp.float32)
        for k in range(PEER_SEL):
            e = idx_ref[t, k]
            s = act_ref[t, k]
            lo, hi = _unpack(tab_vmem[pl.ds(pl.multiple_of(e, ROW_SUBLANES), ROW_SUBLANES), :])
            alo = alo + s * lo
            ahi = ahi + s * hi
        o_ref[t] = jnp.concatenate([alo, ahi], axis=0)
        return carry

    lax.fori_loop(0, tb, tok, 0)


def peer_v(idx, act, tab, *, tb=64):
    t = idx.shape[0]
    return pl.pallas_call(
        functools.partial(_peer_v_body, tb=tb),
        grid=(t // tb,),
        in_specs=[
            pl.BlockSpec((tb, PEER_SEL), lambda i: (i, 0), memory_space=pltpu.SMEM),
            pl.BlockSpec((tb, PEER_SEL), lambda i: (i, 0), memory_space=pltpu.SMEM),
            pl.BlockSpec(memory_space=pl.ANY),
        ],
        out_specs=pl.BlockSpec((tb, SUBLANES, LANES), lambda i: (i, 0, 0)),
        out_shape=jax.ShapeDtypeStruct((t, SUBLANES, LANES), jnp.float32),
        scratch_shapes=[pltpu.VMEM(tab.shape, jnp.int32), pltpu.SemaphoreType.DMA],
        compiler_params=pltpu.CompilerParams(dimension_semantics=("arbitrary",), vmem_limit_bytes=VMEM_LIMIT_BYTES),
        name="peer_v",
    )(idx, act, tab)


def _add_norm_body(h_ref, p_ref, w_ref, o_ref):
    xf = h_ref[...] + p_ref[...]
    y = xf * lax.rsqrt(jnp.mean(xf * xf, axis=-1, keepdims=True) + NORM_EPS)
    o_ref[...] = y * w_ref[...]


def add_norm(h, p, w, *, tm=1024):
    t, d = h.shape
    return pl.pallas_call(
        _add_norm_body,
        grid=(t // tm,),
        in_specs=[pl.BlockSpec((tm, d), lambda i: (i, 0)), pl.BlockSpec((tm, d), lambda i: (i, 0)),
                  pl.BlockSpec((1, d), lambda i: (0, 0))],
        out_specs=pl.BlockSpec((tm, d), lambda i: (i, 0)),
        out_shape=jax.ShapeDtypeStruct((t, d), h.dtype),
        compiler_params=pltpu.CompilerParams(dimension_semantics=("arbitrary",)),
        name="add_norm",
    )(h, p, w.reshape(1, d))


def kernel(x, norm_mix_w, w_in, shift_mu, attn_sinks, decay_w0, decay_w2, iclr_a0, iclr_a2,
           gate_g2, k_k, k_a, r_k, ln_x_w, ln_x_b, proj_attn, proj_rwkv, w_out, norm_ffn_w,
           peer_wq, peer_subkeys, peer_u_tab, peer_v_tab, norm_final_w):
    b, s, d = x.shape
    t = b * s
    h = x.reshape(t, d)
    for l in range(DEPTH):
        w_lay, mu_lay = layout_in_proj(w_in[l], shift_mu[l])
        qkv, zr, gates = in_proj(h, norm_mix_w[l], w_lay)
        y_attn = swa_attention(qkv, attn_sinks[l], b, s)
        r, dec, k, v, kk, nb, g, bonus = rwkv_prep(zr, mu_lay, decay_w0[l], decay_w2[l], iclr_a0[l], iclr_a2[l],
                                                   gate_g2[l], k_k[l], k_a[l], r_k[l].reshape(-1), b, s)
        to_chains = lambda a: jnp.transpose(a.reshape(b, s, RWKV_HEADS, RWKV_HEAD), (1, 3, 0, 2)).reshape(
            s, RWKV_HEAD, b * RWKV_HEADS)
        y = rwkv_scan(to_chains(r), to_chains(dec), to_chains(k), to_chains(v), to_chains(kk), to_chains(nb))
        y = jnp.transpose(y.reshape(s, RWKV_HEAD, b, RWKV_HEADS), (2, 0, 3, 1)).reshape(t, RWKV_DIM)
        h = merge_out(h, y, g, bonus, y_attn, gates, ln_x_w[l], ln_x_b[l], proj_attn[l], proj_rwkv[l], w_out[l])
        xn2, idx, gate = peer_route(h, norm_ffn_w[l], peer_wq[l], peer_subkeys[l])
        act = peer_u(idx, xn2.reshape(t, SUBLANES, LANES), gate, pack_table(peer_u_tab[l]))
        peer_out = peer_v(idx, act, pack_table(peer_v_tab[l])).reshape(t, d)
        if l + 1 < DEPTH:
            h = h + peer_out
    return add_norm(h, peer_out, norm_final_w).reshape(b, s, d)
```

```python
import functools

import jax, jax.numpy as jnp
from jax import lax
from jax.experimental import pallas as pl
from jax.experimental.pallas import tpu as pltpu

D_MODEL = 1024
DEPTH = 1

HEAD_DIM = 64
ATTN_HEADS = 8
ATTN_KV_HEADS = 2
ATTN_GROUP = ATTN_HEADS // ATTN_KV_HEADS
ATTN_DIM = ATTN_HEADS * HEAD_DIM
KV_DIM = ATTN_KV_HEADS * HEAD_DIM
WINDOW = 128
BLOCK = WINDOW
ROPE_THETA = 10000.0

RWKV_HEADS = 8
RWKV_HEAD = 64
RWKV_DIM = RWKV_HEADS * RWKV_HEAD
DECAY_LORA = 64
ICLR_LORA = 64
GATE_LORA = 160
RWKV_GN_EPS = 64e-5

PEER_HEADS = 8
N_KEYS = 128
PEER_HALF = 64
PEER_TOPK = 16
PEER_SEL = PEER_HEADS * PEER_TOPK

NORM_EPS = 1e-5
NEG_INF = -1e30

LANES = 128
SUBLANES = 8
VMEM_LIMIT_BYTES = 56 * 1024 * 1024

ROW_WORDS = D_MODEL // 2
ROW_SUBLANES = ROW_WORDS // LANES
NEG_FILL = -3.0e38
HI = lax.Precision.HIGHEST
RWKV_VGROUPS = RWKV_HEAD // SUBLANES

QKV_COLS = ATTN_DIM + 2 * KV_DIM
LORA_PAD = (LANES, LANES, 2 * LANES)
RWKV_COLS = 3 * RWKV_DIM + sum(LORA_PAD)
GATE_COLS = 2 * D_MODEL


def _pad_cols(w, width):
    return jnp.pad(w, ((0, 0), (0, width - w.shape[1])))


def _pad_rows(w, height):
    return jnp.pad(w, ((0, height - w.shape[0]), (0, 0)))


def layout_in_proj(w_in, shift_mu):
    o1 = QKV_COLS
    o_r = o1 + 3 * RWKV_DIM
    o_w = o_r + DECAY_LORA
    o_a = o_w + ICLR_LORA
    o_g = o_a + GATE_LORA
    parts = [w_in[:, :o_r], _pad_cols(w_in[:, o_r:o_w], LORA_PAD[0]), _pad_cols(w_in[:, o_w:o_a], LORA_PAD[1]),
             _pad_cols(w_in[:, o_a:o_g], LORA_PAD[2]), w_in[:, o_g:]]
    mu = shift_mu.reshape(1, -1)
    m_r = 3 * RWKV_DIM
    mu_parts = [mu[:, :m_r], _pad_cols(mu[:, m_r:m_r + DECAY_LORA], LORA_PAD[0]),
                _pad_cols(mu[:, m_r + DECAY_LORA:m_r + DECAY_LORA + ICLR_LORA], LORA_PAD[1]),
                _pad_cols(mu[:, m_r + DECAY_LORA + ICLR_LORA:], LORA_PAD[2])]
    return jnp.concatenate(parts, axis=1), jnp.concatenate(mu_parts, axis=1)


def _in_proj_body(x_ref, nw_ref, w_ref, qkv_ref, zr_ref, gate_ref):
    xf = x_ref[...]
    xn = xf * lax.rsqrt(jnp.mean(xf * xf, axis=-1, keepdims=True) + NORM_EPS) * nw_ref[...]
    z = jnp.dot(xn.astype(jnp.bfloat16), w_ref[...], preferred_element_type=jnp.float32)
    qkv_ref[...] = z[:, :QKV_COLS]
    zr_ref[...] = z[:, QKV_COLS:QKV_COLS + RWKV_COLS]
    gate_ref[...] = jax.nn.sigmoid(z[:, QKV_COLS + RWKV_COLS:]).astype(jnp.bfloat16)


def in_proj(x, norm_w, w_lay, *, tm=256):
    t, d = x.shape
    n = w_lay.shape[1]
    return pl.pallas_call(
        _in_proj_body,
        grid=(t // tm,),
        in_specs=[pl.BlockSpec((tm, d), lambda i: (i, 0)), pl.BlockSpec((1, d), lambda i: (0, 0)),
                  pl.BlockSpec((d, n), lambda i: (0, 0))],
        out_specs=[pl.BlockSpec((tm, QKV_COLS), lambda i: (i, 0)), pl.BlockSpec((tm, RWKV_COLS), lambda i: (i, 0)),
                   pl.BlockSpec((tm, GATE_COLS), lambda i: (i, 0))],
        out_shape=[jax.ShapeDtypeStruct((t, QKV_COLS), jnp.float32), jax.ShapeDtypeStruct((t, RWKV_COLS), jnp.float32),
                   jax.ShapeDtypeStruct((t, GATE_COLS), jnp.bfloat16)],
        compiler_params=pltpu.CompilerParams(dimension_semantics=("arbitrary",), vmem_limit_bytes=VMEM_LIMIT_BYTES),
        name="in_proj",
    )(x, norm_w.reshape(1, d), w_lay.astype(jnp.bfloat16))


def rope_tables(s):
    inv = ROPE_THETA ** (-jnp.arange(0, HEAD_DIM, 2, dtype=jnp.float32) / HEAD_DIM)
    ang = jnp.arange(s, dtype=jnp.float32)[:, None] * inv[None, :]
    cos = jnp.concatenate([jnp.cos(ang), jnp.cos(ang)], axis=-1)
    sin = jnp.concatenate([-jnp.sin(ang), jnp.sin(ang)], axis=-1)
    reps = LANES // HEAD_DIM
    return jnp.tile(cos, (1, reps)), jnp.tile(sin, (1, reps))


def _rope(x, cos, sin, first_half):
    rot = jnp.where(first_half, pltpu.roll(x, LANES - HEAD_DIM // 2, 1), pltpu.roll(x, HEAD_DIM // 2, 1))
    return x * cos + rot * sin


def _attn_body(cur_ref, prev_ref, cos_ref, sin_ref, cosp_ref, sinp_ref, sink_ref, o_ref):
    n = pl.program_id(1)
    lane = lax.broadcasted_iota(jnp.int32, (BLOCK, LANES), 1)
    first_half = (lane % HEAD_DIM) < HEAD_DIM // 2
    low = lane < HEAD_DIM
    cos, sin = cos_ref[...], sin_ref[...]
    cur = cur_ref[0]
    prev = prev_ref[0]
    k_cur = _rope(cur[:, ATTN_DIM:ATTN_DIM + KV_DIM], cos, sin, first_half)
    k_prev = _rope(prev[:, ATTN_DIM:ATTN_DIM + KV_DIM], cosp_ref[...], sinp_ref[...], first_half)
    keys = jnp.concatenate([k_prev, k_cur], axis=0)
    vals = jnp.concatenate([prev[:, ATTN_DIM + KV_DIM:], cur[:, ATTN_DIM + KV_DIM:]], axis=0)
    keys_sw = pltpu.roll(keys, HEAD_DIM, 1)
    vals_sw = pltpu.roll(vals, HEAD_DIM, 1)
    low2 = lax.broadcasted_iota(jnp.int32, (2 * BLOCK, LANES), 1) < HEAD_DIM
    qi = lax.broadcasted_iota(jnp.int32, (BLOCK, 2 * BLOCK), 0) + BLOCK
    kj = lax.broadcasted_iota(jnp.int32, (BLOCK, 2 * BLOCK), 1)
    diff = qi - kj
    mask = (diff >= 0) & (diff < WINDOW) & ((kj >= BLOCK) | (n > 0))
    scale = HEAD_DIM ** -0.5
    outs = []
    for j in range(ATTN_HEADS // 2):
        g = (2 * j) // ATTN_GROUP
        kg = (jnp.where(low2, keys, keys_sw) if g == 0 else jnp.where(low2, keys_sw, keys)).astype(jnp.bfloat16)
        vg = (jnp.where(low2, vals, vals_sw) if g == 0 else jnp.where(low2, vals_sw, vals)).astype(jnp.bfloat16)
        qg = _rope(cur[:, j * LANES:(j + 1) * LANES], cos, sin, first_half) * scale
        halves = []
        for half in range(2):
            qh = jnp.where(low if half == 0 else ~low, qg, 0.0).astype(jnp.bfloat16)
            logits = lax.dot_general(qh, kg, (((1,), (1,)), ((), ())), preferred_element_type=jnp.float32)
            logits = jnp.where(mask, logits, NEG_INF)
            sink = sink_ref[2 * j + half]
            m = jnp.maximum(jnp.max(logits, axis=-1, keepdims=True), sink)
            p = jnp.exp(logits - m)
            denom = jnp.sum(p, axis=-1, keepdims=True) + jnp.exp(sink - m)
            o = jnp.dot(p.astype(jnp.bfloat16), vg, preferred_element_type=jnp.float32)
            halves.append(o / denom)
        outs.append(jnp.where(low, halves[0], halves[1]))
    o_ref[0] = jnp.concatenate(outs, axis=1).astype(o_ref.dtype)


def swa_attention(qkv, sinks, b, s):
    nb = s // BLOCK
    cos, sin = rope_tables(s)
    qkv3 = qkv.reshape(b, s, QKV_COLS)
    cur_spec = pl.BlockSpec((1, BLOCK, QKV_COLS), lambda bi, n: (bi, n, 0))
    prev_spec = pl.BlockSpec((1, BLOCK, QKV_COLS), lambda bi, n: (bi, jnp.maximum(n - 1, 0), 0))
    tab = pl.BlockSpec((BLOCK, LANES), lambda bi, n: (n, 0))
    tabp = pl.BlockSpec((BLOCK, LANES), lambda bi, n: (jnp.maximum(n - 1, 0), 0))
    out = pl.pallas_call(
        _attn_body,
        grid=(b, nb),
        in_specs=[cur_spec, prev_spec, tab, tab, tabp, tabp, pl.BlockSpec(memory_space=pltpu.SMEM)],
        out_specs=pl.BlockSpec((1, BLOCK, ATTN_DIM), lambda bi, n: (bi, n, 0)),
        out_shape=jax.ShapeDtypeStruct((b, s, ATTN_DIM), jnp.bfloat16),
        compiler_params=pltpu.CompilerParams(dimension_semantics=("arbitrary", "arbitrary"),
                                             vmem_limit_bytes=VMEM_LIMIT_BYTES),
        name="swa_attention",
    )(qkv3, qkv3, cos, sin, cos, sin, sinks.astype(jnp.float32))
    return out.reshape(b * s, ATTN_DIM)


def head_block_ones():
    i = jnp.arange(RWKV_DIM) // RWKV_HEAD
    return (i[:, None] == i[None, :]).astype(jnp.float32)


def _softplus(x):
    return jnp.maximum(x, 0.0) + jnp.log1p(jnp.exp(-jnp.abs(x)))


def _rwkv_prep_body(z_ref, zp_ref, mu_ref, w0_ref, w2_ref, a0_ref, a2_ref, g2_ref, kk_w_ref, ka_ref, rk_ref, ones_ref,
                    r_ref, dec_ref, k_ref, v_ref, kk_ref, nb_ref, g_ref, bonus_ref, *, ts):
    z = z_ref[0]
    row = lax.broadcasted_iota(jnp.int32, (ts, 1), 0)
    last_prev = jnp.where(pl.program_id(1) > 0, zp_ref[0][SUBLANES - 1:SUBLANES, :], 0.0)
    z_prev = jnp.where(row == 0, last_prev, pltpu.roll(z, 1, 0))
    zs = z + mu_ref[...] * (z_prev - z)
    c = RWKV_DIM
    r, k, v = zs[:, :c], zs[:, c:2 * c], zs[:, 2 * c:3 * c]
    o = 3 * c
    w_lo = zs[:, o:o + LORA_PAD[0]]
    a_lo = zs[:, o + LORA_PAD[0]:o + LORA_PAD[0] + LORA_PAD[1]]
    g_lo = zs[:, o + LORA_PAD[0] + LORA_PAD[1]:]
    dot = lambda x, w: jnp.dot(x, w, preferred_element_type=jnp.float32, precision=HI)
    w = -_softplus(-(w0_ref[...] + dot(jnp.tanh(w_lo), w2_ref[...]))) - 0.5
    dec_ref[...] = jnp.exp(-jnp.exp(w))
    a = jax.nn.sigmoid(a0_ref[...] + dot(a_lo, a2_ref[...]))
    g_ref[...] = dot(jax.nn.sigmoid(g_lo), g2_ref[...])
    kk = k * kk_w_ref[...]
    kk = kk / jnp.maximum(jnp.sqrt(dot(kk * kk, ones_ref[...])), 1e-12)
    kmod = k * (1.0 + (a - 1.0) * ka_ref[...])
    r_ref[...] = r
    k_ref[...] = kmod
    v_ref[...] = v
    kk_ref[...] = kk
    nb_ref[...] = -kk * a
    bonus_ref[...] = dot(r * kmod * rk_ref[...], ones_ref[...]) * v


def rwkv_prep(zr, mu_lay, w0, w2, a0, a2, g2, k_k, k_a, r_k, b, s, *, ts=256):
    t = b * s
    zr3 = zr.reshape(b, s, RWKV_COLS)
    row = lambda x: x.reshape(1, -1)
    w2p, a2p, g2p = _pad_rows(w2, LORA_PAD[0]), _pad_rows(a2, LORA_PAD[1]), _pad_rows(g2, LORA_PAD[2])
    const = lambda shape: pl.BlockSpec(shape, lambda bi, si: (0,) * len(shape))
    nblk = s // ts
    out_spec = pl.BlockSpec((ts, RWKV_DIM), lambda bi, si: (bi * nblk + si, 0))
    return pl.pallas_call(
        functools.partial(_rwkv_prep_body, ts=ts),
        grid=(b, nblk),
        in_specs=[pl.BlockSpec((1, ts, RWKV_COLS), lambda bi, si: (bi, si, 0)),
                  pl.BlockSpec((1, SUBLANES, RWKV_COLS), lambda bi, si: (bi, jnp.maximum(si * (ts // SUBLANES) - 1, 0), 0)),
                  const((1, RWKV_COLS)), const((1, RWKV_DIM)), const(w2p.shape), const((1, RWKV_DIM)), const(a2p.shape),
                  const(g2p.shape), const((1, RWKV_DIM)), const((1, RWKV_DIM)), const((1, RWKV_DIM)),
                  const((RWKV_DIM, RWKV_DIM))],
        out_specs=[out_spec] * 8,
        out_shape=[jax.ShapeDtypeStruct((t, RWKV_DIM), jnp.float32)] * 8,
        compiler_params=pltpu.CompilerParams(dimension_semantics=("arbitrary", "arbitrary"),
                                             vmem_limit_bytes=VMEM_LIMIT_BYTES),
        name="rwkv_prep",
    )(zr3, zr3, mu_lay, row(w0), w2p, row(a0), a2p, g2p, row(k_k), row(k_a), row(r_k), head_block_ones())


SCAN_PARTIALS = 4


def _tree_sum(parts):
    while len(parts) > 1:
        parts = [parts[i] + parts[i + 1] for i in range(0, len(parts), 2)]
    return parts[0]


def _scan_body(r_ref, w_ref, k_ref, v_ref, kk_ref, nb_ref, y_ref, *s_refs, tc):
    @pl.when(pl.program_id(1) == 0)
    def _():
        for s_ref in s_refs:
            s_ref[...] = jnp.zeros_like(s_ref)

    def step(t, carry):
        for g, s_ref in enumerate(s_refs):
            rows = slice(g * SUBLANES, (g + 1) * SUBLANES)
            acc = [None] * SCAN_PARTIALS
            for k in range(RWKV_HEAD):
                p = s_ref[k] * kk_ref[t, pl.ds(k, 1), :]
                acc[k % SCAN_PARTIALS] = p if acc[k % SCAN_PARTIALS] is None else acc[k % SCAN_PARTIALS] + p
            sa = _tree_sum(acc)
            vv = v_ref[t, rows, :]
            yacc = [None] * SCAN_PARTIALS
            for k in range(RWKV_HEAD):
                s_new = s_ref[k] * w_ref[t, pl.ds(k, 1), :] + sa * nb_ref[t, pl.ds(k, 1), :] + vv * k_ref[t, pl.ds(k, 1), :]
                s_ref[k] = s_new
                p = s_new * r_ref[t, pl.ds(k, 1), :]
                yacc[k % SCAN_PARTIALS] = p if yacc[k % SCAN_PARTIALS] is None else yacc[k % SCAN_PARTIALS] + p
            y_ref[t, rows, :] = _tree_sum(yacc)
        return carry

    lax.fori_loop(0, tc, step, 0)


def rwkv_scan(r, w, k, v, kk, nb, *, tc=32):
    s, n, c = r.shape
    spec = pl.BlockSpec((tc, n, LANES), lambda ci, ti: (ti, 0, ci))
    return pl.pallas_call(
        functools.partial(_scan_body, tc=tc),
        grid=(c // LANES, s // tc),
        in_specs=[spec] * 6,
        out_specs=spec,
        out_shape=jax.ShapeDtypeStruct((s, n, c), jnp.float32),
        scratch_shapes=[pltpu.VMEM((n, SUBLANES, LANES), jnp.float32)] * RWKV_VGROUPS,
        compiler_params=pltpu.CompilerParams(dimension_semantics=("arbitrary", "arbitrary"),
                                             vmem_limit_bytes=VMEM_LIMIT_BYTES),
        name="rwkv_scan",
    )(r, w, k, v, kk, nb)


def _merge_body(x_ref, y_ref, g_ref, bonus_ref, ya_ref, gate_ref, lnw_ref, lnb_ref, ones_ref, pa_ref, pb_ref, wo_ref, h_ref):
    dot_hi = lambda x, w: jnp.dot(x, w, preferred_element_type=jnp.float32, precision=HI)
    dot_bf = lambda x, w: jnp.dot(x.astype(jnp.bfloat16), w, preferred_element_type=jnp.float32)
    y = y_ref[...]
    inv_n = 1.0 / RWKV_HEAD
    mean = dot_hi(y, ones_ref[...]) * inv_n
    yc = y - mean
    var = dot_hi(yc * yc, ones_ref[...]) * inv_n
    yn = yc * lax.rsqrt(var + RWKV_GN_EPS) * lnw_ref[...] + lnb_ref[...] + bonus_ref[...]
    y_rwkv = yn * g_ref[...]
    gates = gate_ref[...].astype(jnp.float32)
    merged = gates[:, :D_MODEL] * dot_bf(ya_ref[...], pa_ref[...]) + gates[:, D_MODEL:] * dot_bf(y_rwkv, pb_ref[...])
    h_ref[...] = x_ref[...] + dot_bf(merged, wo_ref[...])


def merge_out(x, y, g, bonus, y_attn, gates, ln_w, ln_b, proj_attn, proj_rwkv, w_out, *, tm=256):
    t, d = x.shape
    row = lambda v: v.reshape(1, -1)
    tok = lambda w: pl.BlockSpec((tm, w), lambda i: (i, 0))
    const = lambda shape: pl.BlockSpec(shape, lambda i: (0,) * len(shape))
    bf = lambda w: w.astype(jnp.bfloat16)
    return pl.pallas_call(
        _merge_body,
        grid=(t // tm,),
        in_specs=[tok(d), tok(RWKV_DIM), tok(RWKV_DIM), tok(RWKV_DIM), tok(ATTN_DIM), tok(GATE_COLS),
                  const((1, RWKV_DIM)), const((1, RWKV_DIM)), const((RWKV_DIM, RWKV_DIM)),
                  const(proj_attn.shape), const(proj_rwkv.shape), const(w_out.shape)],
        out_specs=tok(d),
        out_shape=jax.ShapeDtypeStruct((t, d), jnp.float32),
        compiler_params=pltpu.CompilerParams(dimension_semantics=("arbitrary",), vmem_limit_bytes=VMEM_LIMIT_BYTES),
        name="merge_out",
    )(x, y, g, bonus, y_attn, gates, row(ln_w), row(ln_b), head_block_ones(), bf(proj_attn), bf(proj_rwkv), bf(w_out))


def _top16(problems):
    arrays = [p[0] for p in problems]
    for i in range(PEER_TOPK):
        for j, (_, ids, big, vals_ref, ids_ref) in enumerate(problems):
            a = arrays[j]
            m = jnp.max(a, axis=0, keepdims=True)
            r = jnp.min(jnp.where(a == m, ids, big), axis=0, keepdims=True)
            vals_ref[i:i + 1, :] = m
            ids_ref[i:i + 1, :] = r
            arrays[j] = jnp.where(ids == r, NEG_FILL, a)


def _select_rows(pos, table):
    out = jnp.zeros(pos.shape, table.dtype)
    for a in range(PEER_TOPK):
        out = jnp.where(pos == a, table[a:a + 1, :], out)
    return out


def _peer_route_body(h_ref, nw_ref, wq_ref, sk_ref, xn_ref, idx_ref, gate_ref, q_ref, idx_t_ref, gate_t_ref,
                     s1_ref, i1_ref, s2_ref, i2_ref, top_ref, pos_ref, *, tm):
    xf = h_ref[...]
    xn = xf * lax.rsqrt(jnp.mean(xf * xf, axis=-1, keepdims=True) + NORM_EPS) * nw_ref[...]
    xn_ref[...] = xn
    q_ref[...] = jnp.dot(xn, wq_ref[...], preferred_element_type=jnp.float32, precision=HI)
    key_ids = lax.broadcasted_iota(jnp.int32, (N_KEYS, LANES), 0)
    sub = lax.broadcasted_iota(jnp.int32, (SUBLANES, LANES), 0)
    cand_ids = jnp.concatenate([sub, sub + SUBLANES] + [sub + a * PEER_TOPK for a in range(1, SUBLANES)]
                               + [(sub + SUBLANES) * PEER_TOPK], axis=0)
    ncol = tm // LANES

    def head(h, carry):
        col0 = pl.multiple_of(h * 2 * PEER_HALF, 2 * PEER_HALF)
        row0 = pl.multiple_of(h * PEER_TOPK, PEER_TOPK)
        for c in range(ncol):
            qh = q_ref[c * LANES:(c + 1) * LANES, pl.ds(col0, 2 * PEER_HALF)]
            sc = lax.dot_general(sk_ref[h], qh, (((1,), (1,)), ((), ())),
                                 preferred_element_type=jnp.float32, precision=HI)
            _top16([(sc[:N_KEYS], key_ids, N_KEYS, s1_ref.at[c], i1_ref.at[c]),
                    (sc[N_KEYS:], key_ids, N_KEYS, s2_ref.at[c], i2_ref.at[c])])
        problems = []
        for c in range(ncol):
            s1, s2 = s1_ref[c], s2_ref[c]
            cand = jnp.concatenate([s1[0:1] + s2[0:SUBLANES], s1[0:1] + s2[SUBLANES:]]
                                   + [s1[a:a + 1] + s2[0:SUBLANES] for a in range(1, SUBLANES)]
                                   + [s1[SUBLANES:] + s2[0:1]], axis=0)
            problems.append((cand, cand_ids, PEER_TOPK * PEER_TOPK, top_ref.at[c], pos_ref.at[c]))
        _top16(problems)
        for c in range(ncol):
            top, pos = top_ref[c], pos_ref[c]
            e = _select_rows(pos // PEER_TOPK, i1_ref[c]) * N_KEYS + _select_rows(pos % PEER_TOPK, i2_ref[c])
            p = jnp.exp(top - top[0:1, :])
            idx_t_ref[pl.ds(row0, PEER_TOPK), c * LANES:(c + 1) * LANES] = e * ROW_SUBLANES
            gate_t_ref[pl.ds(row0, PEER_TOPK), c * LANES:(c + 1) * LANES] = p / jnp.sum(p, axis=0, keepdims=True)
        return carry

    lax.fori_loop(0, PEER_HEADS, head, 0)
    for c in range(ncol):
        idx_ref[c * LANES:(c + 1) * LANES, :] = idx_t_ref[:, c * LANES:(c + 1) * LANES].T
        gate_ref[c * LANES:(c + 1) * LANES, :] = gate_t_ref[:, c * LANES:(c + 1) * LANES].T


def _block_diag_subkeys(subkeys):
    z = jnp.zeros_like(subkeys[:, 0])
    top = jnp.concatenate([subkeys[:, 0], z], axis=-1)
    bot = jnp.concatenate([z, subkeys[:, 1]], axis=-1)
    return jnp.concatenate([top, bot], axis=1)


def peer_route(h, norm_w, wq, subkeys, *, tm=256):
    t, d = h.shape
    sk = _block_diag_subkeys(subkeys)
    return pl.pallas_call(
        functools.partial(_peer_route_body, tm=tm),
        grid=(t // tm,),
        in_specs=[
            pl.BlockSpec((tm, d), lambda i: (i, 0)),
            pl.BlockSpec((1, d), lambda i: (0, 0)),
            pl.BlockSpec(wq.shape, lambda i: (0, 0)),
            pl.BlockSpec(sk.shape, lambda i: (0, 0, 0)),
        ],
        out_specs=[pl.BlockSpec((tm, d), lambda i: (i, 0)),
                   pl.BlockSpec((tm, PEER_SEL), lambda i: (i, 0)),
                   pl.BlockSpec((tm, PEER_SEL), lambda i: (i, 0))],
        out_shape=[jax.ShapeDtypeStruct((t, d), jnp.float32),
                   jax.ShapeDtypeStruct((t, PEER_SEL), jnp.int32),
                   jax.ShapeDtypeStruct((t, PEER_SEL), jnp.float32)],
        scratch_shapes=[pltpu.VMEM((tm, d), jnp.float32), pltpu.VMEM((PEER_SEL, tm), jnp.int32),
                        pltpu.VMEM((PEER_SEL, tm), jnp.float32)]
                       + [pltpu.VMEM((tm // LANES, PEER_TOPK, LANES), jnp.float32),
                          pltpu.VMEM((tm // LANES, PEER_TOPK, LANES), jnp.int32)] * 3,
        compiler_params=pltpu.CompilerParams(dimension_semantics=("arbitrary",), vmem_limit_bytes=VMEM_LIMIT_BYTES),
        name="peer_route",
    )(h, norm_w.reshape(1, d), wq, sk)


def pack_table(tab):
    n, d = tab.shape
    b = lax.bitcast_convert_type(tab.astype(jnp.bfloat16), jnp.uint16).astype(jnp.uint32)
    w = b[:, : d // 2] | (b[:, d // 2:] << 16)
    return lax.bitcast_convert_type(w, jnp.int32).reshape(n * ROW_SUBLANES, LANES)


def _unpack(w):
    lo = lax.bitcast_convert_type(w << 16, jnp.float32)
    hi = lax.bitcast_convert_type(w & jnp.int32(-65536), jnp.float32)
    return lo, hi


def _load_table(tab_hbm, tab_vmem, sem):
    @pl.when(pl.program_id(0) == 0)
    def _():
        cp = pltpu.make_async_copy(tab_hbm, tab_vmem, sem)
        cp.start()
        cp.wait()


def _rowsum8(ps, sub):
    f = [jnp.concatenate([ps[0], ps[4]], axis=0), jnp.concatenate([ps[2], ps[6]], axis=0),
         jnp.concatenate([ps[1], ps[5]], axis=0), jnp.concatenate([ps[3], ps[7]], axis=0)]
    even2 = (sub & 2) == 0
    even1 = (sub & 1) == 0

    def merge(x, y, stride, keep_x):
        u = x + pltpu.roll(x, SUBLANES - stride, 0)
        v = y + pltpu.roll(y, stride, 0)
        return jnp.where(keep_x, u, v)

    m01 = merge(f[0], f[1], 2, even2)
    m23 = merge(f[2], f[3], 2, even2)
    return merge(m01, m23, 1, even1)


def _peer_u_body(idx_ref, x_ref, gate_ref, tab_hbm, act_ref, tab_vmem, q_ref, sem, *, tb):
    _load_table(tab_hbm, tab_vmem, sem)
    sub = lax.broadcasted_iota(jnp.int32, (SUBLANES, LANES), 0)

    def tok(t, carry):
        xrow = x_ref[pl.ds(t, 1), :]
        xl = jnp.concatenate([xrow[:, s * LANES:(s + 1) * LANES] for s in range(ROW_SUBLANES)], axis=0)
        xh = jnp.concatenate([xrow[:, s * LANES:(s + 1) * LANES] for s in range(ROW_SUBLANES, 2 * ROW_SUBLANES)], axis=0)
        rows = []
        for g in range(PEER_SEL // 8):
            ps = []
            for j in range(8):
                e = idx_ref[t, g * 8 + j]
                lo, hi = _unpack(tab_vmem[pl.ds(pl.multiple_of(e, ROW_SUBLANES), ROW_SUBLANES), :])
                ps.append(lo * xl + hi * xh)
            rows.append(_rowsum8(ps, sub))
        q_ref[t] = jnp.concatenate(rows, axis=0)
        return carry

    lax.fori_loop(0, tb, tok, 0)

    def lane_sums(c, carry):
        for j in range(SUBLANES):
            t = c * SUBLANES + j
            act_ref[pl.ds(t, 1), :] = jnp.sum(q_ref[t].T, axis=0, keepdims=True)
        return carry

    lax.fori_loop(0, tb // SUBLANES, lane_sums, 0)
    hp = act_ref[...]
    act_ref[...] = 0.5 * hp * (1.0 + lax.erf(hp * (2.0 ** -0.5))) * gate_ref[...]


def peer_u(idx, x, gate, tab, *, tb=128):
    t, d = x.shape
    return pl.pallas_call(
        functools.partial(_peer_u_body, tb=tb),
        grid=(t // tb,),
        in_specs=[
            pl.BlockSpec((tb, PEER_SEL), lambda i: (i, 0), memory_space=pltpu.SMEM),
            pl.BlockSpec((tb, d), lambda i: (i, 0)),
            pl.BlockSpec((tb, PEER_SEL), lambda i: (i, 0)),
            pl.BlockSpec(memory_space=pl.ANY),
        ],
        out_specs=pl.BlockSpec((tb, PEER_SEL), lambda i: (i, 0)),
        out_shape=jax.ShapeDtypeStruct((t, PEER_SEL), jnp.float32),
        scratch_shapes=[pltpu.VMEM(tab.shape, jnp.int32), pltpu.VMEM((tb, PEER_SEL, LANES), jnp.float32),
                        pltpu.SemaphoreType.DMA],
        compiler_params=pltpu.CompilerParams(dimension_semantics=("arbitrary",), vmem_limit_bytes=VMEM_LIMIT_BYTES),
        name="peer_u",
    )(idx, x, gate, tab)


def _peer_v_body(idx_ref, act_ref, h_ref, nw_ref, tab_hbm, o_ref, tab_vmem, sem, *, tb, final_norm):
    _load_table(tab_hbm, tab_vmem, sem)

    def tok(t, carry):
        alo = jnp.zeros((ROW_SUBLANES, LANES), jnp.float32)
        ahi = jnp.zeros((ROW_SUBLANES, LANES), jnp.float32)
        for k in range(PEER_SEL):
            e = idx_ref[t, k]
            s = act_ref[t, k]
            lo, hi = _unpack(tab_vmem[pl.ds(pl.multiple_of(e, ROW_SUBLANES), ROW_SUBLANES), :])
            alo = alo + s * lo
            ahi = ahi + s * hi
        o_ref[pl.ds(t, 1), :] = jnp.concatenate([alo[s:s + 1] for s in range(ROW_SUBLANES)]
                                                + [ahi[s:s + 1] for s in range(ROW_SUBLANES)], axis=1)
        return carry

    lax.fori_loop(0, tb, tok, 0)
    y = h_ref[...] + o_ref[...]
    if final_norm:
        y = y * lax.rsqrt(jnp.mean(y * y, axis=-1, keepdims=True) + NORM_EPS) * nw_ref[...]
    o_ref[...] = y


def peer_v(idx, act, tab, h, norm_w, *, final_norm, tb=128):
    t, d = h.shape
    return pl.pallas_call(
        functools.partial(_peer_v_body, tb=tb, final_norm=final_norm),
        grid=(t // tb,),
        in_specs=[
            pl.BlockSpec((tb, PEER_SEL), lambda i: (i, 0), memory_space=pltpu.SMEM),
            pl.BlockSpec((tb, PEER_SEL), lambda i: (i, 0), memory_space=pltpu.SMEM),
            pl.BlockSpec((tb, d), lambda i: (i, 0)),
            pl.BlockSpec((1, d), lambda i: (0, 0)),
            pl.BlockSpec(memory_space=pl.ANY),
        ],
        out_specs=pl.BlockSpec((tb, d), lambda i: (i, 0)),
        out_shape=jax.ShapeDtypeStruct((t, d), jnp.float32),
        scratch_shapes=[pltpu.VMEM(tab.shape, jnp.int32), pltpu.SemaphoreType.DMA],
        compiler_params=pltpu.CompilerParams(dimension_semantics=("arbitrary",), vmem_limit_bytes=VMEM_LIMIT_BYTES),
        name="peer_v",
    )(idx, act, h, norm_w.reshape(1, d), tab)


def kernel(x, norm_mix_w, w_in, shift_mu, attn_sinks, decay_w0, decay_w2, iclr_a0, iclr_a2,
           gate_g2, k_k, k_a, r_k, ln_x_w, ln_x_b, proj_attn, proj_rwkv, w_out, norm_ffn_w,
           peer_wq, peer_subkeys, peer_u_tab, peer_v_tab, norm_final_w):
    b, s, d = x.shape
    t = b * s
    h = x.reshape(t, d)
    for l in range(DEPTH):
        w_lay, mu_lay = layout_in_proj(w_in[l], shift_mu[l])
        qkv, zr, gates = in_proj(h, norm_mix_w[l], w_lay)
        y_attn = swa_attention(qkv, attn_sinks[l], b, s)
        r, dec, k, v, kk, nb, g, bonus = rwkv_prep(zr, mu_lay, decay_w0[l], decay_w2[l], iclr_a0[l], iclr_a2[l],
                                                   gate_g2[l], k_k[l], k_a[l], r_k[l].reshape(-1), b, s)
        to_chains = lambda a: jnp.transpose(a.reshape(b, s, RWKV_HEADS, RWKV_HEAD), (1, 3, 0, 2)).reshape(
            s, RWKV_HEAD, b * RWKV_HEADS)
        y = rwkv_scan(to_chains(r), to_chains(dec), to_chains(k), to_chains(v), to_chains(kk), to_chains(nb))
        y = jnp.transpose(y.reshape(s, RWKV_HEAD, b, RWKV_HEADS), (2, 0, 3, 1)).reshape(t, RWKV_DIM)
        h = merge_out(h, y, g, bonus, y_attn, gates, ln_x_w[l], ln_x_b[l], proj_attn[l], proj_rwkv[l], w_out[l])
        xn2, idx, gate = peer_route(h, norm_ffn_w[l], peer_wq[l], peer_subkeys[l])
        act = peer_u(idx, xn2, gate, pack_table(peer_u_tab[l]))
        h = peer_v(idx, act, pack_table(peer_v_tab[l]), h, norm_final_w, final_norm=(l + 1 == DEPTH))
    return h.reshape(b, s, d)
```

```python
import functools

import jax, jax.numpy as jnp
from jax import lax
from jax.experimental import pallas as pl
from jax.experimental.pallas import tpu as pltpu

D_MODEL = 1024
DEPTH = 1

HEAD_DIM = 64
ATTN_HEADS = 8
ATTN_KV_HEADS = 2
ATTN_GROUP = ATTN_HEADS // ATTN_KV_HEADS
ATTN_DIM = ATTN_HEADS * HEAD_DIM
KV_DIM = ATTN_KV_HEADS * HEAD_DIM
WINDOW = 128
BLOCK = WINDOW
ROPE_THETA = 10000.0

RWKV_HEADS = 8
RWKV_HEAD = 64
RWKV_DIM = RWKV_HEADS * RWKV_HEAD
DECAY_LORA = 64
ICLR_LORA = 64
GATE_LORA = 160
RWKV_GN_EPS = 64e-5

PEER_HEADS = 8
N_KEYS = 128
PEER_HALF = 64
PEER_TOPK = 16
PEER_SEL = PEER_HEADS * PEER_TOPK

NORM_EPS = 1e-5
NEG_INF = -1e30

LANES = 128
SUBLANES = 8
VMEM_LIMIT_BYTES = 56 * 1024 * 1024

ROW_WORDS = D_MODEL // 2
ROW_SUBLANES = ROW_WORDS // LANES
NEG_FILL = -3.0e38
HI = lax.Precision.HIGHEST
RWKV_VGROUPS = RWKV_HEAD // SUBLANES

QKV_COLS = ATTN_DIM + 2 * KV_DIM
LORA_PAD = (LANES, LANES, 2 * LANES)
RWKV_COLS = 3 * RWKV_DIM + sum(LORA_PAD)
GATE_COLS = 2 * D_MODEL


def _pad_cols(w, width):
    return jnp.pad(w, ((0, 0), (0, width - w.shape[1])))


def _pad_rows(w, height):
    return jnp.pad(w, ((0, height - w.shape[0]), (0, 0)))


def layout_in_proj(w_in, shift_mu):
    o1 = QKV_COLS
    o_r = o1 + 3 * RWKV_DIM
    o_w = o_r + DECAY_LORA
    o_a = o_w + ICLR_LORA
    o_g = o_a + GATE_LORA
    parts = [w_in[:, :o_r], _pad_cols(w_in[:, o_r:o_w], LORA_PAD[0]), _pad_cols(w_in[:, o_w:o_a], LORA_PAD[1]),
             _pad_cols(w_in[:, o_a:o_g], LORA_PAD[2]), w_in[:, o_g:]]
    mu = shift_mu.reshape(1, -1)
    m_r = 3 * RWKV_DIM
    mu_parts = [mu[:, :m_r], _pad_cols(mu[:, m_r:m_r + DECAY_LORA], LORA_PAD[0]),
                _pad_cols(mu[:, m_r + DECAY_LORA:m_r + DECAY_LORA + ICLR_LORA], LORA_PAD[1]),
                _pad_cols(mu[:, m_r + DECAY_LORA + ICLR_LORA:], LORA_PAD[2])]
    return jnp.concatenate(parts, axis=1), jnp.concatenate(mu_parts, axis=1)


def _in_proj_body(x_ref, nw_ref, w_ref, qkv_ref, zr_ref, gate_ref):
    xf = x_ref[...]
    xn = xf * lax.rsqrt(jnp.mean(xf * xf, axis=-1, keepdims=True) + NORM_EPS) * nw_ref[...]
    z = jnp.dot(xn.astype(jnp.bfloat16), w_ref[...], preferred_element_type=jnp.float32)
    qkv_ref[...] = z[:, :QKV_COLS]
    zr_ref[...] = z[:, QKV_COLS:QKV_COLS + RWKV_COLS]
    gate_ref[...] = jax.nn.sigmoid(z[:, QKV_COLS + RWKV_COLS:]).astype(jnp.bfloat16)


def in_proj(x, norm_w, w_lay, *, tm=256):
    t, d = x.shape
    n = w_lay.shape[1]
    return pl.pallas_call(
        _in_proj_body,
        grid=(t // tm,),
        in_specs=[pl.BlockSpec((tm, d), lambda i: (i, 0)), pl.BlockSpec((1, d), lambda i: (0, 0)),
                  pl.BlockSpec((d, n), lambda i: (0, 0))],
        out_specs=[pl.BlockSpec((tm, QKV_COLS), lambda i: (i, 0)), pl.BlockSpec((tm, RWKV_COLS), lambda i: (i, 0)),
                   pl.BlockSpec((tm, GATE_COLS), lambda i: (i, 0))],
        out_shape=[jax.ShapeDtypeStruct((t, QKV_COLS), jnp.float32), jax.ShapeDtypeStruct((t, RWKV_COLS), jnp.float32),
                   jax.ShapeDtypeStruct((t, GATE_COLS), jnp.bfloat16)],
        compiler_params=pltpu.CompilerParams(dimension_semantics=("arbitrary",), vmem_limit_bytes=VMEM_LIMIT_BYTES),
        name="in_proj",
    )(x, norm_w.reshape(1, d), w_lay.astype(jnp.bfloat16))


def rope_tables(s):
    inv = ROPE_THETA ** (-jnp.arange(0, HEAD_DIM, 2, dtype=jnp.float32) / HEAD_DIM)
    ang = jnp.arange(s, dtype=jnp.float32)[:, None] * inv[None, :]
    cos = jnp.concatenate([jnp.cos(ang), jnp.cos(ang)], axis=-1)
    sin = jnp.concatenate([-jnp.sin(ang), jnp.sin(ang)], axis=-1)
    reps = LANES // HEAD_DIM
    return jnp.tile(cos, (1, reps)), jnp.tile(sin, (1, reps))


def _rope(x, cos, sin, first_half):
    rot = jnp.where(first_half, pltpu.roll(x, LANES - HEAD_DIM // 2, 1), pltpu.roll(x, HEAD_DIM // 2, 1))
    return x * cos + rot * sin


def _attn_body(cur_ref, prev_ref, cos_ref, sin_ref, cosp_ref, sinp_ref, sink_ref, o_ref):
    n = pl.program_id(1)
    lane = lax.broadcasted_iota(jnp.int32, (BLOCK, LANES), 1)
    first_half = (lane % HEAD_DIM) < HEAD_DIM // 2
    low = lane < HEAD_DIM
    cos, sin = cos_ref[...], sin_ref[...]
    cur = cur_ref[0]
    prev = prev_ref[0]
    k_cur = _rope(cur[:, ATTN_DIM:ATTN_DIM + KV_DIM], cos, sin, first_half)
    k_prev = _rope(prev[:, ATTN_DIM:ATTN_DIM + KV_DIM], cosp_ref[...], sinp_ref[...], first_half)
    keys = jnp.concatenate([k_prev, k_cur], axis=0)
    vals = jnp.concatenate([prev[:, ATTN_DIM + KV_DIM:], cur[:, ATTN_DIM + KV_DIM:]], axis=0)
    keys_sw = pltpu.roll(keys, HEAD_DIM, 1)
    vals_sw = pltpu.roll(vals, HEAD_DIM, 1)
    low2 = lax.broadcasted_iota(jnp.int32, (2 * BLOCK, LANES), 1) < HEAD_DIM
    qi = lax.broadcasted_iota(jnp.int32, (BLOCK, 2 * BLOCK), 0) + BLOCK
    kj = lax.broadcasted_iota(jnp.int32, (BLOCK, 2 * BLOCK), 1)
    diff = qi - kj
    mask = (diff >= 0) & (diff < WINDOW) & ((kj >= BLOCK) | (n > 0))
    scale = HEAD_DIM ** -0.5
    outs = []
    for j in range(ATTN_HEADS // 2):
        g = (2 * j) // ATTN_GROUP
        kg = (jnp.where(low2, keys, keys_sw) if g == 0 else jnp.where(low2, keys_sw, keys)).astype(jnp.bfloat16)
        vg = (jnp.where(low2, vals, vals_sw) if g == 0 else jnp.where(low2, vals_sw, vals)).astype(jnp.bfloat16)
        qg = _rope(cur[:, j * LANES:(j + 1) * LANES], cos, sin, first_half) * scale
        halves = []
        for half in range(2):
            qh = jnp.where(low if half == 0 else ~low, qg, 0.0).astype(jnp.bfloat16)
            logits = lax.dot_general(qh, kg, (((1,), (1,)), ((), ())), preferred_element_type=jnp.float32)
            logits = jnp.where(mask, logits, NEG_INF)
            sink = sink_ref[2 * j + half]
            m = jnp.maximum(jnp.max(logits, axis=-1, keepdims=True), sink)
            p = jnp.exp(logits - m)
            denom = jnp.sum(p, axis=-1, keepdims=True) + jnp.exp(sink - m)
            o = jnp.dot(p.astype(jnp.bfloat16), vg, preferred_element_type=jnp.float32)
            halves.append(o / denom)
        outs.append(jnp.where(low, halves[0], halves[1]))
    o_ref[0] = jnp.concatenate(outs, axis=1).astype(o_ref.dtype)


def swa_attention(qkv, sinks, b, s):
    nb = s // BLOCK
    cos, sin = rope_tables(s)
    qkv3 = qkv.reshape(b, s, QKV_COLS)
    cur_spec = pl.BlockSpec((1, BLOCK, QKV_COLS), lambda bi, n: (bi, n, 0))
    prev_spec = pl.BlockSpec((1, BLOCK, QKV_COLS), lambda bi, n: (bi, jnp.maximum(n - 1, 0), 0))
    tab = pl.BlockSpec((BLOCK, LANES), lambda bi, n: (n, 0))
    tabp = pl.BlockSpec((BLOCK, LANES), lambda bi, n: (jnp.maximum(n - 1, 0), 0))
    out = pl.pallas_call(
        _attn_body,
        grid=(b, nb),
        in_specs=[cur_spec, prev_spec, tab, tab, tabp, tabp, pl.BlockSpec(memory_space=pltpu.SMEM)],
        out_specs=pl.BlockSpec((1, BLOCK, ATTN_DIM), lambda bi, n: (bi, n, 0)),
        out_shape=jax.ShapeDtypeStruct((b, s, ATTN_DIM), jnp.bfloat16),
        compiler_params=pltpu.CompilerParams(dimension_semantics=("arbitrary", "arbitrary"),
                                             vmem_limit_bytes=VMEM_LIMIT_BYTES),
        name="swa_attention",
    )(qkv3, qkv3, cos, sin, cos, sin, sinks.astype(jnp.float32))
    return out.reshape(b * s, ATTN_DIM)


def head_block_ones():
    i = jnp.arange(RWKV_DIM) // RWKV_HEAD
    return (i[:, None] == i[None, :]).astype(jnp.float32)


def _softplus(x):
    return jnp.maximum(x, 0.0) + jnp.log1p(jnp.exp(-jnp.abs(x)))


def _rwkv_prep_body(z_ref, zp_ref, mu_ref, w0_ref, w2_ref, a0_ref, a2_ref, g2_ref, kk_w_ref, ka_ref, rk_ref, ones_ref,
                    r_ref, dec_ref, k_ref, v_ref, kk_ref, nb_ref, g_ref, bonus_ref, *, ts):
    z = z_ref[0]
    row = lax.broadcasted_iota(jnp.int32, (ts, 1), 0)
    last_prev = jnp.where(pl.program_id(1) > 0, zp_ref[0][SUBLANES - 1:SUBLANES, :], 0.0)
    z_prev = jnp.where(row == 0, last_prev, pltpu.roll(z, 1, 0))
    zs = z + mu_ref[...] * (z_prev - z)
    c = RWKV_DIM
    r, k, v = zs[:, :c], zs[:, c:2 * c], zs[:, 2 * c:3 * c]
    o = 3 * c
    w_lo = zs[:, o:o + LORA_PAD[0]]
    a_lo = zs[:, o + LORA_PAD[0]:o + LORA_PAD[0] + LORA_PAD[1]]
    g_lo = zs[:, o + LORA_PAD[0] + LORA_PAD[1]:]
    dot = lambda x, w: jnp.dot(x, w, preferred_element_type=jnp.float32, precision=HI)
    w = -_softplus(-(w0_ref[...] + dot(jnp.tanh(w_lo), w2_ref[...]))) - 0.5
    dec_ref[...] = jnp.exp(-jnp.exp(w))
    a = jax.nn.sigmoid(a0_ref[...] + dot(a_lo, a2_ref[...]))
    g_ref[...] = dot(jax.nn.sigmoid(g_lo), g2_ref[...])
    kk = k * kk_w_ref[...]
    kk = kk / jnp.maximum(jnp.sqrt(dot(kk * kk, ones_ref[...])), 1e-12)
    kmod = k * (1.0 + (a - 1.0) * ka_ref[...])
    r_ref[...] = r
    k_ref[...] = kmod
    v_ref[...] = v
    kk_ref[...] = kk
    nb_ref[...] = -kk * a
    bonus_ref[...] = dot(r * kmod * rk_ref[...], ones_ref[...]) * v


def rwkv_prep(zr, mu_lay, w0, w2, a0, a2, g2, k_k, k_a, r_k, b, s, *, ts=256):
    t = b * s
    zr3 = zr.reshape(b, s, RWKV_COLS)
    row = lambda x: x.reshape(1, -1)
    w2p, a2p, g2p = _pad_rows(w2, LORA_PAD[0]), _pad_rows(a2, LORA_PAD[1]), _pad_rows(g2, LORA_PAD[2])
    const = lambda shape: pl.BlockSpec(shape, lambda bi, si: (0,) * len(shape))
    nblk = s // ts
    out_spec = pl.BlockSpec((ts, RWKV_DIM), lambda bi, si: (bi * nblk + si, 0))
    return pl.pallas_call(
        functools.partial(_rwkv_prep_body, ts=ts),
        grid=(b, nblk),
        in_specs=[pl.BlockSpec((1, ts, RWKV_COLS), lambda bi, si: (bi, si, 0)),
                  pl.BlockSpec((1, SUBLANES, RWKV_COLS), lambda bi, si: (bi, jnp.maximum(si * (ts // SUBLANES) - 1, 0), 0)),
                  const((1, RWKV_COLS)), const((1, RWKV_DIM)), const(w2p.shape), const((1, RWKV_DIM)), const(a2p.shape),
                  const(g2p.shape), const((1, RWKV_DIM)), const((1, RWKV_DIM)), const((1, RWKV_DIM)),
                  const((RWKV_DIM, RWKV_DIM))],
        out_specs=[out_spec] * 8,
        out_shape=[jax.ShapeDtypeStruct((t, RWKV_DIM), jnp.float32)] * 8,
        compiler_params=pltpu.CompilerParams(dimension_semantics=("arbitrary", "arbitrary"),
                                             vmem_limit_bytes=VMEM_LIMIT_BYTES),
        name="rwkv_prep",
    )(zr3, zr3, mu_lay, row(w0), w2p, row(a0), a2p, g2p, row(k_k), row(k_a), row(r_k), head_block_ones())


SCAN_PARTIALS = 4


def _tree_sum(parts):
    while len(parts) > 1:
        parts = [parts[i] + parts[i + 1] for i in range(0, len(parts), 2)]
    return parts[0]


def _scan_body(r_ref, w_ref, k_ref, v_ref, kk_ref, nb_ref, y_ref, *s_refs, tc):
    @pl.when(pl.program_id(1) == 0)
    def _():
        for s_ref in s_refs:
            s_ref[...] = jnp.zeros_like(s_ref)

    def step(t, carry):
        for g, s_ref in enumerate(s_refs):
            rows = slice(g * SUBLANES, (g + 1) * SUBLANES)
            acc = [None] * SCAN_PARTIALS
            for k in range(RWKV_HEAD):
                p = s_ref[k] * kk_ref[t, pl.ds(k, 1), :]
                acc[k % SCAN_PARTIALS] = p if acc[k % SCAN_PARTIALS] is None else acc[k % SCAN_PARTIALS] + p
            sa = _tree_sum(acc)
            vv = v_ref[t, rows, :]
            yacc = [None] * SCAN_PARTIALS
            for k in range(RWKV_HEAD):
                s_new = s_ref[k] * w_ref[t, pl.ds(k, 1), :] + sa * nb_ref[t, pl.ds(k, 1), :] + vv * k_ref[t, pl.ds(k, 1), :]
                s_ref[k] = s_new
                p = s_new * r_ref[t, pl.ds(k, 1), :]
                yacc[k % SCAN_PARTIALS] = p if yacc[k % SCAN_PARTIALS] is None else yacc[k % SCAN_PARTIALS] + p
            y_ref[t, rows, :] = _tree_sum(yacc)
        return carry

    lax.fori_loop(0, tc, step, 0)


def rwkv_scan(r, w, k, v, kk, nb, *, tc=32):
    s, n, c = r.shape
    spec = pl.BlockSpec((tc, n, LANES), lambda ci, ti: (ti, 0, ci))
    return pl.pallas_call(
        functools.partial(_scan_body, tc=tc),
        grid=(c // LANES, s // tc),
        in_specs=[spec] * 6,
        out_specs=spec,
        out_shape=jax.ShapeDtypeStruct((s, n, c), jnp.float32),
        scratch_shapes=[pltpu.VMEM((n, SUBLANES, LANES), jnp.float32)] * RWKV_VGROUPS,
        compiler_params=pltpu.CompilerParams(dimension_semantics=("arbitrary", "arbitrary"),
                                             vmem_limit_bytes=VMEM_LIMIT_BYTES),
        name="rwkv_scan",
    )(r, w, k, v, kk, nb)


def _merge_body(x_ref, y_ref, g_ref, bonus_ref, ya_ref, gate_ref, lnw_ref, lnb_ref, ones_ref, pa_ref, pb_ref, wo_ref, h_ref):
    dot_hi = lambda x, w: jnp.dot(x, w, preferred_element_type=jnp.float32, precision=HI)
    dot_bf = lambda x, w: jnp.dot(x.astype(jnp.bfloat16), w, preferred_element_type=jnp.float32)
    y = y_ref[...]
    inv_n = 1.0 / RWKV_HEAD
    mean = dot_hi(y, ones_ref[...]) * inv_n
    yc = y - mean
    var = dot_hi(yc * yc, ones_ref[...]) * inv_n
    yn = yc * lax.rsqrt(var + RWKV_GN_EPS) * lnw_ref[...] + lnb_ref[...] + bonus_ref[...]
    y_rwkv = yn * g_ref[...]
    gates = gate_ref[...].astype(jnp.float32)
    merged = gates[:, :D_MODEL] * dot_bf(ya_ref[...], pa_ref[...]) + gates[:, D_MODEL:] * dot_bf(y_rwkv, pb_ref[...])
    h_ref[...] = x_ref[...] + dot_bf(merged, wo_ref[...])


def merge_out(x, y, g, bonus, y_attn, gates, ln_w, ln_b, proj_attn, proj_rwkv, w_out, *, tm=256):
    t, d = x.shape
    row = lambda v: v.reshape(1, -1)
    tok = lambda w: pl.BlockSpec((tm, w), lambda i: (i, 0))
    const = lambda shape: pl.BlockSpec(shape, lambda i: (0,) * len(shape))
    bf = lambda w: w.astype(jnp.bfloat16)
    return pl.pallas_call(
        _merge_body,
        grid=(t // tm,),
        in_specs=[tok(d), tok(RWKV_DIM), tok(RWKV_DIM), tok(RWKV_DIM), tok(ATTN_DIM), tok(GATE_COLS),
                  const((1, RWKV_DIM)), const((1, RWKV_DIM)), const((RWKV_DIM, RWKV_DIM)),
                  const(proj_attn.shape), const(proj_rwkv.shape), const(w_out.shape)],
        out_specs=tok(d),
        out_shape=jax.ShapeDtypeStruct((t, d), jnp.float32),
        compiler_params=pltpu.CompilerParams(dimension_semantics=("arbitrary",), vmem_limit_bytes=VMEM_LIMIT_BYTES),
        name="merge_out",
    )(x, y, g, bonus, y_attn, gates, row(ln_w), row(ln_b), head_block_ones(), bf(proj_attn), bf(proj_rwkv), bf(w_out))


def _top16(problems):
    arrays = [p[0] for p in problems]
    for i in range(PEER_TOPK):
        for j, (_, ids, big, vals_ref, ids_ref) in enumerate(problems):
            a = arrays[j]
            m = jnp.max(a, axis=0, keepdims=True)
            r = jnp.min(jnp.where(a == m, ids, big), axis=0, keepdims=True)
            vals_ref[i:i + 1, :] = m
            ids_ref[i:i + 1, :] = r
            arrays[j] = jnp.where(ids == r, NEG_FILL, a)


def _select_rows(pos, table):
    out = jnp.zeros(pos.shape, table.dtype)
    for a in range(PEER_TOPK):
        out = jnp.where(pos == a, table[a:a + 1, :], out)
    return out


def _peer_route_body(h_ref, nw_ref, wq_ref, sk_ref, xn_ref, idx_ref, gate_ref, q_ref, idx_t_ref, gate_t_ref,
                     s1_ref, i1_ref, s2_ref, i2_ref, top_ref, pos_ref, *, tm):
    xf = h_ref[...]
    xn = xf * lax.rsqrt(jnp.mean(xf * xf, axis=-1, keepdims=True) + NORM_EPS) * nw_ref[...]
    xn_ref[...] = xn
    q_ref[...] = jnp.dot(xn, wq_ref[...], preferred_element_type=jnp.float32, precision=HI)
    key_ids = lax.broadcasted_iota(jnp.int32, (N_KEYS, LANES), 0)
    sub = lax.broadcasted_iota(jnp.int32, (SUBLANES, LANES), 0)
    cand_ids = jnp.concatenate([sub, sub + SUBLANES] + [sub + a * PEER_TOPK for a in range(1, SUBLANES)]
                               + [(sub + SUBLANES) * PEER_TOPK], axis=0)
    ncol = tm // LANES

    def head(h, carry):
        col0 = pl.multiple_of(h * 2 * PEER_HALF, 2 * PEER_HALF)
        row0 = pl.multiple_of(h * PEER_TOPK, PEER_TOPK)
        for c in range(ncol):
            qh = q_ref[c * LANES:(c + 1) * LANES, pl.ds(col0, 2 * PEER_HALF)]
            sc = lax.dot_general(sk_ref[h], qh, (((1,), (1,)), ((), ())),
                                 preferred_element_type=jnp.float32, precision=HI)
            _top16([(sc[:N_KEYS], key_ids, N_KEYS, s1_ref.at[c], i1_ref.at[c]),
                    (sc[N_KEYS:], key_ids, N_KEYS, s2_ref.at[c], i2_ref.at[c])])
        problems = []
        for c in range(ncol):
            s1, s2 = s1_ref[c], s2_ref[c]
            cand = jnp.concatenate([s1[0:1] + s2[0:SUBLANES], s1[0:1] + s2[SUBLANES:]]
                                   + [s1[a:a + 1] + s2[0:SUBLANES] for a in range(1, SUBLANES)]
                                   + [s1[SUBLANES:] + s2[0:1]], axis=0)
            problems.append((cand, cand_ids, PEER_TOPK * PEER_TOPK, top_ref.at[c], pos_ref.at[c]))
        _top16(problems)
        for c in range(ncol):
            top, pos = top_ref[c], pos_ref[c]
            e = _select_rows(pos // PEER_TOPK, i1_ref[c]) * N_KEYS + _select_rows(pos % PEER_TOPK, i2_ref[c])
            p = jnp.exp(top - top[0:1, :])
            idx_t_ref[pl.ds(row0, PEER_TOPK), c * LANES:(c + 1) * LANES] = e * ROW_SUBLANES
            gate_t_ref[pl.ds(row0, PEER_TOPK), c * LANES:(c + 1) * LANES] = p / jnp.sum(p, axis=0, keepdims=True)
        return carry

    lax.fori_loop(0, PEER_HEADS, head, 0)
    for c in range(ncol):
        idx_ref[c * LANES:(c + 1) * LANES, :] = idx_t_ref[:, c * LANES:(c + 1) * LANES].T
        gate_ref[c * LANES:(c + 1) * LANES, :] = gate_t_ref[:, c * LANES:(c + 1) * LANES].T


def _block_diag_subkeys(subkeys):
    z = jnp.zeros_like(subkeys[:, 0])
    top = jnp.concatenate([subkeys[:, 0], z], axis=-1)
    bot = jnp.concatenate([z, subkeys[:, 1]], axis=-1)
    return jnp.concatenate([top, bot], axis=1)


def peer_route(h, norm_w, wq, subkeys, *, tm=256):
    t, d = h.shape
    sk = _block_diag_subkeys(subkeys)
    return pl.pallas_call(
        functools.partial(_peer_route_body, tm=tm),
        grid=(t // tm,),
        in_specs=[
            pl.BlockSpec((tm, d), lambda i: (i, 0)),
            pl.BlockSpec((1, d), lambda i: (0, 0)),
            pl.BlockSpec(wq.shape, lambda i: (0, 0)),
            pl.BlockSpec(sk.shape, lambda i: (0, 0, 0)),
        ],
        out_specs=[pl.BlockSpec((tm, d), lambda i: (i, 0)),
                   pl.BlockSpec((tm, PEER_SEL), lambda i: (i, 0)),
                   pl.BlockSpec((tm, PEER_SEL), lambda i: (i, 0))],
        out_shape=[jax.ShapeDtypeStruct((t, d), jnp.float32),
                   jax.ShapeDtypeStruct((t, PEER_SEL), jnp.int32),
                   jax.ShapeDtypeStruct((t, PEER_SEL), jnp.float32)],
        scratch_shapes=[pltpu.VMEM((tm, d), jnp.float32), pltpu.VMEM((PEER_SEL, tm), jnp.int32),
                        pltpu.VMEM((PEER_SEL, tm), jnp.float32)]
                       + [pltpu.VMEM((tm // LANES, PEER_TOPK, LANES), jnp.float32),
                          pltpu.VMEM((tm // LANES, PEER_TOPK, LANES), jnp.int32)] * 3,
        compiler_params=pltpu.CompilerParams(dimension_semantics=("arbitrary",), vmem_limit_bytes=VMEM_LIMIT_BYTES),
        name="peer_route",
    )(h, norm_w.reshape(1, d), wq, sk)


def pack_table(tab):
    n, d = tab.shape
    b = lax.bitcast_convert_type(tab.astype(jnp.bfloat16), jnp.uint16).astype(jnp.uint32)
    w = b[:, : d // 2] | (b[:, d // 2:] << 16)
    return lax.bitcast_convert_type(w, jnp.int32).reshape(n * ROW_SUBLANES, LANES)


def _unpack(w):
    lo = lax.bitcast_convert_type(w << 16, jnp.float32)
    hi = lax.bitcast_convert_type(w & jnp.int32(-65536), jnp.float32)
    return lo, hi


STAGE_STRIDE = 136
STAGE_ROWS = ROW_SUBLANES * STAGE_STRIDE
STAGE_TOKENS = 8


def _load_table(tab_hbm, tab_vmem, sem, st_first_read):
    @pl.when(pl.program_id(0) == 0)
    def _():
        cp = pltpu.make_async_copy(tab_hbm, tab_vmem, sem)
        cp.start()
        st_first_read[...] = jnp.zeros_like(st_first_read)
        cp.wait()


def _split_bf16(x):
    hi = x.astype(jnp.bfloat16)
    return hi, (x - hi.astype(jnp.float32)).astype(jnp.bfloat16)


def _stage_rows(i, k):
    return pl.ds(i * STAGE_ROWS + k, ROW_SUBLANES, stride=STAGE_STRIDE)


def _gather_u(idx_ref, x_ref, t0, tab_vmem, st_ref):
    for i in range(STAGE_TOKENS):
        xrow = x_ref[pl.ds(t0 + i, 1), :]
        xl = jnp.concatenate([xrow[:, s * LANES:(s + 1) * LANES] for s in range(ROW_SUBLANES)], axis=0)
        xh = jnp.concatenate([xrow[:, s * LANES:(s + 1) * LANES] for s in range(ROW_SUBLANES, 2 * ROW_SUBLANES)], axis=0)
        for k in range(PEER_SEL):
            e = idx_ref[t0 + i, k]
            lo, hi = _unpack(tab_vmem[pl.ds(pl.multiple_of(e, ROW_SUBLANES), ROW_SUBLANES), :])
            st_ref[_stage_rows(i, k), :] = lo * xl + hi * xh


def _dot_products(st_ref, ones, eye):
    sums = jnp.dot(st_ref[...].astype(jnp.bfloat16), ones, preferred_element_type=jnp.float32)
    rows = []
    for i in range(STAGE_TOKENS):
        r0 = i * STAGE_ROWS
        per_pair = (sums[r0:r0 + PEER_SEL] + sums[r0 + STAGE_STRIDE:r0 + STAGE_STRIDE + PEER_SEL]
                    + sums[r0 + 2 * STAGE_STRIDE:r0 + 2 * STAGE_STRIDE + PEER_SEL]
                    + sums[r0 + 3 * STAGE_STRIDE:r0 + 3 * STAGE_STRIDE + PEER_SEL])
        rows.append(jnp.sum(jnp.where(eye, per_pair, 0.0), axis=0, keepdims=True))
    return jnp.concatenate(rows, axis=0)


def _peer_u_body(idx_ref, x_ref, gate_ref, tab_hbm, act_ref, tab_vmem, st_a, st_b, sem, *, tb):
    _load_table(tab_hbm, tab_vmem, sem, st_b)
    n = STAGE_TOKENS
    ones = jnp.ones((LANES, LANES), jnp.bfloat16)
    eye = lax.broadcasted_iota(jnp.int32, (PEER_SEL, LANES), 0) == lax.broadcasted_iota(jnp.int32, (PEER_SEL, LANES), 1)

    def two_groups(j, carry):
        t0 = pl.multiple_of(j * 2 * n, 2 * n)
        tp = pl.multiple_of(jnp.maximum(t0 - n, 0), n)
        act_ref[pl.ds(tp, n), :] = _dot_products(st_b, ones, eye)
        _gather_u(idx_ref, x_ref, t0, tab_vmem, st_a)
        act_ref[pl.ds(t0, n), :] = _dot_products(st_a, ones, eye)
        _gather_u(idx_ref, x_ref, t0 + n, tab_vmem, st_b)
        return carry

    lax.fori_loop(0, tb // (2 * n), two_groups, 0)
    act_ref[tb - n:tb, :] = _dot_products(st_b, ones, eye)
    hp = act_ref[...]
    act_ref[...] = 0.5 * hp * (1.0 + lax.erf(hp * (2.0 ** -0.5))) * gate_ref[...]


def peer_u(idx, x, gate, tab, *, tb=128):
    t, d = x.shape
    stage = pltpu.VMEM((STAGE_TOKENS * STAGE_ROWS, LANES), jnp.float32)
    return pl.pallas_call(
        functools.partial(_peer_u_body, tb=tb),
        grid=(t // tb,),
        in_specs=[
            pl.BlockSpec((tb, PEER_SEL), lambda i: (i, 0), memory_space=pltpu.SMEM),
            pl.BlockSpec((tb, d), lambda i: (i, 0)),
            pl.BlockSpec((tb, PEER_SEL), lambda i: (i, 0)),
            pl.BlockSpec(memory_space=pl.ANY),
        ],
        out_specs=pl.BlockSpec((tb, PEER_SEL), lambda i: (i, 0)),
        out_shape=jax.ShapeDtypeStruct((t, PEER_SEL), jnp.float32),
        scratch_shapes=[pltpu.VMEM(tab.shape, jnp.int32), stage, stage, pltpu.SemaphoreType.DMA],
        compiler_params=pltpu.CompilerParams(dimension_semantics=("arbitrary",), vmem_limit_bytes=VMEM_LIMIT_BYTES),
        name="peer_u",
    )(idx, x, gate, tab)


def _gather_v(idx_ref, t, i, tab_vmem, st_ref):
    for k in range(PEER_SEL):
        e = idx_ref[t, k]
        st_ref[_stage_rows(i, k), :] = tab_vmem[pl.ds(pl.multiple_of(e, ROW_SUBLANES), ROW_SUBLANES), :]


def _weighted_sum(st_ref, i, act_row):
    ah, al = _split_bf16(act_row)
    lhs = jnp.concatenate([ah, al, jnp.zeros((SUBLANES - 2, PEER_SEL), jnp.bfloat16)], axis=0)
    r0 = i * STAGE_ROWS
    w = jnp.concatenate([st_ref[r0 + s * STAGE_STRIDE:r0 + s * STAGE_STRIDE + PEER_SEL, :] for s in range(ROW_SUBLANES)],
                        axis=1)
    lo, hi = _unpack(w)
    r = jnp.dot(lhs, jnp.concatenate([lo, hi], axis=1).astype(jnp.bfloat16), preferred_element_type=jnp.float32)
    return r[0:1] + r[1:2]


def _peer_v_body(idx_ref, act_ref, h_ref, nw_ref, tab_hbm, o_ref, tab_vmem, st_a, st_b, sem, *, tb, final_norm):
    _load_table(tab_hbm, tab_vmem, sem, st_b)
    n = STAGE_TOKENS

    def half(t_gather, st_fill, t_mxu, st_read):
        rows = []
        for i in range(n):
            _gather_v(idx_ref, t_gather + i, i, tab_vmem, st_fill)
            rows.append(_weighted_sum(st_read, i, act_ref[pl.ds(t_mxu + i, 1), :]))
        o_ref[pl.ds(t_mxu, n), :] = jnp.concatenate(rows, axis=0)

    def two_groups(j, carry):
        t0 = pl.multiple_of(j * 2 * n, 2 * n)
        tp = pl.multiple_of(jnp.maximum(t0 - n, 0), n)
        half(t0, st_a, tp, st_b)
        half(t0 + n, st_b, t0, st_a)
        return carry

    lax.fori_loop(0, tb // (2 * n), two_groups, 0)
    o_ref[tb - n:tb, :] = jnp.concatenate(
        [_weighted_sum(st_b, i, act_ref[tb - n + i:tb - n + i + 1, :]) for i in range(n)], axis=0)
    y = h_ref[...] + o_ref[...]
    if final_norm:
        y = y * lax.rsqrt(jnp.mean(y * y, axis=-1, keepdims=True) + NORM_EPS) * nw_ref[...]
    o_ref[...] = y


def peer_v(idx, act, tab, h, norm_w, *, final_norm, tb=128):
    t, d = h.shape
    stage = pltpu.VMEM((STAGE_TOKENS * STAGE_ROWS, LANES), jnp.int32)
    return pl.pallas_call(
        functools.partial(_peer_v_body, tb=tb, final_norm=final_norm),
        grid=(t // tb,),
        in_specs=[
            pl.BlockSpec((tb, PEER_SEL), lambda i: (i, 0), memory_space=pltpu.SMEM),
            pl.BlockSpec((tb, PEER_SEL), lambda i: (i, 0)),
            pl.BlockSpec((tb, d), lambda i: (i, 0)),
            pl.BlockSpec((1, d), lambda i: (0, 0)),
            pl.BlockSpec(memory_space=pl.ANY),
        ],
        out_specs=pl.BlockSpec((tb, d), lambda i: (i, 0)),
        out_shape=jax.ShapeDtypeStruct((t, d), jnp.float32),
        scratch_shapes=[pltpu.VMEM(tab.shape, jnp.int32), stage, stage, pltpu.SemaphoreType.DMA],
        compiler_params=pltpu.CompilerParams(dimension_semantics=("arbitrary",), vmem_limit_bytes=VMEM_LIMIT_BYTES),
        name="peer_v",
    )(idx, act, h, norm_w.reshape(1, d), tab)


def kernel(x, norm_mix_w, w_in, shift_mu, attn_sinks, decay_w0, decay_w2, iclr_a0, iclr_a2,
           gate_g2, k_k, k_a, r_k, ln_x_w, ln_x_b, proj_attn, proj_rwkv, w_out, norm_ffn_w,
           peer_wq, peer_subkeys, peer_u_tab, peer_v_tab, norm_final_w):
    b, s, d = x.shape
    t = b * s
    h = x.reshape(t, d)
    for l in range(DEPTH):
        w_lay, mu_lay = layout_in_proj(w_in[l], shift_mu[l])
        qkv, zr, gates = in_proj(h, norm_mix_w[l], w_lay)
        y_attn = swa_attention(qkv, attn_sinks[l], b, s)
        r, dec, k, v, kk, nb, g, bonus = rwkv_prep(zr, mu_lay, decay_w0[l], decay_w2[l], iclr_a0[l], iclr_a2[l],
                                                   gate_g2[l], k_k[l], k_a[l], r_k[l].reshape(-1), b, s)
        to_chains = lambda a: jnp.transpose(a.reshape(b, s, RWKV_HEADS, RWKV_HEAD), (1, 3, 0, 2)).reshape(
            s, RWKV_HEAD, b * RWKV_HEADS)
        y = rwkv_scan(to_chains(r), to_chains(dec), to_chains(k), to_chains(v), to_chains(kk), to_chains(nb))
        y = jnp.transpose(y.reshape(s, RWKV_HEAD, b, RWKV_HEADS), (2, 0, 3, 1)).reshape(t, RWKV_DIM)
        h = merge_out(h, y, g, bonus, y_attn, gates, ln_x_w[l], ln_x_b[l], proj_attn[l], proj_rwkv[l], w_out[l])
        xn2, idx, gate = peer_route(h, norm_ffn_w[l], peer_wq[l], peer_subkeys[l])
        act = peer_u(idx, xn2, gate, pack_table(peer_u_tab[l]))
        h = peer_v(idx, act, pack_table(peer_v_tab[l]), h, norm_final_w, final_norm=(l + 1 == DEPTH))
    return h.reshape(b, s, d)
```

```python
import functools

import jax, jax.numpy as jnp
from jax import lax
from jax.experimental import pallas as pl
from jax.experimental.pallas import tpu as pltpu

D_MODEL = 1024
DEPTH = 1

HEAD_DIM = 64
ATTN_HEADS = 8
ATTN_KV_HEADS = 2
ATTN_GROUP = ATTN_HEADS // ATTN_KV_HEADS
ATTN_DIM = ATTN_HEADS * HEAD_DIM
KV_DIM = ATTN_KV_HEADS * HEAD_DIM
WINDOW = 128
BLOCK = WINDOW
ROPE_THETA = 10000.0

RWKV_HEADS = 8
RWKV_HEAD = 64
RWKV_DIM = RWKV_HEADS * RWKV_HEAD
DECAY_LORA = 64
ICLR_LORA = 64
GATE_LORA = 160
RWKV_GN_EPS = 64e-5

PEER_HEADS = 8
N_KEYS = 128
PEER_HALF = 64
PEER_TOPK = 16
PEER_SEL = PEER_HEADS * PEER_TOPK

NORM_EPS = 1e-5
NEG_INF = -1e30

LANES = 128
SUBLANES = 8
VMEM_LIMIT_BYTES = 56 * 1024 * 1024

ROW_WORDS = D_MODEL // 2
ROW_SUBLANES = ROW_WORDS // LANES
NEG_FILL = -3.0e38
HI = lax.Precision.HIGHEST
RWKV_VGROUPS = RWKV_HEAD // SUBLANES

QKV_COLS = ATTN_DIM + 2 * KV_DIM
LORA_PAD = (LANES, LANES, 2 * LANES)
RWKV_COLS = 3 * RWKV_DIM + sum(LORA_PAD)
GATE_COLS = 2 * D_MODEL


def _pad_cols(w, width):
    return jnp.pad(w, ((0, 0), (0, width - w.shape[1])))


def _pad_rows(w, height):
    return jnp.pad(w, ((0, height - w.shape[0]), (0, 0)))


def layout_in_proj(w_in, shift_mu):
    o1 = QKV_COLS
    o_r = o1 + 3 * RWKV_DIM
    o_w = o_r + DECAY_LORA
    o_a = o_w + ICLR_LORA
    o_g = o_a + GATE_LORA
    parts = [w_in[:, :o_r], _pad_cols(w_in[:, o_r:o_w], LORA_PAD[0]), _pad_cols(w_in[:, o_w:o_a], LORA_PAD[1]),
             _pad_cols(w_in[:, o_a:o_g], LORA_PAD[2]), w_in[:, o_g:]]
    mu = shift_mu.reshape(1, -1)
    m_r = 3 * RWKV_DIM
    mu_parts = [mu[:, :m_r], _pad_cols(mu[:, m_r:m_r + DECAY_LORA], LORA_PAD[0]),
                _pad_cols(mu[:, m_r + DECAY_LORA:m_r + DECAY_LORA + ICLR_LORA], LORA_PAD[1]),
                _pad_cols(mu[:, m_r + DECAY_LORA + ICLR_LORA:], LORA_PAD[2])]
    return jnp.concatenate(parts, axis=1), jnp.concatenate(mu_parts, axis=1)


def _in_proj_body(x_ref, nw_ref, w_ref, qkv_ref, zr_ref, gate_ref):
    xf = x_ref[...]
    xn = xf * lax.rsqrt(jnp.mean(xf * xf, axis=-1, keepdims=True) + NORM_EPS) * nw_ref[...]
    z = jnp.dot(xn.astype(jnp.bfloat16), w_ref[...], preferred_element_type=jnp.float32)
    qkv_ref[...] = z[:, :QKV_COLS]
    zr_ref[...] = z[:, QKV_COLS:QKV_COLS + RWKV_COLS]
    gate_ref[...] = jax.nn.sigmoid(z[:, QKV_COLS + RWKV_COLS:]).astype(jnp.bfloat16)


def in_proj(x, norm_w, w_lay, *, tm=256):
    t, d = x.shape
    n = w_lay.shape[1]
    return pl.pallas_call(
        _in_proj_body,
        grid=(t // tm,),
        in_specs=[pl.BlockSpec((tm, d), lambda i: (i, 0)), pl.BlockSpec((1, d), lambda i: (0, 0)),
                  pl.BlockSpec((d, n), lambda i: (0, 0))],
        out_specs=[pl.BlockSpec((tm, QKV_COLS), lambda i: (i, 0)), pl.BlockSpec((tm, RWKV_COLS), lambda i: (i, 0)),
                   pl.BlockSpec((tm, GATE_COLS), lambda i: (i, 0))],
        out_shape=[jax.ShapeDtypeStruct((t, QKV_COLS), jnp.float32), jax.ShapeDtypeStruct((t, RWKV_COLS), jnp.float32),
                   jax.ShapeDtypeStruct((t, GATE_COLS), jnp.bfloat16)],
        compiler_params=pltpu.CompilerParams(dimension_semantics=("arbitrary",), vmem_limit_bytes=VMEM_LIMIT_BYTES),
        name="in_proj",
    )(x, norm_w.reshape(1, d), w_lay.astype(jnp.bfloat16))


def rope_tables(s):
    inv = ROPE_THETA ** (-jnp.arange(0, HEAD_DIM, 2, dtype=jnp.float32) / HEAD_DIM)
    ang = jnp.arange(s, dtype=jnp.float32)[:, None] * inv[None, :]
    cos = jnp.concatenate([jnp.cos(ang), jnp.cos(ang)], axis=-1)
    sin = jnp.concatenate([-jnp.sin(ang), jnp.sin(ang)], axis=-1)
    reps = LANES // HEAD_DIM
    return jnp.tile(cos, (1, reps)), jnp.tile(sin, (1, reps))


def _rope(x, cos, sin, first_half):
    rot = jnp.where(first_half, pltpu.roll(x, LANES - HEAD_DIM // 2, 1), pltpu.roll(x, HEAD_DIM // 2, 1))
    return x * cos + rot * sin


def _attn_body(cur_ref, prev_ref, cos_ref, sin_ref, cosp_ref, sinp_ref, sink_ref, o_ref):
    n = pl.program_id(1)
    lane = lax.broadcasted_iota(jnp.int32, (BLOCK, LANES), 1)
    first_half = (lane % HEAD_DIM) < HEAD_DIM // 2
    low = lane < HEAD_DIM
    cos, sin = cos_ref[...], sin_ref[...]
    cur = cur_ref[0]
    prev = prev_ref[0]
    k_cur = _rope(cur[:, ATTN_DIM:ATTN_DIM + KV_DIM], cos, sin, first_half)
    k_prev = _rope(prev[:, ATTN_DIM:ATTN_DIM + KV_DIM], cosp_ref[...], sinp_ref[...], first_half)
    keys = jnp.concatenate([k_prev, k_cur], axis=0)
    vals = jnp.concatenate([prev[:, ATTN_DIM + KV_DIM:], cur[:, ATTN_DIM + KV_DIM:]], axis=0)
    keys_sw = pltpu.roll(keys, HEAD_DIM, 1)
    vals_sw = pltpu.roll(vals, HEAD_DIM, 1)
    low2 = lax.broadcasted_iota(jnp.int32, (2 * BLOCK, LANES), 1) < HEAD_DIM
    qi = lax.broadcasted_iota(jnp.int32, (BLOCK, 2 * BLOCK), 0) + BLOCK
    kj = lax.broadcasted_iota(jnp.int32, (BLOCK, 2 * BLOCK), 1)
    diff = qi - kj
    mask = (diff >= 0) & (diff < WINDOW) & ((kj >= BLOCK) | (n > 0))
    scale = HEAD_DIM ** -0.5
    outs = []
    for j in range(ATTN_HEADS // 2):
        g = (2 * j) // ATTN_GROUP
        kg = (jnp.where(low2, keys, keys_sw) if g == 0 else jnp.where(low2, keys_sw, keys)).astype(jnp.bfloat16)
        vg = (jnp.where(low2, vals, vals_sw) if g == 0 else jnp.where(low2, vals_sw, vals)).astype(jnp.bfloat16)
        qg = _rope(cur[:, j * LANES:(j + 1) * LANES], cos, sin, first_half) * scale
        halves = []
        for half in range(2):
            qh = jnp.where(low if half == 0 else ~low, qg, 0.0).astype(jnp.bfloat16)
            logits = lax.dot_general(qh, kg, (((1,), (1,)), ((), ())), preferred_element_type=jnp.float32)
            logits = jnp.where(mask, logits, NEG_INF)
            sink = sink_ref[2 * j + half]
            m = jnp.maximum(jnp.max(logits, axis=-1, keepdims=True), sink)
            p = jnp.exp(logits - m)
            denom = jnp.sum(p, axis=-1, keepdims=True) + jnp.exp(sink - m)
            o = jnp.dot(p.astype(jnp.bfloat16), vg, preferred_element_type=jnp.float32)
            halves.append(o / denom)
        outs.append(jnp.where(low, halves[0], halves[1]))
    o_ref[0] = jnp.concatenate(outs, axis=1).astype(o_ref.dtype)


def swa_attention(qkv, sinks, b, s):
    nb = s // BLOCK
    cos, sin = rope_tables(s)
    qkv3 = qkv.reshape(b, s, QKV_COLS)
    cur_spec = pl.BlockSpec((1, BLOCK, QKV_COLS), lambda bi, n: (bi, n, 0))
    prev_spec = pl.BlockSpec((1, BLOCK, QKV_COLS), lambda bi, n: (bi, jnp.maximum(n - 1, 0), 0))
    tab = pl.BlockSpec((BLOCK, LANES), lambda bi, n: (n, 0))
    tabp = pl.BlockSpec((BLOCK, LANES), lambda bi, n: (jnp.maximum(n - 1, 0), 0))
    out = pl.pallas_call(
        _attn_body,
        grid=(b, nb),
        in_specs=[cur_spec, prev_spec, tab, tab, tabp, tabp, pl.BlockSpec(memory_space=pltpu.SMEM)],
        out_specs=pl.BlockSpec((1, BLOCK, ATTN_DIM), lambda bi, n: (bi, n, 0)),
        out_shape=jax.ShapeDtypeStruct((b, s, ATTN_DIM), jnp.bfloat16),
        compiler_params=pltpu.CompilerParams(dimension_semantics=("arbitrary", "arbitrary"),
                                             vmem_limit_bytes=VMEM_LIMIT_BYTES),
        name="swa_attention",
    )(qkv3, qkv3, cos, sin, cos, sin, sinks.astype(jnp.float32))
    return out.reshape(b * s, ATTN_DIM)


def head_block_ones():
    i = jnp.arange(RWKV_DIM) // RWKV_HEAD
    return (i[:, None] == i[None, :]).astype(jnp.float32)


def _softplus(x):
    return jnp.maximum(x, 0.0) + jnp.log1p(jnp.exp(-jnp.abs(x)))


def _rwkv_prep_body(z_ref, zp_ref, mu_ref, w0_ref, w2_ref, a0_ref, a2_ref, g2_ref, kk_w_ref, ka_ref, rk_ref, ones_ref,
                    r_ref, dec_ref, k_ref, v_ref, kk_ref, nb_ref, g_ref, bonus_ref, *, ts):
    z = z_ref[0]
    row = lax.broadcasted_iota(jnp.int32, (ts, 1), 0)
    last_prev = jnp.where(pl.program_id(1) > 0, zp_ref[0][SUBLANES - 1:SUBLANES, :], 0.0)
    z_prev = jnp.where(row == 0, last_prev, pltpu.roll(z, 1, 0))
    zs = z + mu_ref[...] * (z_prev - z)
    c = RWKV_DIM
    r, k, v = zs[:, :c], zs[:, c:2 * c], zs[:, 2 * c:3 * c]
    o = 3 * c
    w_lo = zs[:, o:o + LORA_PAD[0]]
    a_lo = zs[:, o + LORA_PAD[0]:o + LORA_PAD[0] + LORA_PAD[1]]
    g_lo = zs[:, o + LORA_PAD[0] + LORA_PAD[1]:]
    dot = lambda x, w: jnp.dot(x, w, preferred_element_type=jnp.float32, precision=HI)
    w = -_softplus(-(w0_ref[...] + dot(jnp.tanh(w_lo), w2_ref[...]))) - 0.5
    dec_ref[...] = jnp.exp(-jnp.exp(w))
    a = jax.nn.sigmoid(a0_ref[...] + dot(a_lo, a2_ref[...]))
    g_ref[...] = dot(jax.nn.sigmoid(g_lo), g2_ref[...])
    kk = k * kk_w_ref[...]
    kk = kk / jnp.maximum(jnp.sqrt(dot(kk * kk, ones_ref[...])), 1e-12)
    kmod = k * (1.0 + (a - 1.0) * ka_ref[...])
    r_ref[...] = r
    k_ref[...] = kmod
    v_ref[...] = v
    kk_ref[...] = kk
    nb_ref[...] = -kk * a
    bonus_ref[...] = dot(r * kmod * rk_ref[...], ones_ref[...]) * v


def rwkv_prep(zr, mu_lay, w0, w2, a0, a2, g2, k_k, k_a, r_k, b, s, *, ts=256):
    t = b * s
    zr3 = zr.reshape(b, s, RWKV_COLS)
    row = lambda x: x.reshape(1, -1)
    w2p, a2p, g2p = _pad_rows(w2, LORA_PAD[0]), _pad_rows(a2, LORA_PAD[1]), _pad_rows(g2, LORA_PAD[2])
    const = lambda shape: pl.BlockSpec(shape, lambda bi, si: (0,) * len(shape))
    nblk = s // ts
    tok_spec = pl.BlockSpec((ts, RWKV_DIM), lambda bi, si: (bi * nblk + si, 0))
    time_spec = pl.BlockSpec((ts, RWKV_DIM), lambda bi, si: (si, bi))
    tok_shape = jax.ShapeDtypeStruct((t, RWKV_DIM), jnp.float32)
    time_shape = jax.ShapeDtypeStruct((s, b * RWKV_DIM), jnp.float32)
    return pl.pallas_call(
        functools.partial(_rwkv_prep_body, ts=ts),
        grid=(b, nblk),
        in_specs=[pl.BlockSpec((1, ts, RWKV_COLS), lambda bi, si: (bi, si, 0)),
                  pl.BlockSpec((1, SUBLANES, RWKV_COLS), lambda bi, si: (bi, jnp.maximum(si * (ts // SUBLANES) - 1, 0), 0)),
                  const((1, RWKV_COLS)), const((1, RWKV_DIM)), const(w2p.shape), const((1, RWKV_DIM)), const(a2p.shape),
                  const(g2p.shape), const((1, RWKV_DIM)), const((1, RWKV_DIM)), const((1, RWKV_DIM)),
                  const((RWKV_DIM, RWKV_DIM))],
        out_specs=[time_spec] * 6 + [tok_spec] * 2,
        out_shape=[time_shape] * 6 + [tok_shape] * 2,
        compiler_params=pltpu.CompilerParams(dimension_semantics=("arbitrary", "arbitrary"),
                                             vmem_limit_bytes=VMEM_LIMIT_BYTES),
        name="rwkv_prep",
    )(zr3, zr3, mu_lay, row(w0), w2p, row(a0), a2p, g2p, row(k_k), row(k_a), row(r_k), head_block_ones())


SCAN_PARTIALS = 4


def _tree_sum(parts):
    while len(parts) > 1:
        parts = [parts[i] + parts[i + 1] for i in range(0, len(parts), 2)]
    return parts[0]


def _scan_body(r_ref, w_ref, k_ref, v_ref, kk_ref, nb_ref, y_ref, *s_refs, tc):
    @pl.when(pl.program_id(1) == 0)
    def _():
        for s_ref in s_refs:
            s_ref[...] = jnp.zeros_like(s_ref)

    def step(t, carry):
        for g, s_ref in enumerate(s_refs):
            rows = slice(g * SUBLANES, (g + 1) * SUBLANES)
            acc = [None] * SCAN_PARTIALS
            for k in range(RWKV_HEAD):
                p = s_ref[k] * kk_ref[t, pl.ds(k, 1), :]
                acc[k % SCAN_PARTIALS] = p if acc[k % SCAN_PARTIALS] is None else acc[k % SCAN_PARTIALS] + p
            sa = _tree_sum(acc)
            vv = v_ref[t, rows, :]
            yacc = [None] * SCAN_PARTIALS
            for k in range(RWKV_HEAD):
                s_new = s_ref[k] * w_ref[t, pl.ds(k, 1), :] + sa * nb_ref[t, pl.ds(k, 1), :] + vv * k_ref[t, pl.ds(k, 1), :]
                s_ref[k] = s_new
                p = s_new * r_ref[t, pl.ds(k, 1), :]
                yacc[k % SCAN_PARTIALS] = p if yacc[k % SCAN_PARTIALS] is None else yacc[k % SCAN_PARTIALS] + p
            y_ref[t, rows, :] = _tree_sum(yacc)
        return carry

    lax.fori_loop(0, tc, step, 0)


def rwkv_scan(r, w, k, v, kk, nb, *, tc=32):
    s, n, c = r.shape
    spec = pl.BlockSpec((tc, n, LANES), lambda ci, ti: (ti, 0, ci))
    return pl.pallas_call(
        functools.partial(_scan_body, tc=tc),
        grid=(c // LANES, s // tc),
        in_specs=[spec] * 6,
        out_specs=spec,
        out_shape=jax.ShapeDtypeStruct((s, n, c), jnp.float32),
        scratch_shapes=[pltpu.VMEM((n, SUBLANES, LANES), jnp.float32)] * RWKV_VGROUPS,
        compiler_params=pltpu.CompilerParams(dimension_semantics=("arbitrary", "arbitrary"),
                                             vmem_limit_bytes=VMEM_LIMIT_BYTES),
        name="rwkv_scan",
    )(r, w, k, v, kk, nb)


def _merge_body(x_ref, y_ref, g_ref, bonus_ref, ya_ref, gate_ref, lnw_ref, lnb_ref, ones_ref, pa_ref, pb_ref, wo_ref, h_ref):
    dot_hi = lambda x, w: jnp.dot(x, w, preferred_element_type=jnp.float32, precision=HI)
    dot_bf = lambda x, w: jnp.dot(x.astype(jnp.bfloat16), w, preferred_element_type=jnp.float32)
    y = y_ref[...]
    inv_n = 1.0 / RWKV_HEAD
    mean = dot_hi(y, ones_ref[...]) * inv_n
    yc = y - mean
    var = dot_hi(yc * yc, ones_ref[...]) * inv_n
    yn = yc * lax.rsqrt(var + RWKV_GN_EPS) * lnw_ref[...] + lnb_ref[...] + bonus_ref[...]
    y_rwkv = yn * g_ref[...]
    gates = gate_ref[...].astype(jnp.float32)
    merged = gates[:, :D_MODEL] * dot_bf(ya_ref[...], pa_ref[...]) + gates[:, D_MODEL:] * dot_bf(y_rwkv, pb_ref[...])
    h_ref[...] = x_ref[...] + dot_bf(merged, wo_ref[...])


def merge_out(x, y, g, bonus, y_attn, gates, ln_w, ln_b, proj_attn, proj_rwkv, w_out, *, tm=256):
    t, d = x.shape
    s = y.shape[0]
    nblk = s // tm
    row = lambda v: v.reshape(1, -1)
    tok = lambda w: pl.BlockSpec((tm, w), lambda bi, si: (bi * nblk + si, 0))
    const = lambda shape: pl.BlockSpec(shape, lambda bi, si: (0,) * len(shape))
    bf = lambda w: w.astype(jnp.bfloat16)
    return pl.pallas_call(
        _merge_body,
        grid=(t // s, nblk),
        in_specs=[tok(d), pl.BlockSpec((tm, RWKV_DIM), lambda bi, si: (si, bi)), tok(RWKV_DIM), tok(RWKV_DIM),
                  tok(ATTN_DIM), tok(GATE_COLS),
                  const((1, RWKV_DIM)), const((1, RWKV_DIM)), const((RWKV_DIM, RWKV_DIM)),
                  const(proj_attn.shape), const(proj_rwkv.shape), const(w_out.shape)],
        out_specs=tok(d),
        out_shape=jax.ShapeDtypeStruct((t, d), jnp.float32),
        compiler_params=pltpu.CompilerParams(dimension_semantics=("arbitrary", "arbitrary"),
                                             vmem_limit_bytes=VMEM_LIMIT_BYTES),
        name="merge_out",
    )(x, y, g, bonus, y_attn, gates, row(ln_w), row(ln_b), head_block_ones(), bf(proj_attn), bf(proj_rwkv), bf(w_out))


def _top16(problems):
    arrays = [p[0] for p in problems]
    for i in range(PEER_TOPK):
        for j, (_, ids, big, vals_ref, ids_ref) in enumerate(problems):
            a = arrays[j]
            m = jnp.max(a, axis=0, keepdims=True)
            r = jnp.min(jnp.where(a == m, ids, big), axis=0, keepdims=True)
            vals_ref[i:i + 1, :] = m
            ids_ref[i:i + 1, :] = r
            arrays[j] = jnp.where(ids == r, NEG_FILL, a)


def _select_rows(pos, table):
    out = jnp.zeros(pos.shape, table.dtype)
    for a in range(PEER_TOPK):
        out = jnp.where(pos == a, table[a:a + 1, :], out)
    return out


def _peer_route_body(h_ref, nw_ref, wqh_ref, wql_ref, sk_ref, xn_ref, idx_ref, gate_ref, q_ref, idx_t_ref, gate_t_ref,
                     s1_ref, i1_ref, s2_ref, i2_ref, top_ref, pos_ref, *, tm):
    xf = h_ref[...]
    xn = xf * lax.rsqrt(jnp.mean(xf * xf, axis=-1, keepdims=True) + NORM_EPS) * nw_ref[...]
    xn_ref[...] = xn
    xh, xl = _split_bf16(xn)
    dot = lambda a, w_ref: jnp.dot(a, w_ref[...], preferred_element_type=jnp.float32)
    q_ref[...] = dot(xh, wqh_ref) + (dot(xh, wql_ref) + dot(xl, wqh_ref))
    key_ids = lax.broadcasted_iota(jnp.int32, (N_KEYS, LANES), 0)
    sub = lax.broadcasted_iota(jnp.int32, (SUBLANES, LANES), 0)
    cand_ids = jnp.concatenate([sub, sub + SUBLANES] + [sub + a * PEER_TOPK for a in range(1, SUBLANES)]
                               + [(sub + SUBLANES) * PEER_TOPK], axis=0)
    ncol = tm // LANES

    def head(h, carry):
        col0 = pl.multiple_of(h * 2 * PEER_HALF, 2 * PEER_HALF)
        row0 = pl.multiple_of(h * PEER_TOPK, PEER_TOPK)
        for c in range(ncol):
            qh = q_ref[c * LANES:(c + 1) * LANES, pl.ds(col0, 2 * PEER_HALF)]
            sc = lax.dot_general(sk_ref[h], qh, (((1,), (1,)), ((), ())),
                                 preferred_element_type=jnp.float32, precision=HI)
            _top16([(sc[:N_KEYS], key_ids, N_KEYS, s1_ref.at[c], i1_ref.at[c]),
                    (sc[N_KEYS:], key_ids, N_KEYS, s2_ref.at[c], i2_ref.at[c])])
        problems = []
        for c in range(ncol):
            s1, s2 = s1_ref[c], s2_ref[c]
            cand = jnp.concatenate([s1[0:1] + s2[0:SUBLANES], s1[0:1] + s2[SUBLANES:]]
                                   + [s1[a:a + 1] + s2[0:SUBLANES] for a in range(1, SUBLANES)]
                                   + [s1[SUBLANES:] + s2[0:1]], axis=0)
            problems.append((cand, cand_ids, PEER_TOPK * PEER_TOPK, top_ref.at[c], pos_ref.at[c]))
        _top16(problems)
        for c in range(ncol):
            top, pos = top_ref[c], pos_ref[c]
            e = _select_rows(pos // PEER_TOPK, i1_ref[c]) * N_KEYS + _select_rows(pos % PEER_TOPK, i2_ref[c])
            p = jnp.exp(top - top[0:1, :])
            idx_t_ref[pl.ds(row0, PEER_TOPK), c * LANES:(c + 1) * LANES] = e * ROW_SUBLANES
            gate_t_ref[pl.ds(row0, PEER_TOPK), c * LANES:(c + 1) * LANES] = p / jnp.sum(p, axis=0, keepdims=True)
        return carry

    lax.fori_loop(0, PEER_HEADS, head, 0)
    for c in range(ncol):
        idx_ref[c * LANES:(c + 1) * LANES, :] = idx_t_ref[:, c * LANES:(c + 1) * LANES].T
        gate_ref[c * LANES:(c + 1) * LANES, :] = gate_t_ref[:, c * LANES:(c + 1) * LANES].T


def _block_diag_subkeys(subkeys):
    z = jnp.zeros_like(subkeys[:, 0])
    top = jnp.concatenate([subkeys[:, 0], z], axis=-1)
    bot = jnp.concatenate([z, subkeys[:, 1]], axis=-1)
    return jnp.concatenate([top, bot], axis=1)


def peer_route(h, norm_w, wq, subkeys, *, tm=256):
    t, d = h.shape
    sk = _block_diag_subkeys(subkeys)
    return pl.pallas_call(
        functools.partial(_peer_route_body, tm=tm),
        grid=(t // tm,),
        in_specs=[
            pl.BlockSpec((tm, d), lambda i: (i, 0)),
            pl.BlockSpec((1, d), lambda i: (0, 0)),
            pl.BlockSpec(wq.shape, lambda i: (0, 0)),
            pl.BlockSpec(wq.shape, lambda i: (0, 0)),
            pl.BlockSpec(sk.shape, lambda i: (0, 0, 0)),
        ],
        out_specs=[pl.BlockSpec((tm, d), lambda i: (i, 0)),
                   pl.BlockSpec((tm, PEER_SEL), lambda i: (i, 0)),
                   pl.BlockSpec((tm, PEER_SEL), lambda i: (i, 0))],
        out_shape=[jax.ShapeDtypeStruct((t, d), jnp.float32),
                   jax.ShapeDtypeStruct((t, PEER_SEL), jnp.int32),
                   jax.ShapeDtypeStruct((t, PEER_SEL), jnp.float32)],
        scratch_shapes=[pltpu.VMEM((tm, d), jnp.float32), pltpu.VMEM((PEER_SEL, tm), jnp.int32),
                        pltpu.VMEM((PEER_SEL, tm), jnp.float32)]
                       + [pltpu.VMEM((tm // LANES, PEER_TOPK, LANES), jnp.float32),
                          pltpu.VMEM((tm // LANES, PEER_TOPK, LANES), jnp.int32)] * 3,
        compiler_params=pltpu.CompilerParams(dimension_semantics=("arbitrary",), vmem_limit_bytes=VMEM_LIMIT_BYTES),
        name="peer_route",
    )(h, norm_w.reshape(1, d), *_split_bf16(wq), sk)


def pack_table(tab):
    n, d = tab.shape
    b = lax.bitcast_convert_type(tab.astype(jnp.bfloat16), jnp.uint16).astype(jnp.uint32)
    w = b[:, : d // 2] | (b[:, d // 2:] << 16)
    return lax.bitcast_convert_type(w, jnp.int32).reshape(n * ROW_SUBLANES, LANES)


def _unpack(w):
    lo = lax.bitcast_convert_type(w << 16, jnp.float32)
    hi = lax.bitcast_convert_type(w & jnp.int32(-65536), jnp.float32)
    return lo, hi


STAGE_STRIDE = 136
STAGE_ROWS = ROW_SUBLANES * STAGE_STRIDE
STAGE_TOKENS = 8


def _load_table(tab_hbm, tab_vmem, sem, st_first_read):
    @pl.when(pl.program_id(0) == 0)
    def _():
        cp = pltpu.make_async_copy(tab_hbm, tab_vmem, sem)
        cp.start()
        st_first_read[...] = jnp.zeros_like(st_first_read)
        cp.wait()


def _split_bf16(x):
    hi = x.astype(jnp.bfloat16)
    return hi, (x - hi.astype(jnp.float32)).astype(jnp.bfloat16)


def _stage_rows(i, k):
    return pl.ds(i * STAGE_ROWS + k, ROW_SUBLANES, stride=STAGE_STRIDE)


def _gather_u(idx_ref, x_ref, t0, tab_vmem, st_ref):
    for i in range(STAGE_TOKENS):
        xrow = x_ref[pl.ds(t0 + i, 1), :]
        xl = jnp.concatenate([xrow[:, s * LANES:(s + 1) * LANES] for s in range(ROW_SUBLANES)], axis=0)
        xh = jnp.concatenate([xrow[:, s * LANES:(s + 1) * LANES] for s in range(ROW_SUBLANES, 2 * ROW_SUBLANES)], axis=0)
        for k in range(PEER_SEL):
            e = idx_ref[t0 + i, k]
            lo, hi = _unpack(tab_vmem[pl.ds(pl.multiple_of(e, ROW_SUBLANES), ROW_SUBLANES), :])
            st_ref[_stage_rows(i, k), :] = lo * xl + hi * xh


def _dot_products(st_ref, ones, eye):
    sums = jnp.dot(st_ref[...].astype(jnp.bfloat16), ones, preferred_element_type=jnp.float32)
    rows = []
    for i in range(STAGE_TOKENS):
        r0 = i * STAGE_ROWS
        per_pair = (sums[r0:r0 + PEER_SEL] + sums[r0 + STAGE_STRIDE:r0 + STAGE_STRIDE + PEER_SEL]
                    + sums[r0 + 2 * STAGE_STRIDE:r0 + 2 * STAGE_STRIDE + PEER_SEL]
                    + sums[r0 + 3 * STAGE_STRIDE:r0 + 3 * STAGE_STRIDE + PEER_SEL])
        rows.append(jnp.sum(jnp.where(eye, per_pair, 0.0), axis=0, keepdims=True))
    return jnp.concatenate(rows, axis=0)


def _peer_u_body(idx_ref, x_ref, gate_ref, tab_hbm, act_ref, tab_vmem, st_a, st_b, sem, *, tb):
    _load_table(tab_hbm, tab_vmem, sem, st_b)
    n = STAGE_TOKENS
    ones = jnp.ones((LANES, LANES), jnp.bfloat16)
    eye = lax.broadcasted_iota(jnp.int32, (PEER_SEL, LANES), 0) == lax.broadcasted_iota(jnp.int32, (PEER_SEL, LANES), 1)

    def two_groups(j, carry):
        t0 = pl.multiple_of(j * 2 * n, 2 * n)
        tp = pl.multiple_of(jnp.maximum(t0 - n, 0), n)
        act_ref[pl.ds(tp, n), :] = _dot_products(st_b, ones, eye)
        _gather_u(idx_ref, x_ref, t0, tab_vmem, st_a)
        act_ref[pl.ds(t0, n), :] = _dot_products(st_a, ones, eye)
        _gather_u(idx_ref, x_ref, t0 + n, tab_vmem, st_b)
        return carry

    lax.fori_loop(0, tb // (2 * n), two_groups, 0)
    act_ref[tb - n:tb, :] = _dot_products(st_b, ones, eye)
    hp = act_ref[...]
    act_ref[...] = 0.5 * hp * (1.0 + lax.erf(hp * (2.0 ** -0.5))) * gate_ref[...]


def peer_u(idx, x, gate, tab, *, tb=128):
    t, d = x.shape
    stage = pltpu.VMEM((STAGE_TOKENS * STAGE_ROWS, LANES), jnp.float32)
    return pl.pallas_call(
        functools.partial(_peer_u_body, tb=tb),
        grid=(t // tb,),
        in_specs=[
            pl.BlockSpec((tb, PEER_SEL), lambda i: (i, 0), memory_space=pltpu.SMEM),
            pl.BlockSpec((tb, d), lambda i: (i, 0)),
            pl.BlockSpec((tb, PEER_SEL), lambda i: (i, 0)),
            pl.BlockSpec(memory_space=pl.ANY),
        ],
        out_specs=pl.BlockSpec((tb, PEER_SEL), lambda i: (i, 0)),
        out_shape=jax.ShapeDtypeStruct((t, PEER_SEL), jnp.float32),
        scratch_shapes=[pltpu.VMEM(tab.shape, jnp.int32), stage, stage, pltpu.SemaphoreType.DMA],
        compiler_params=pltpu.CompilerParams(dimension_semantics=("arbitrary",), vmem_limit_bytes=VMEM_LIMIT_BYTES),
        name="peer_u",
    )(idx, x, gate, tab)


def _gather_v(idx_ref, t, i, tab_vmem, st_ref):
    for k in range(PEER_SEL):
        e = idx_ref[t, k]
        st_ref[_stage_rows(i, k), :] = tab_vmem[pl.ds(pl.multiple_of(e, ROW_SUBLANES), ROW_SUBLANES), :]


def _weighted_sum(st_ref, i, act_row):
    ah, al = _split_bf16(act_row)
    lhs = jnp.concatenate([ah, al, jnp.zeros((SUBLANES - 2, PEER_SEL), jnp.bfloat16)], axis=0)
    r0 = i * STAGE_ROWS
    w = jnp.concatenate([st_ref[r0 + s * STAGE_STRIDE:r0 + s * STAGE_STRIDE + PEER_SEL, :] for s in range(ROW_SUBLANES)],
                        axis=1)
    lo, hi = _unpack(w)
    r = jnp.dot(lhs, jnp.concatenate([lo, hi], axis=1).astype(jnp.bfloat16), preferred_element_type=jnp.float32)
    return r[0:1] + r[1:2]


def _peer_v_body(idx_ref, act_ref, h_ref, nw_ref, tab_hbm, o_ref, tab_vmem, st_a, st_b, sem, *, tb, final_norm):
    _load_table(tab_hbm, tab_vmem, sem, st_b)
    n = STAGE_TOKENS

    def half(t_gather, st_fill, t_mxu, st_read):
        rows = []
        for i in range(n):
            _gather_v(idx_ref, t_gather + i, i, tab_vmem, st_fill)
            rows.append(_weighted_sum(st_read, i, act_ref[pl.ds(t_mxu + i, 1), :]))
        o_ref[pl.ds(t_mxu, n), :] = jnp.concatenate(rows, axis=0)

    def two_groups(j, carry):
        t0 = pl.multiple_of(j * 2 * n, 2 * n)
        tp = pl.multiple_of(jnp.maximum(t0 - n, 0), n)
        half(t0, st_a, tp, st_b)
        half(t0 + n, st_b, t0, st_a)
        return carry

    lax.fori_loop(0, tb // (2 * n), two_groups, 0)
    o_ref[tb - n:tb, :] = jnp.concatenate(
        [_weighted_sum(st_b, i, act_ref[tb - n + i:tb - n + i + 1, :]) for i in range(n)], axis=0)
    y = h_ref[...] + o_ref[...]
    if final_norm:
        y = y * lax.rsqrt(jnp.mean(y * y, axis=-1, keepdims=True) + NORM_EPS) * nw_ref[...]
    o_ref[...] = y


def peer_v(idx, act, tab, h, norm_w, *, final_norm, tb=128):
    t, d = h.shape
    stage = pltpu.VMEM((STAGE_TOKENS * STAGE_ROWS, LANES), jnp.int32)
    return pl.pallas_call(
        functools.partial(_peer_v_body, tb=tb, final_norm=final_norm),
        grid=(t // tb,),
        in_specs=[
            pl.BlockSpec((tb, PEER_SEL), lambda i: (i, 0), memory_space=pltpu.SMEM),
            pl.BlockSpec((tb, PEER_SEL), lambda i: (i, 0)),
            pl.BlockSpec((tb, d), lambda i: (i, 0)),
            pl.BlockSpec((1, d), lambda i: (0, 0)),
            pl.BlockSpec(memory_space=pl.ANY),
        ],
        out_specs=pl.BlockSpec((tb, d), lambda i: (i, 0)),
        out_shape=jax.ShapeDtypeStruct((t, d), jnp.float32),
        scratch_shapes=[pltpu.VMEM(tab.shape, jnp.int32), stage, stage, pltpu.SemaphoreType.DMA],
        compiler_params=pltpu.CompilerParams(dimension_semantics=("arbitrary",), vmem_limit_bytes=VMEM_LIMIT_BYTES),
        name="peer_v",
    )(idx, act, h, norm_w.reshape(1, d), tab)


def kernel(x, norm_mix_w, w_in, shift_mu, attn_sinks, decay_w0, decay_w2, iclr_a0, iclr_a2,
           gate_g2, k_k, k_a, r_k, ln_x_w, ln_x_b, proj_attn, proj_rwkv, w_out, norm_ffn_w,
           peer_wq, peer_subkeys, peer_u_tab, peer_v_tab, norm_final_w):
    b, s, d = x.shape
    t = b * s
    h = x.reshape(t, d)
    for l in range(DEPTH):
        w_lay, mu_lay = layout_in_proj(w_in[l], shift_mu[l])
        qkv, zr, gates = in_proj(h, norm_mix_w[l], w_lay)
        y_attn = swa_attention(qkv, attn_sinks[l], b, s)
        r, dec, k, v, kk, nb, g, bonus = rwkv_prep(zr, mu_lay, decay_w0[l], decay_w2[l], iclr_a0[l], iclr_a2[l],
                                                   gate_g2[l], k_k[l], k_a[l], r_k[l].reshape(-1), b, s)
        to_chains = lambda a: jnp.swapaxes(a.reshape(s, b * RWKV_HEADS, RWKV_HEAD), 1, 2)
        y = rwkv_scan(to_chains(r), to_chains(dec), to_chains(k), to_chains(v), to_chains(kk), to_chains(nb))
        y = jnp.swapaxes(y, 1, 2).reshape(s, b * RWKV_DIM)
        h = merge_out(h, y, g, bonus, y_attn, gates, ln_x_w[l], ln_x_b[l], proj_attn[l], proj_rwkv[l], w_out[l])
        xn2, idx, gate = peer_route(h, norm_ffn_w[l], peer_wq[l], peer_subkeys[l])
        act = peer_u(idx, xn2, gate, pack_table(peer_u_tab[l]))
        h = peer_v(idx, act, pack_table(peer_v_tab[l]), h, norm_final_w, final_norm=(l + 1 == DEPTH))
    return h.reshape(b, s, d)
```

```python
import functools

import jax, jax.numpy as jnp
from jax import lax
from jax.experimental import pallas as pl
from jax.experimental.pallas import tpu as pltpu

D_MODEL = 1024
DEPTH = 1

HEAD_DIM = 64
ATTN_HEADS = 8
ATTN_KV_HEADS = 2
ATTN_GROUP = ATTN_HEADS // ATTN_KV_HEADS
ATTN_DIM = ATTN_HEADS * HEAD_DIM
KV_DIM = ATTN_KV_HEADS * HEAD_DIM
WINDOW = 128
BLOCK = WINDOW
ROPE_THETA = 10000.0

RWKV_HEADS = 8
RWKV_HEAD = 64
RWKV_DIM = RWKV_HEADS * RWKV_HEAD
DECAY_LORA = 64
ICLR_LORA = 64
GATE_LORA = 160
RWKV_GN_EPS = 64e-5

PEER_HEADS = 8
N_KEYS = 128
PEER_HALF = 64
PEER_TOPK = 16
PEER_SEL = PEER_HEADS * PEER_TOPK

NORM_EPS = 1e-5
NEG_INF = -1e30

LANES = 128
SUBLANES = 8
VMEM_LIMIT_BYTES = 56 * 1024 * 1024

ROW_WORDS = D_MODEL // 2
ROW_SUBLANES = ROW_WORDS // LANES
NEG_FILL = -3.0e38
HI = lax.Precision.HIGHEST
RWKV_VGROUPS = RWKV_HEAD // SUBLANES

QKV_COLS = ATTN_DIM + 2 * KV_DIM
LORA_PAD = (LANES, LANES, 2 * LANES)
RWKV_COLS = 3 * RWKV_DIM + sum(LORA_PAD)
GATE_COLS = 2 * D_MODEL


def _pad_cols(w, width):
    return jnp.pad(w, ((0, 0), (0, width - w.shape[1])))


def _pad_rows(w, height):
    return jnp.pad(w, ((0, height - w.shape[0]), (0, 0)))


def layout_in_proj(w_in, shift_mu):
    o1 = QKV_COLS
    o_r = o1 + 3 * RWKV_DIM
    o_w = o_r + DECAY_LORA
    o_a = o_w + ICLR_LORA
    o_g = o_a + GATE_LORA
    parts = [w_in[:, :o_r], _pad_cols(w_in[:, o_r:o_w], LORA_PAD[0]), _pad_cols(w_in[:, o_w:o_a], LORA_PAD[1]),
             _pad_cols(w_in[:, o_a:o_g], LORA_PAD[2]), w_in[:, o_g:]]
    mu = shift_mu.reshape(1, -1)
    m_r = 3 * RWKV_DIM
    mu_parts = [mu[:, :m_r], _pad_cols(mu[:, m_r:m_r + DECAY_LORA], LORA_PAD[0]),
                _pad_cols(mu[:, m_r + DECAY_LORA:m_r + DECAY_LORA + ICLR_LORA], LORA_PAD[1]),
                _pad_cols(mu[:, m_r + DECAY_LORA + ICLR_LORA:], LORA_PAD[2])]
    return jnp.concatenate(parts, axis=1), jnp.concatenate(mu_parts, axis=1)


def _in_proj_body(x_ref, nw_ref, w_ref, qkv_ref, zr_ref, gate_ref):
    xf = x_ref[...]
    xn = xf * lax.rsqrt(jnp.mean(xf * xf, axis=-1, keepdims=True) + NORM_EPS) * nw_ref[...]
    z = jnp.dot(xn.astype(jnp.bfloat16), w_ref[...], preferred_element_type=jnp.float32)
    qkv_ref[...] = z[:, :QKV_COLS]
    zr_ref[...] = z[:, QKV_COLS:QKV_COLS + RWKV_COLS]
    gate_ref[...] = jax.nn.sigmoid(z[:, QKV_COLS + RWKV_COLS:]).astype(jnp.bfloat16)


def in_proj(x, norm_w, w_lay, *, tm=256):
    t, d = x.shape
    n = w_lay.shape[1]
    return pl.pallas_call(
        _in_proj_body,
        grid=(t // tm,),
        in_specs=[pl.BlockSpec((tm, d), lambda i: (i, 0)), pl.BlockSpec((1, d), lambda i: (0, 0)),
                  pl.BlockSpec((d, n), lambda i: (0, 0))],
        out_specs=[pl.BlockSpec((tm, QKV_COLS), lambda i: (i, 0)), pl.BlockSpec((tm, RWKV_COLS), lambda i: (i, 0)),
                   pl.BlockSpec((tm, GATE_COLS), lambda i: (i, 0))],
        out_shape=[jax.ShapeDtypeStruct((t, QKV_COLS), jnp.float32), jax.ShapeDtypeStruct((t, RWKV_COLS), jnp.float32),
                   jax.ShapeDtypeStruct((t, GATE_COLS), jnp.bfloat16)],
        compiler_params=pltpu.CompilerParams(dimension_semantics=("arbitrary",), vmem_limit_bytes=VMEM_LIMIT_BYTES),
        name="in_proj",
    )(x, norm_w.reshape(1, d), w_lay.astype(jnp.bfloat16))


def rope_tables(s):
    inv = ROPE_THETA ** (-jnp.arange(0, HEAD_DIM, 2, dtype=jnp.float32) / HEAD_DIM)
    ang = jnp.arange(s, dtype=jnp.float32)[:, None] * inv[None, :]
    cos = jnp.concatenate([jnp.cos(ang), jnp.cos(ang)], axis=-1)
    sin = jnp.concatenate([-jnp.sin(ang), jnp.sin(ang)], axis=-1)
    reps = LANES // HEAD_DIM
    return jnp.tile(cos, (1, reps)), jnp.tile(sin, (1, reps))


def _rope(x, cos, sin, first_half):
    rot = jnp.where(first_half, pltpu.roll(x, LANES - HEAD_DIM // 2, 1), pltpu.roll(x, HEAD_DIM // 2, 1))
    return x * cos + rot * sin


def _attn_body(cur_ref, prev_ref, cos_ref, sin_ref, cosp_ref, sinp_ref, sink_ref, o_ref):
    n = pl.program_id(1)
    lane = lax.broadcasted_iota(jnp.int32, (BLOCK, LANES), 1)
    first_half = (lane % HEAD_DIM) < HEAD_DIM // 2
    low = lane < HEAD_DIM
    cos, sin = cos_ref[...], sin_ref[...]
    cur = cur_ref[0]
    prev = prev_ref[0]
    k_cur = _rope(cur[:, ATTN_DIM:ATTN_DIM + KV_DIM], cos, sin, first_half)
    k_prev = _rope(prev[:, ATTN_DIM:ATTN_DIM + KV_DIM], cosp_ref[...], sinp_ref[...], first_half)
    keys = jnp.concatenate([k_prev, k_cur], axis=0)
    vals = jnp.concatenate([prev[:, ATTN_DIM + KV_DIM:], cur[:, ATTN_DIM + KV_DIM:]], axis=0)
    keys_sw = pltpu.roll(keys, HEAD_DIM, 1)
    vals_sw = pltpu.roll(vals, HEAD_DIM, 1)
    low2 = lax.broadcasted_iota(jnp.int32, (2 * BLOCK, LANES), 1) < HEAD_DIM
    qi = lax.broadcasted_iota(jnp.int32, (BLOCK, 2 * BLOCK), 0) + BLOCK
    kj = lax.broadcasted_iota(jnp.int32, (BLOCK, 2 * BLOCK), 1)
    diff = qi - kj
    mask = (diff >= 0) & (diff < WINDOW) & ((kj >= BLOCK) | (n > 0))
    scale = HEAD_DIM ** -0.5
    outs = []
    for j in range(ATTN_HEADS // 2):
        g = (2 * j) // ATTN_GROUP
        kg = (jnp.where(low2, keys, keys_sw) if g == 0 else jnp.where(low2, keys_sw, keys)).astype(jnp.bfloat16)
        vg = (jnp.where(low2, vals, vals_sw) if g == 0 else jnp.where(low2, vals_sw, vals)).astype(jnp.bfloat16)
        qg = _rope(cur[:, j * LANES:(j + 1) * LANES], cos, sin, first_half) * scale
        halves = []
        for half in range(2):
            qh = jnp.where(low if half == 0 else ~low, qg, 0.0).astype(jnp.bfloat16)
            logits = lax.dot_general(qh, kg, (((1,), (1,)), ((), ())), preferred_element_type=jnp.float32)
            logits = jnp.where(mask, logits, NEG_INF)
            sink = sink_ref[2 * j + half]
            m = jnp.maximum(jnp.max(logits, axis=-1, keepdims=True), sink)
            p = jnp.exp(logits - m)
            denom = jnp.sum(p, axis=-1, keepdims=True) + jnp.exp(sink - m)
            o = jnp.dot(p.astype(jnp.bfloat16), vg, preferred_element_type=jnp.float32)
            halves.append(o / denom)
        outs.append(jnp.where(low, halves[0], halves[1]))
    o_ref[0] = jnp.concatenate(outs, axis=1).astype(o_ref.dtype)


def swa_attention(qkv, sinks, b, s):
    nb = s // BLOCK
    cos, sin = rope_tables(s)
    qkv3 = qkv.reshape(b, s, QKV_COLS)
    cur_spec = pl.BlockSpec((1, BLOCK, QKV_COLS), lambda bi, n: (bi, n, 0))
    prev_spec = pl.BlockSpec((1, BLOCK, QKV_COLS), lambda bi, n: (bi, jnp.maximum(n - 1, 0), 0))
    tab = pl.BlockSpec((BLOCK, LANES), lambda bi, n: (n, 0))
    tabp = pl.BlockSpec((BLOCK, LANES), lambda bi, n: (jnp.maximum(n - 1, 0), 0))
    out = pl.pallas_call(
        _attn_body,
        grid=(b, nb),
        in_specs=[cur_spec, prev_spec, tab, tab, tabp, tabp, pl.BlockSpec(memory_space=pltpu.SMEM)],
        out_specs=pl.BlockSpec((1, BLOCK, ATTN_DIM), lambda bi, n: (bi, n, 0)),
        out_shape=jax.ShapeDtypeStruct((b, s, ATTN_DIM), jnp.bfloat16),
        compiler_params=pltpu.CompilerParams(dimension_semantics=("arbitrary", "arbitrary"),
                                             vmem_limit_bytes=VMEM_LIMIT_BYTES),
        name="swa_attention",
    )(qkv3, qkv3, cos, sin, cos, sin, sinks.astype(jnp.float32))
    return out.reshape(b * s, ATTN_DIM)


def head_block_ones():
    i = jnp.arange(RWKV_DIM) // RWKV_HEAD
    return (i[:, None] == i[None, :]).astype(jnp.float32)


def _softplus(x):
    return jnp.maximum(x, 0.0) + jnp.log1p(jnp.exp(-jnp.abs(x)))


def _rwkv_prep_body(z_ref, zp_ref, mu_ref, w0_ref, w2_ref, a0_ref, a2_ref, g2_ref, kk_w_ref, ka_ref, rk_ref, ones_ref,
                    r_ref, dec_ref, k_ref, v_ref, kk_ref, nb_ref, g_ref, bonus_ref, *, ts):
    z = z_ref[0]
    row = lax.broadcasted_iota(jnp.int32, (ts, 1), 0)
    last_prev = jnp.where(pl.program_id(1) > 0, zp_ref[0][SUBLANES - 1:SUBLANES, :], 0.0)
    z_prev = jnp.where(row == 0, last_prev, pltpu.roll(z, 1, 0))
    zs = z + mu_ref[...] * (z_prev - z)
    c = RWKV_DIM
    r, k, v = zs[:, :c], zs[:, c:2 * c], zs[:, 2 * c:3 * c]
    o = 3 * c
    w_lo = zs[:, o:o + LORA_PAD[0]]
    a_lo = zs[:, o + LORA_PAD[0]:o + LORA_PAD[0] + LORA_PAD[1]]
    g_lo = zs[:, o + LORA_PAD[0] + LORA_PAD[1]:]
    dot = lambda x, w: jnp.dot(x, w, preferred_element_type=jnp.float32, precision=HI)
    w = -_softplus(-(w0_ref[...] + dot(jnp.tanh(w_lo), w2_ref[...]))) - 0.5
    dec_ref[...] = jnp.exp(-jnp.exp(w))
    a = jax.nn.sigmoid(a0_ref[...] + dot(a_lo, a2_ref[...]))
    g_ref[...] = dot(jax.nn.sigmoid(g_lo), g2_ref[...])
    kk = k * kk_w_ref[...]
    kk = kk / jnp.maximum(jnp.sqrt(dot(kk * kk, ones_ref[...])), 1e-12)
    kmod = k * (1.0 + (a - 1.0) * ka_ref[...])
    r_ref[...] = r
    k_ref[...] = kmod
    v_ref[...] = v
    kk_ref[...] = kk
    nb_ref[...] = -kk * a
    bonus_ref[...] = dot(r * kmod * rk_ref[...], ones_ref[...]) * v


def rwkv_prep(zr, mu_lay, w0, w2, a0, a2, g2, k_k, k_a, r_k, b, s, *, ts=256):
    t = b * s
    zr3 = zr.reshape(b, s, RWKV_COLS)
    row = lambda x: x.reshape(1, -1)
    w2p, a2p, g2p = _pad_rows(w2, LORA_PAD[0]), _pad_rows(a2, LORA_PAD[1]), _pad_rows(g2, LORA_PAD[2])
    const = lambda shape: pl.BlockSpec(shape, lambda bi, si: (0,) * len(shape))
    nblk = s // ts
    tok_spec = pl.BlockSpec((ts, RWKV_DIM), lambda bi, si: (bi * nblk + si, 0))
    time_spec = pl.BlockSpec((ts, RWKV_DIM), lambda bi, si: (si, bi))
    tok_shape = jax.ShapeDtypeStruct((t, RWKV_DIM), jnp.float32)
    time_shape = jax.ShapeDtypeStruct((s, b * RWKV_DIM), jnp.float32)
    return pl.pallas_call(
        functools.partial(_rwkv_prep_body, ts=ts),
        grid=(b, nblk),
        in_specs=[pl.BlockSpec((1, ts, RWKV_COLS), lambda bi, si: (bi, si, 0)),
                  pl.BlockSpec((1, SUBLANES, RWKV_COLS), lambda bi, si: (bi, jnp.maximum(si * (ts // SUBLANES) - 1, 0), 0)),
                  const((1, RWKV_COLS)), const((1, RWKV_DIM)), const(w2p.shape), const((1, RWKV_DIM)), const(a2p.shape),
                  const(g2p.shape), const((1, RWKV_DIM)), const((1, RWKV_DIM)), const((1, RWKV_DIM)),
                  const((RWKV_DIM, RWKV_DIM))],
        out_specs=[time_spec] * 6 + [tok_spec] * 2,
        out_shape=[time_shape] * 6 + [tok_shape] * 2,
        compiler_params=pltpu.CompilerParams(dimension_semantics=("arbitrary", "arbitrary"),
                                             vmem_limit_bytes=VMEM_LIMIT_BYTES),
        name="rwkv_prep",
    )(zr3, zr3, mu_lay, row(w0), w2p, row(a0), a2p, g2p, row(k_k), row(k_a), row(r_k), head_block_ones())


SCAN_PARTIALS = 4


def _tree_sum(parts):
    while len(parts) > 1:
        parts = [parts[i] + parts[i + 1] for i in range(0, len(parts), 2)]
    return parts[0]


def _scan_body(r_ref, w_ref, k_ref, v_ref, kk_ref, nb_ref, y_ref, *s_refs, tc):
    @pl.when(pl.program_id(1) == 0)
    def _():
        for s_ref in s_refs:
            s_ref[...] = jnp.zeros_like(s_ref)

    def step(t, carry):
        for g, s_ref in enumerate(s_refs):
            rows = slice(g * SUBLANES, (g + 1) * SUBLANES)
            acc = [None] * SCAN_PARTIALS
            for k in range(RWKV_HEAD):
                p = s_ref[k] * kk_ref[t, pl.ds(k, 1), :]
                acc[k % SCAN_PARTIALS] = p if acc[k % SCAN_PARTIALS] is None else acc[k % SCAN_PARTIALS] + p
            sa = _tree_sum(acc)
            vv = v_ref[t, rows, :]
            yacc = [None] * SCAN_PARTIALS
            for k in range(RWKV_HEAD):
                s_new = s_ref[k] * w_ref[t, pl.ds(k, 1), :] + sa * nb_ref[t, pl.ds(k, 1), :] + vv * k_ref[t, pl.ds(k, 1), :]
                s_ref[k] = s_new
                p = s_new * r_ref[t, pl.ds(k, 1), :]
                yacc[k % SCAN_PARTIALS] = p if yacc[k % SCAN_PARTIALS] is None else yacc[k % SCAN_PARTIALS] + p
            y_ref[t, rows, :] = _tree_sum(yacc)
        return carry

    lax.fori_loop(0, tc, step, 0)


def rwkv_scan(r, w, k, v, kk, nb, *, tc=32):
    s, n, c = r.shape
    spec = pl.BlockSpec((tc, n, LANES), lambda ci, ti: (ti, 0, ci))
    return pl.pallas_call(
        functools.partial(_scan_body, tc=tc),
        grid=(c // LANES, s // tc),
        in_specs=[spec] * 6,
        out_specs=spec,
        out_shape=jax.ShapeDtypeStruct((s, n, c), jnp.float32),
        scratch_shapes=[pltpu.VMEM((n, SUBLANES, LANES), jnp.float32)] * RWKV_VGROUPS,
        compiler_params=pltpu.CompilerParams(dimension_semantics=("arbitrary", "arbitrary"),
                                             vmem_limit_bytes=VMEM_LIMIT_BYTES),
        name="rwkv_scan",
    )(r, w, k, v, kk, nb)


def _merge_body(x_ref, y_ref, g_ref, bonus_ref, ya_ref, gate_ref, lnw_ref, lnb_ref, ones_ref, pa_ref, pb_ref, wo_ref, h_ref):
    dot_hi = lambda x, w: jnp.dot(x, w, preferred_element_type=jnp.float32, precision=HI)
    dot_bf = lambda x, w: jnp.dot(x.astype(jnp.bfloat16), w, preferred_element_type=jnp.float32)
    y = y_ref[...]
    inv_n = 1.0 / RWKV_HEAD
    mean = dot_hi(y, ones_ref[...]) * inv_n
    yc = y - mean
    var = dot_hi(yc * yc, ones_ref[...]) * inv_n
    yn = yc * lax.rsqrt(var + RWKV_GN_EPS) * lnw_ref[...] + lnb_ref[...] + bonus_ref[...]
    y_rwkv = yn * g_ref[...]
    gates = gate_ref[...].astype(jnp.float32)
    merged = gates[:, :D_MODEL] * dot_bf(ya_ref[...], pa_ref[...]) + gates[:, D_MODEL:] * dot_bf(y_rwkv, pb_ref[...])
    h_ref[...] = x_ref[...] + dot_bf(merged, wo_ref[...])


def merge_out(x, y, g, bonus, y_attn, gates, ln_w, ln_b, proj_attn, proj_rwkv, w_out, *, tm=256):
    t, d = x.shape
    s = y.shape[0]
    nblk = s // tm
    row = lambda v: v.reshape(1, -1)
    tok = lambda w: pl.BlockSpec((tm, w), lambda bi, si: (bi * nblk + si, 0))
    const = lambda shape: pl.BlockSpec(shape, lambda bi, si: (0,) * len(shape))
    bf = lambda w: w.astype(jnp.bfloat16)
    return pl.pallas_call(
        _merge_body,
        grid=(t // s, nblk),
        in_specs=[tok(d), pl.BlockSpec((tm, RWKV_DIM), lambda bi, si: (si, bi)), tok(RWKV_DIM), tok(RWKV_DIM),
                  tok(ATTN_DIM), tok(GATE_COLS),
                  const((1, RWKV_DIM)), const((1, RWKV_DIM)), const((RWKV_DIM, RWKV_DIM)),
                  const(proj_attn.shape), const(proj_rwkv.shape), const(w_out.shape)],
        out_specs=tok(d),
        out_shape=jax.ShapeDtypeStruct((t, d), jnp.float32),
        compiler_params=pltpu.CompilerParams(dimension_semantics=("arbitrary", "arbitrary"),
                                             vmem_limit_bytes=VMEM_LIMIT_BYTES),
        name="merge_out",
    )(x, y, g, bonus, y_attn, gates, row(ln_w), row(ln_b), head_block_ones(), bf(proj_attn), bf(proj_rwkv), bf(w_out))


def _top16(problems):
    arrays = [p[0] for p in problems]
    for i in range(PEER_TOPK):
        for j, (_, ids, big, vals_ref, ids_ref) in enumerate(problems):
            a = arrays[j]
            m = jnp.max(a, axis=0, keepdims=True)
            r = jnp.min(jnp.where(a == m, ids, big), axis=0, keepdims=True)
            vals_ref[i:i + 1, :] = m
            ids_ref[i:i + 1, :] = r
            arrays[j] = jnp.where(ids == r, NEG_FILL, a)


def _select_rows(pos, table):
    out = jnp.zeros(pos.shape, table.dtype)
    for a in range(PEER_TOPK):
        out = jnp.where(pos == a, table[a:a + 1, :], out)
    return out


def _peer_route_body(h_ref, nw_ref, wqh_ref, wql_ref, sk_ref, xn_ref, idx_ref, gate_ref, q_ref, idx_t_ref, gate_t_ref,
                     s1_ref, i1_ref, s2_ref, i2_ref, top_ref, pos_ref, *, tm):
    xf = h_ref[...]
    xn = xf * lax.rsqrt(jnp.mean(xf * xf, axis=-1, keepdims=True) + NORM_EPS) * nw_ref[...]
    xn_ref[...] = xn
    xh, xl = _split_bf16(xn)
    dot = lambda a, w_ref: jnp.dot(a, w_ref[...], preferred_element_type=jnp.float32)
    q_ref[...] = dot(xh, wqh_ref) + (dot(xh, wql_ref) + dot(xl, wqh_ref))
    key_ids = lax.broadcasted_iota(jnp.int32, (N_KEYS, LANES), 0).astype(jnp.float32)
    sub = lax.broadcasted_iota(jnp.int32, (SUBLANES, LANES), 0)
    cand_ids = jnp.concatenate([sub, sub + SUBLANES] + [sub + a * PEER_TOPK for a in range(1, SUBLANES)]
                               + [(sub + SUBLANES) * PEER_TOPK], axis=0).astype(jnp.float32)
    ncol = tm // LANES

    def head(h, carry):
        col0 = pl.multiple_of(h * 2 * PEER_HALF, 2 * PEER_HALF)
        row0 = pl.multiple_of(h * PEER_TOPK, PEER_TOPK)
        for c in range(ncol):
            qh = q_ref[c * LANES:(c + 1) * LANES, pl.ds(col0, 2 * PEER_HALF)]
            sc = lax.dot_general(sk_ref[h], qh, (((1,), (1,)), ((), ())),
                                 preferred_element_type=jnp.float32, precision=HI)
            _top16([(sc[:N_KEYS], key_ids, float(N_KEYS), s1_ref.at[c], i1_ref.at[c]),
                    (sc[N_KEYS:], key_ids, float(N_KEYS), s2_ref.at[c], i2_ref.at[c])])
        problems = []
        for c in range(ncol):
            s1, s2 = s1_ref[c], s2_ref[c]
            cand = jnp.concatenate([s1[0:1] + s2[0:SUBLANES], s1[0:1] + s2[SUBLANES:]]
                                   + [s1[a:a + 1] + s2[0:SUBLANES] for a in range(1, SUBLANES)]
                                   + [s1[SUBLANES:] + s2[0:1]], axis=0)
            problems.append((cand, cand_ids, float(PEER_TOPK * PEER_TOPK), top_ref.at[c], pos_ref.at[c]))
        _top16(problems)
        for c in range(ncol):
            top, pos = top_ref[c], pos_ref[c].astype(jnp.int32)
            i1, i2 = i1_ref[c].astype(jnp.int32), i2_ref[c].astype(jnp.int32)
            e = _select_rows(pos // PEER_TOPK, i1) * N_KEYS + _select_rows(pos % PEER_TOPK, i2)
            p = jnp.exp(top - top[0:1, :])
            idx_t_ref[pl.ds(row0, PEER_TOPK), c * LANES:(c + 1) * LANES] = e * ROW_SUBLANES
            gate_t_ref[pl.ds(row0, PEER_TOPK), c * LANES:(c + 1) * LANES] = p / jnp.sum(p, axis=0, keepdims=True)
        return carry

    lax.fori_loop(0, PEER_HEADS, head, 0)
    for c in range(ncol):
        idx_ref[c * LANES:(c + 1) * LANES, :] = idx_t_ref[:, c * LANES:(c + 1) * LANES].T
        gate_ref[c * LANES:(c + 1) * LANES, :] = gate_t_ref[:, c * LANES:(c + 1) * LANES].T


def _block_diag_subkeys(subkeys):
    z = jnp.zeros_like(subkeys[:, 0])
    top = jnp.concatenate([subkeys[:, 0], z], axis=-1)
    bot = jnp.concatenate([z, subkeys[:, 1]], axis=-1)
    return jnp.concatenate([top, bot], axis=1)


def peer_route(h, norm_w, wq, subkeys, *, tm=256):
    t, d = h.shape
    sk = _block_diag_subkeys(subkeys)
    return pl.pallas_call(
        functools.partial(_peer_route_body, tm=tm),
        grid=(t // tm,),
        in_specs=[
            pl.BlockSpec((tm, d), lambda i: (i, 0)),
            pl.BlockSpec((1, d), lambda i: (0, 0)),
            pl.BlockSpec(wq.shape, lambda i: (0, 0)),
            pl.BlockSpec(wq.shape, lambda i: (0, 0)),
            pl.BlockSpec(sk.shape, lambda i: (0, 0, 0)),
        ],
        out_specs=[pl.BlockSpec((tm, d), lambda i: (i, 0)),
                   pl.BlockSpec((tm, PEER_SEL), lambda i: (i, 0)),
                   pl.BlockSpec((tm, PEER_SEL), lambda i: (i, 0))],
        out_shape=[jax.ShapeDtypeStruct((t, d), jnp.float32),
                   jax.ShapeDtypeStruct((t, PEER_SEL), jnp.int32),
                   jax.ShapeDtypeStruct((t, PEER_SEL), jnp.float32)],
        scratch_shapes=[pltpu.VMEM((tm, d), jnp.float32), pltpu.VMEM((PEER_SEL, tm), jnp.int32),
                        pltpu.VMEM((PEER_SEL, tm), jnp.float32)]
                       + [pltpu.VMEM((tm // LANES, PEER_TOPK, LANES), jnp.float32),
                          pltpu.VMEM((tm // LANES, PEER_TOPK, LANES), jnp.float32)] * 3,
        compiler_params=pltpu.CompilerParams(dimension_semantics=("arbitrary",), vmem_limit_bytes=VMEM_LIMIT_BYTES),
        name="peer_route",
    )(h, norm_w.reshape(1, d), *_split_bf16(wq), sk)


def pack_table(tab):
    n, d = tab.shape
    b = lax.bitcast_convert_type(tab.astype(jnp.bfloat16), jnp.uint16).astype(jnp.uint32)
    w = b[:, : d // 2] | (b[:, d // 2:] << 16)
    return lax.bitcast_convert_type(w, jnp.int32).reshape(n * ROW_SUBLANES, LANES)


def _unpack(w):
    lo = lax.bitcast_convert_type(w << 16, jnp.float32)
    hi = lax.bitcast_convert_type(w & jnp.int32(-65536), jnp.float32)
    return lo, hi


STAGE_STRIDE = 136
STAGE_ROWS = ROW_SUBLANES * STAGE_STRIDE
STAGE_TOKENS = 8


def _load_table(tab_hbm, tab_vmem, sem, st_first_read):
    @pl.when(pl.program_id(0) == 0)
    def _():
        cp = pltpu.make_async_copy(tab_hbm, tab_vmem, sem)
        cp.start()
        st_first_read[...] = jnp.zeros_like(st_first_read)
        cp.wait()


def _split_bf16(x):
    hi = x.astype(jnp.bfloat16)
    return hi, (x - hi.astype(jnp.float32)).astype(jnp.bfloat16)


def _stage_rows(i, k):
    return pl.ds(i * STAGE_ROWS + k, ROW_SUBLANES, stride=STAGE_STRIDE)


def _gather_rows(idx_ref, t, i, tab_vmem, st_ref):
    for k in range(PEER_SEL):
        e = idx_ref[t, k]
        st_ref[_stage_rows(i, k), :] = tab_vmem[pl.ds(pl.multiple_of(e, ROW_SUBLANES), ROW_SUBLANES), :]


def _staged_matrix(st_ref, i):
    r0 = i * STAGE_ROWS
    w = jnp.concatenate([st_ref[r0 + s * STAGE_STRIDE:r0 + s * STAGE_STRIDE + PEER_SEL, :] for s in range(ROW_SUBLANES)],
                        axis=1)
    lo, hi = _unpack(w)
    return jnp.concatenate([lo, hi], axis=1).astype(jnp.bfloat16)


def _mxu_lhs(row):
    hi, lo = _split_bf16(row)
    return jnp.concatenate([hi, lo, jnp.zeros((SUBLANES - 2, row.shape[1]), jnp.bfloat16)], axis=0)


def _staged_pipeline(idx_ref, tab_vmem, st_a, st_b, tb, consume, out_ref):
    n = STAGE_TOKENS

    def half(t_gather, st_fill, t_mxu, st_read):
        rows = []
        for i in range(n):
            _gather_rows(idx_ref, t_gather + i, i, tab_vmem, st_fill)
            rows.append(consume(st_read, i, t_mxu + i))
        out_ref[pl.ds(t_mxu, n), :] = jnp.concatenate(rows, axis=0)

    def two_groups(j, carry):
        t0 = pl.multiple_of(j * 2 * n, 2 * n)
        tp = pl.multiple_of(jnp.maximum(t0 - n, 0), n)
        half(t0, st_a, tp, st_b)
        half(t0 + n, st_b, t0, st_a)
        return carry

    lax.fori_loop(0, tb // (2 * n), two_groups, 0)
    out_ref[tb - n:tb, :] = jnp.concatenate([consume(st_b, i, tb - n + i) for i in range(n)], axis=0)


def _peer_u_body(idx_ref, x_ref, gate_ref, tab_hbm, act_ref, tab_vmem, st_a, st_b, sem, *, tb):
    _load_table(tab_hbm, tab_vmem, sem, st_b)

    def dot_products(st_ref, i, t):
        r = lax.dot_general(_mxu_lhs(x_ref[pl.ds(t, 1), :]), _staged_matrix(st_ref, i), (((1,), (1,)), ((), ())),
                            preferred_element_type=jnp.float32)
        return r[0:1] + r[1:2]

    _staged_pipeline(idx_ref, tab_vmem, st_a, st_b, tb, dot_products, act_ref)
    hp = act_ref[...]
    act_ref[...] = 0.5 * hp * (1.0 + lax.erf(hp * (2.0 ** -0.5))) * gate_ref[...]


def peer_u(idx, x, gate, tab, *, tb=128):
    t, d = x.shape
    stage = pltpu.VMEM((STAGE_TOKENS * STAGE_ROWS, LANES), jnp.int32)
    return pl.pallas_call(
        functools.partial(_peer_u_body, tb=tb),
        grid=(t // tb,),
        in_specs=[
            pl.BlockSpec((tb, PEER_SEL), lambda i: (i, 0), memory_space=pltpu.SMEM),
            pl.BlockSpec((tb, d), lambda i: (i, 0)),
            pl.BlockSpec((tb, PEER_SEL), lambda i: (i, 0)),
            pl.BlockSpec(memory_space=pl.ANY),
        ],
        out_specs=pl.BlockSpec((tb, PEER_SEL), lambda i: (i, 0)),
        out_shape=jax.ShapeDtypeStruct((t, PEER_SEL), jnp.float32),
        scratch_shapes=[pltpu.VMEM(tab.shape, jnp.int32), stage, stage, pltpu.SemaphoreType.DMA],
        compiler_params=pltpu.CompilerParams(dimension_semantics=("arbitrary",), vmem_limit_bytes=VMEM_LIMIT_BYTES),
        name="peer_u",
    )(idx, x, gate, tab)


def _peer_v_body(idx_ref, act_ref, h_ref, nw_ref, tab_hbm, o_ref, tab_vmem, st_a, st_b, sem, *, tb, final_norm):
    _load_table(tab_hbm, tab_vmem, sem, st_b)

    def weighted_sum(st_ref, i, t):
        r = jnp.dot(_mxu_lhs(act_ref[pl.ds(t, 1), :]), _staged_matrix(st_ref, i), preferred_element_type=jnp.float32)
        return r[0:1] + r[1:2]

    _staged_pipeline(idx_ref, tab_vmem, st_a, st_b, tb, weighted_sum, o_ref)
    y = h_ref[...] + o_ref[...]
    if final_norm:
        y = y * lax.rsqrt(jnp.mean(y * y, axis=-1, keepdims=True) + NORM_EPS) * nw_ref[...]
    o_ref[...] = y


def peer_v(idx, act, tab, h, norm_w, *, final_norm, tb=128):
    t, d = h.shape
    stage = pltpu.VMEM((STAGE_TOKENS * STAGE_ROWS, LANES), jnp.int32)
    return pl.pallas_call(
        functools.partial(_peer_v_body, tb=tb, final_norm=final_norm),
        grid=(t // tb,),
        in_specs=[
            pl.BlockSpec((tb, PEER_SEL), lambda i: (i, 0), memory_space=pltpu.SMEM),
            pl.BlockSpec((tb, PEER_SEL), lambda i: (i, 0)),
            pl.BlockSpec((tb, d), lambda i: (i, 0)),
            pl.BlockSpec((1, d), lambda i: (0, 0)),
            pl.BlockSpec(memory_space=pl.ANY),
        ],
        out_specs=pl.BlockSpec((tb, d), lambda i: (i, 0)),
        out_shape=jax.ShapeDtypeStruct((t, d), jnp.float32),
        scratch_shapes=[pltpu.VMEM(tab.shape, jnp.int32), stage, stage, pltpu.SemaphoreType.DMA],
        compiler_params=pltpu.CompilerParams(dimension_semantics=("arbitrary",), vmem_limit_bytes=VMEM_LIMIT_BYTES),
        name="peer_v",
    )(idx, act, h, norm_w.reshape(1, d), tab)


def kernel(x, norm_mix_w, w_in, shift_mu, attn_sinks, decay_w0, decay_w2, iclr_a0, iclr_a2,
           gate_g2, k_k, k_a, r_k, ln_x_w, ln_x_b, proj_attn, proj_rwkv, w_out, norm_ffn_w,
           peer_wq, peer_subkeys, peer_u_tab, peer_v_tab, norm_final_w):
    b, s, d = x.shape
    t = b * s
    h = x.reshape(t, d)
    for l in range(DEPTH):
        w_lay, mu_lay = layout_in_proj(w_in[l], shift_mu[l])
        qkv, zr, gates = in_proj(h, norm_mix_w[l], w_lay)
        y_attn = swa_attention(qkv, attn_sinks[l], b, s)
        r, dec, k, v, kk, nb, g, bonus = rwkv_prep(zr, mu_lay, decay_w0[l], decay_w2[l], iclr_a0[l], iclr_a2[l],
                                                   gate_g2[l], k_k[l], k_a[l], r_k[l].reshape(-1), b, s)
        to_chains = lambda a: jnp.swapaxes(a.reshape(s, b * RWKV_HEADS, RWKV_HEAD), 1, 2)
        y = rwkv_scan(to_chains(r), to_chains(dec), to_chains(k), to_chains(v), to_chains(kk), to_chains(nb))
        y = jnp.swapaxes(y, 1, 2).reshape(s, b * RWKV_DIM)
        h = merge_out(h, y, g, bonus, y_attn, gates, ln_x_w[l], ln_x_b[l], proj_attn[l], proj_rwkv[l], w_out[l])
        xn2, idx, gate = peer_route(h, norm_ffn_w[l], peer_wq[l], peer_subkeys[l])
        act = peer_u(idx, xn2, gate, pack_table(peer_u_tab[l]))
        h = peer_v(idx, act, pack_table(peer_v_tab[l]), h, norm_final_w, final_norm=(l + 1 == DEPTH))
    return h.reshape(b, s, d)
```

```python
import functools

import jax, jax.numpy as jnp
from jax import lax
from jax.experimental import pallas as pl
from jax.experimental.pallas import tpu as pltpu

D_MODEL = 1024
DEPTH = 1

HEAD_DIM = 64
ATTN_HEADS = 8
ATTN_KV_HEADS = 2
ATTN_GROUP = ATTN_HEADS // ATTN_KV_HEADS
ATTN_DIM = ATTN_HEADS * HEAD_DIM
KV_DIM = ATTN_KV_HEADS * HEAD_DIM
WINDOW = 128
BLOCK = WINDOW
ROPE_THETA = 10000.0

RWKV_HEADS = 8
RWKV_HEAD = 64
RWKV_DIM = RWKV_HEADS * RWKV_HEAD
DECAY_LORA = 64
ICLR_LORA = 64
GATE_LORA = 160
RWKV_GN_EPS = 64e-5

PEER_HEADS = 8
N_KEYS = 128
PEER_HALF = 64
PEER_TOPK = 16
PEER_SEL = PEER_HEADS * PEER_TOPK

NORM_EPS = 1e-5
NEG_INF = -1e30

LANES = 128
SUBLANES = 8
VMEM_LIMIT_BYTES = 56 * 1024 * 1024

ROW_WORDS = D_MODEL // 2
ROW_SUBLANES = ROW_WORDS // LANES
NEG_FILL = -3.0e38
HI = lax.Precision.HIGHEST
RWKV_VGROUPS = RWKV_HEAD // SUBLANES

QKV_COLS = ATTN_DIM + 2 * KV_DIM
LORA_PAD = (LANES, LANES, 2 * LANES)
RWKV_COLS = 3 * RWKV_DIM + sum(LORA_PAD)
GATE_COLS = 2 * D_MODEL


def _pad_cols(w, width):
    return jnp.pad(w, ((0, 0), (0, width - w.shape[1])))


def _pad_rows(w, height):
    return jnp.pad(w, ((0, height - w.shape[0]), (0, 0)))


def layout_in_proj(w_in, shift_mu):
    o1 = QKV_COLS
    o_r = o1 + 3 * RWKV_DIM
    o_w = o_r + DECAY_LORA
    o_a = o_w + ICLR_LORA
    o_g = o_a + GATE_LORA
    parts = [w_in[:, :o_r], _pad_cols(w_in[:, o_r:o_w], LORA_PAD[0]), _pad_cols(w_in[:, o_w:o_a], LORA_PAD[1]),
             _pad_cols(w_in[:, o_a:o_g], LORA_PAD[2]), w_in[:, o_g:]]
    mu = shift_mu.reshape(1, -1)
    m_r = 3 * RWKV_DIM
    mu_parts = [mu[:, :m_r], _pad_cols(mu[:, m_r:m_r + DECAY_LORA], LORA_PAD[0]),
                _pad_cols(mu[:, m_r + DECAY_LORA:m_r + DECAY_LORA + ICLR_LORA], LORA_PAD[1]),
                _pad_cols(mu[:, m_r + DECAY_LORA + ICLR_LORA:], LORA_PAD[2])]
    return jnp.concatenate(parts, axis=1), jnp.concatenate(mu_parts, axis=1)


def _in_proj_body(x_ref, nw_ref, w_ref, qkv_ref, zr_ref, gate_ref):
    xf = x_ref[...]
    xn = xf * lax.rsqrt(jnp.mean(xf * xf, axis=-1, keepdims=True) + NORM_EPS) * nw_ref[...]
    z = jnp.dot(xn.astype(jnp.bfloat16), w_ref[...], preferred_element_type=jnp.float32)
    qkv_ref[...] = z[:, :QKV_COLS]
    zr_ref[...] = z[:, QKV_COLS:QKV_COLS + RWKV_COLS]
    gate_ref[...] = jax.nn.sigmoid(z[:, QKV_COLS + RWKV_COLS:]).astype(jnp.bfloat16)


def in_proj(x, norm_w, w_lay, *, tm=256):
    t, d = x.shape
    n = w_lay.shape[1]
    return pl.pallas_call(
        _in_proj_body,
        grid=(t // tm,),
        in_specs=[pl.BlockSpec((tm, d), lambda i: (i, 0)), pl.BlockSpec((1, d), lambda i: (0, 0)),
                  pl.BlockSpec((d, n), lambda i: (0, 0))],
        out_specs=[pl.BlockSpec((tm, QKV_COLS), lambda i: (i, 0)), pl.BlockSpec((tm, RWKV_COLS), lambda i: (i, 0)),
                   pl.BlockSpec((tm, GATE_COLS), lambda i: (i, 0))],
        out_shape=[jax.ShapeDtypeStruct((t, QKV_COLS), jnp.float32), jax.ShapeDtypeStruct((t, RWKV_COLS), jnp.float32),
                   jax.ShapeDtypeStruct((t, GATE_COLS), jnp.bfloat16)],
        compiler_params=pltpu.CompilerParams(dimension_semantics=("arbitrary",), vmem_limit_bytes=VMEM_LIMIT_BYTES),
        name="in_proj",
    )(x, norm_w.reshape(1, d), w_lay.astype(jnp.bfloat16))


def rope_tables(s):
    inv = ROPE_THETA ** (-jnp.arange(0, HEAD_DIM, 2, dtype=jnp.float32) / HEAD_DIM)
    ang = jnp.arange(s, dtype=jnp.float32)[:, None] * inv[None, :]
    cos = jnp.concatenate([jnp.cos(ang), jnp.cos(ang)], axis=-1)
    sin = jnp.concatenate([-jnp.sin(ang), jnp.sin(ang)], axis=-1)
    reps = LANES // HEAD_DIM
    return jnp.tile(cos, (1, reps)), jnp.tile(sin, (1, reps))


def _rope(x, cos, sin, first_half):
    rot = jnp.where(first_half, pltpu.roll(x, LANES - HEAD_DIM // 2, 1), pltpu.roll(x, HEAD_DIM // 2, 1))
    return x * cos + rot * sin


def _attn_body(cur_ref, prev_ref, cos_ref, sin_ref, cosp_ref, sinp_ref, sink_ref, o_ref):
    n = pl.program_id(1)
    lane = lax.broadcasted_iota(jnp.int32, (BLOCK, LANES), 1)
    first_half = (lane % HEAD_DIM) < HEAD_DIM // 2
    low = lane < HEAD_DIM
    cos, sin = cos_ref[...], sin_ref[...]
    cur = cur_ref[0]
    prev = prev_ref[0]
    k_cur = _rope(cur[:, ATTN_DIM:ATTN_DIM + KV_DIM], cos, sin, first_half)
    k_prev = _rope(prev[:, ATTN_DIM:ATTN_DIM + KV_DIM], cosp_ref[...], sinp_ref[...], first_half)
    keys = jnp.concatenate([k_prev, k_cur], axis=0)
    vals = jnp.concatenate([prev[:, ATTN_DIM + KV_DIM:], cur[:, ATTN_DIM + KV_DIM:]], axis=0)
    keys_sw = pltpu.roll(keys, HEAD_DIM, 1)
    vals_sw = pltpu.roll(vals, HEAD_DIM, 1)
    low2 = lax.broadcasted_iota(jnp.int32, (2 * BLOCK, LANES), 1) < HEAD_DIM
    qi = lax.broadcasted_iota(jnp.int32, (BLOCK, 2 * BLOCK), 0) + BLOCK
    kj = lax.broadcasted_iota(jnp.int32, (BLOCK, 2 * BLOCK), 1)
    diff = qi - kj
    mask = (diff >= 0) & (diff < WINDOW) & ((kj >= BLOCK) | (n > 0))
    scale = HEAD_DIM ** -0.5
    outs = []
    for j in range(ATTN_HEADS // 2):
        g = (2 * j) // ATTN_GROUP
        kg = (jnp.where(low2, keys, keys_sw) if g == 0 else jnp.where(low2, keys_sw, keys)).astype(jnp.bfloat16)
        vg = (jnp.where(low2, vals, vals_sw) if g == 0 else jnp.where(low2, vals_sw, vals)).astype(jnp.bfloat16)
        qg = _rope(cur[:, j * LANES:(j + 1) * LANES], cos, sin, first_half) * scale
        halves = []
        for half in range(2):
            qh = jnp.where(low if half == 0 else ~low, qg, 0.0).astype(jnp.bfloat16)
            logits = lax.dot_general(qh, kg, (((1,), (1,)), ((), ())), preferred_element_type=jnp.float32)
            logits = jnp.where(mask, logits, NEG_INF)
            sink = sink_ref[2 * j + half]
            m = jnp.maximum(jnp.max(logits, axis=-1, keepdims=True), sink)
            p = jnp.exp(logits - m)
            denom = jnp.sum(p, axis=-1, keepdims=True) + jnp.exp(sink - m)
            o = jnp.dot(p.astype(jnp.bfloat16), vg, preferred_element_type=jnp.float32)
            halves.append(o / denom)
        outs.append(jnp.where(low, halves[0], halves[1]))
    o_ref[0] = jnp.concatenate(outs, axis=1).astype(o_ref.dtype)


def swa_attention(qkv, sinks, b, s):
    nb = s // BLOCK
    cos, sin = rope_tables(s)
    qkv3 = qkv.reshape(b, s, QKV_COLS)
    cur_spec = pl.BlockSpec((1, BLOCK, QKV_COLS), lambda bi, n: (bi, n, 0))
    prev_spec = pl.BlockSpec((1, BLOCK, QKV_COLS), lambda bi, n: (bi, jnp.maximum(n - 1, 0), 0))
    tab = pl.BlockSpec((BLOCK, LANES), lambda bi, n: (n, 0))
    tabp = pl.BlockSpec((BLOCK, LANES), lambda bi, n: (jnp.maximum(n - 1, 0), 0))
    out = pl.pallas_call(
        _attn_body,
        grid=(b, nb),
        in_specs=[cur_spec, prev_spec, tab, tab, tabp, tabp, pl.BlockSpec(memory_space=pltpu.SMEM)],
        out_specs=pl.BlockSpec((1, BLOCK, ATTN_DIM), lambda bi, n: (bi, n, 0)),
        out_shape=jax.ShapeDtypeStruct((b, s, ATTN_DIM), jnp.bfloat16),
        compiler_params=pltpu.CompilerParams(dimension_semantics=("arbitrary", "arbitrary"),
                                             vmem_limit_bytes=VMEM_LIMIT_BYTES),
        name="swa_attention",
    )(qkv3, qkv3, cos, sin, cos, sin, sinks.astype(jnp.float32))
    return out.reshape(b * s, ATTN_DIM)


def head_block_ones():
    i = jnp.arange(RWKV_DIM) // RWKV_HEAD
    return (i[:, None] == i[None, :]).astype(jnp.bfloat16)


def _softplus(x):
    return jnp.maximum(x, 0.0) + jnp.log1p(jnp.exp(-jnp.abs(x)))


def _split_bf16(x):
    hi = x.astype(jnp.bfloat16)
    return hi, (x - hi.astype(jnp.float32)).astype(jnp.bfloat16)


def _dot3(x, wh_ref, wl_ref):
    xh, xl = _split_bf16(x)
    dot = lambda a, w_ref: jnp.dot(a, w_ref[...], preferred_element_type=jnp.float32)
    return dot(xh, wh_ref) + (dot(xh, wl_ref) + dot(xl, wh_ref))


def _head_sums(x, ones_ref):
    x1, rest = _split_bf16(x)
    x2 = rest
    x3 = (x - x1.astype(jnp.float32) - x2.astype(jnp.float32)).astype(jnp.bfloat16)
    dot = lambda a: jnp.dot(a, ones_ref[...], preferred_element_type=jnp.float32)
    return dot(x1) + (dot(x2) + dot(x3))


def _rwkv_prep_body(z_ref, zp_ref, mu_ref, w0_ref, w2h_ref, w2l_ref, a0_ref, a2h_ref, a2l_ref, g2h_ref, g2l_ref,
                    kk_w_ref, ka_ref, rk_ref, ones_ref,
                    r_ref, dec_ref, k_ref, v_ref, kk_ref, nb_ref, g_ref, bonus_ref, *, ts):
    z = z_ref[0]
    row = lax.broadcasted_iota(jnp.int32, (ts, 1), 0)
    last_prev = jnp.where(pl.program_id(1) > 0, zp_ref[0][SUBLANES - 1:SUBLANES, :], 0.0)
    z_prev = jnp.where(row == 0, last_prev, pltpu.roll(z, 1, 0))
    zs = z + mu_ref[...] * (z_prev - z)
    c = RWKV_DIM
    r, k, v = zs[:, :c], zs[:, c:2 * c], zs[:, 2 * c:3 * c]
    o = 3 * c
    w_lo = zs[:, o:o + LORA_PAD[0]]
    a_lo = zs[:, o + LORA_PAD[0]:o + LORA_PAD[0] + LORA_PAD[1]]
    g_lo = zs[:, o + LORA_PAD[0] + LORA_PAD[1]:]
    w = -_softplus(-(w0_ref[...] + _dot3(jnp.tanh(w_lo), w2h_ref, w2l_ref))) - 0.5
    dec_ref[...] = jnp.exp(-jnp.exp(w))
    a = jax.nn.sigmoid(a0_ref[...] + _dot3(a_lo, a2h_ref, a2l_ref))
    g_ref[...] = _dot3(jax.nn.sigmoid(g_lo), g2h_ref, g2l_ref)
    kk = k * kk_w_ref[...]
    kk = kk / jnp.maximum(jnp.sqrt(_head_sums(kk * kk, ones_ref)), 1e-12)
    kmod = k * (1.0 + (a - 1.0) * ka_ref[...])
    r_ref[...] = r
    k_ref[...] = kmod
    v_ref[...] = v
    kk_ref[...] = kk
    nb_ref[...] = -kk * a
    bonus_ref[...] = _head_sums(r * kmod * rk_ref[...], ones_ref) * v


def rwkv_prep(zr, mu_lay, w0, w2, a0, a2, g2, k_k, k_a, r_k, b, s, *, ts=256):
    t = b * s
    zr3 = zr.reshape(b, s, RWKV_COLS)
    row = lambda x: x.reshape(1, -1)
    w2p, a2p, g2p = _pad_rows(w2, LORA_PAD[0]), _pad_rows(a2, LORA_PAD[1]), _pad_rows(g2, LORA_PAD[2])
    const = lambda shape: pl.BlockSpec(shape, lambda bi, si: (0,) * len(shape))
    nblk = s // ts
    tok_spec = pl.BlockSpec((ts, RWKV_DIM), lambda bi, si: (bi * nblk + si, 0))
    time_spec = pl.BlockSpec((ts, RWKV_DIM), lambda bi, si: (si, bi))
    tok_shape = jax.ShapeDtypeStruct((t, RWKV_DIM), jnp.float32)
    time_shape = jax.ShapeDtypeStruct((s, b * RWKV_DIM), jnp.float32)
    return pl.pallas_call(
        functools.partial(_rwkv_prep_body, ts=ts),
        grid=(b, nblk),
        in_specs=[pl.BlockSpec((1, ts, RWKV_COLS), lambda bi, si: (bi, si, 0)),
                  pl.BlockSpec((1, SUBLANES, RWKV_COLS), lambda bi, si: (bi, jnp.maximum(si * (ts // SUBLANES) - 1, 0), 0)),
                  const((1, RWKV_COLS)), const((1, RWKV_DIM)), const(w2p.shape), const(w2p.shape), const((1, RWKV_DIM)),
                  const(a2p.shape), const(a2p.shape), const(g2p.shape), const(g2p.shape),
                  const((1, RWKV_DIM)), const((1, RWKV_DIM)), const((1, RWKV_DIM)), const((RWKV_DIM, RWKV_DIM))],
        out_specs=[time_spec] * 6 + [tok_spec] * 2,
        out_shape=[time_shape] * 6 + [tok_shape] * 2,
        compiler_params=pltpu.CompilerParams(dimension_semantics=("arbitrary", "arbitrary"),
                                             vmem_limit_bytes=VMEM_LIMIT_BYTES),
        name="rwkv_prep",
    )(zr3, zr3, mu_lay, row(w0), *_split_bf16(w2p), row(a0), *_split_bf16(a2p), *_split_bf16(g2p),
      row(k_k), row(k_a), row(r_k), head_block_ones())


SCAN_PARTIALS = 4


def _tree_sum(parts):
    while len(parts) > 1:
        parts = [parts[i] + parts[i + 1] for i in range(0, len(parts), 2)]
    return parts[0]


def _scan_body(r_ref, w_ref, k_ref, v_ref, kk_ref, nb_ref, y_ref, *s_refs, tc):
    @pl.when(pl.program_id(1) == 0)
    def _():
        for s_ref in s_refs:
            s_ref[...] = jnp.zeros_like(s_ref)

    def step(t, carry):
        for g, s_ref in enumerate(s_refs):
            rows = slice(g * SUBLANES, (g + 1) * SUBLANES)
            acc = [None] * SCAN_PARTIALS
            for k in range(RWKV_HEAD):
                p = s_ref[k] * kk_ref[t, pl.ds(k, 1), :]
                acc[k % SCAN_PARTIALS] = p if acc[k % SCAN_PARTIALS] is None else acc[k % SCAN_PARTIALS] + p
            sa = _tree_sum(acc)
            vv = v_ref[t, rows, :]
            yacc = [None] * SCAN_PARTIALS
            for k in range(RWKV_HEAD):
                s_new = s_ref[k] * w_ref[t, pl.ds(k, 1), :] + sa * nb_ref[t, pl.ds(k, 1), :] + vv * k_ref[t, pl.ds(k, 1), :]
                s_ref[k] = s_new
                p = s_new * r_ref[t, pl.ds(k, 1), :]
                yacc[k % SCAN_PARTIALS] = p if yacc[k % SCAN_PARTIALS] is None else yacc[k % SCAN_PARTIALS] + p
            y_ref[t, rows, :] = _tree_sum(yacc)
        return carry

    lax.fori_loop(0, tc, step, 0)


def rwkv_scan(r, w, k, v, kk, nb, *, tc=32):
    s, n, c = r.shape
    spec = pl.BlockSpec((tc, n, LANES), lambda ci, ti: (ti, 0, ci))
    return pl.pallas_call(
        functools.partial(_scan_body, tc=tc),
        grid=(c // LANES, s // tc),
        in_specs=[spec] * 6,
        out_specs=spec,
        out_shape=jax.ShapeDtypeStruct((s, n, c), jnp.float32),
        scratch_shapes=[pltpu.VMEM((n, SUBLANES, LANES), jnp.float32)] * RWKV_VGROUPS,
        compiler_params=pltpu.CompilerParams(dimension_semantics=("arbitrary", "arbitrary"),
                                             vmem_limit_bytes=VMEM_LIMIT_BYTES),
        name="rwkv_scan",
    )(r, w, k, v, kk, nb)


def _merge_body(x_ref, y_ref, g_ref, bonus_ref, ya_ref, gate_ref, lnw_ref, lnb_ref, ones_ref, pa_ref, pb_ref, wo_ref, h_ref):
    dot_bf = lambda x, w: jnp.dot(x.astype(jnp.bfloat16), w, preferred_element_type=jnp.float32)
    y = y_ref[...]
    inv_n = 1.0 / RWKV_HEAD
    mean = _head_sums(y, ones_ref) * inv_n
    yc = y - mean
    var = _head_sums(yc * yc, ones_ref) * inv_n
    yn = yc * lax.rsqrt(var + RWKV_GN_EPS) * lnw_ref[...] + lnb_ref[...] + bonus_ref[...]
    y_rwkv = yn * g_ref[...]
    gates = gate_ref[...].astype(jnp.float32)
    merged = gates[:, :D_MODEL] * dot_bf(ya_ref[...], pa_ref[...]) + gates[:, D_MODEL:] * dot_bf(y_rwkv, pb_ref[...])
    h_ref[...] = x_ref[...] + dot_bf(merged, wo_ref[...])


def merge_out(x, y, g, bonus, y_attn, gates, ln_w, ln_b, proj_attn, proj_rwkv, w_out, *, tm=256):
    t, d = x.shape
    s = y.shape[0]
    nblk = s // tm
    row = lambda v: v.reshape(1, -1)
    tok = lambda w: pl.BlockSpec((tm, w), lambda bi, si: (bi * nblk + si, 0))
    const = lambda shape: pl.BlockSpec(shape, lambda bi, si: (0,) * len(shape))
    bf = lambda w: w.astype(jnp.bfloat16)
    return pl.pallas_call(
        _merge_body,
        grid=(t // s, nblk),
        in_specs=[tok(d), pl.BlockSpec((tm, RWKV_DIM), lambda bi, si: (si, bi)), tok(RWKV_DIM), tok(RWKV_DIM),
                  tok(ATTN_DIM), tok(GATE_COLS),
                  const((1, RWKV_DIM)), const((1, RWKV_DIM)), const((RWKV_DIM, RWKV_DIM)),
                  const(proj_attn.shape), const(proj_rwkv.shape), const(w_out.shape)],
        out_specs=tok(d),
        out_shape=jax.ShapeDtypeStruct((t, d), jnp.float32),
        compiler_params=pltpu.CompilerParams(dimension_semantics=("arbitrary", "arbitrary"),
                                             vmem_limit_bytes=VMEM_LIMIT_BYTES),
        name="merge_out",
    )(x, y, g, bonus, y_attn, gates, row(ln_w), row(ln_b), head_block_ones(), bf(proj_attn), bf(proj_rwkv), bf(w_out))


def _top16(problems):
    arrays = [p[0] for p in problems]
    for i in range(PEER_TOPK):
        for j, (_, ids, big, vals_ref, ids_ref) in enumerate(problems):
            a = arrays[j]
            m = jnp.max(a, axis=0, keepdims=True)
            r = jnp.min(jnp.where(a == m, ids, big), axis=0, keepdims=True)
            vals_ref[i:i + 1, :] = m
            ids_ref[i:i + 1, :] = r
            arrays[j] = jnp.where(ids == r, NEG_FILL, a)


def _select_rows(pos, table):
    out = jnp.zeros(pos.shape, table.dtype)
    for a in range(PEER_TOPK):
        out = jnp.where(pos == a, table[a:a + 1, :], out)
    return out


def _peer_route_body(h_ref, nw_ref, wqh_ref, wql_ref, sk_ref, xn_ref, idx_ref, gate_ref, q_ref, idx_t_ref, gate_t_ref,
                     s1_ref, i1_ref, s2_ref, i2_ref, top_ref, pos_ref, *, tm):
    xf = h_ref[...]
    xn = xf * lax.rsqrt(jnp.mean(xf * xf, axis=-1, keepdims=True) + NORM_EPS) * nw_ref[...]
    xn_ref[...] = xn
    q_ref[...] = _dot3(xn, wqh_ref, wql_ref)
    key_ids = lax.broadcasted_iota(jnp.int32, (N_KEYS, LANES), 0).astype(jnp.float32)
    sub = lax.broadcasted_iota(jnp.int32, (SUBLANES, LANES), 0)
    cand_ids = jnp.concatenate([sub, sub + SUBLANES] + [sub + a * PEER_TOPK for a in range(1, SUBLANES)]
                               + [(sub + SUBLANES) * PEER_TOPK], axis=0).astype(jnp.float32)
    ncol = tm // LANES

    def head(h, carry):
        col0 = pl.multiple_of(h * 2 * PEER_HALF, 2 * PEER_HALF)
        row0 = pl.multiple_of(h * PEER_TOPK, PEER_TOPK)
        for c in range(ncol):
            qh = q_ref[c * LANES:(c + 1) * LANES, pl.ds(col0, 2 * PEER_HALF)]
            sc = lax.dot_general(sk_ref[h], qh, (((1,), (1,)), ((), ())),
                                 preferred_element_type=jnp.float32, precision=HI)
            _top16([(sc[:N_KEYS], key_ids, float(N_KEYS), s1_ref.at[c], i1_ref.at[c]),
                    (sc[N_KEYS:], key_ids, float(N_KEYS), s2_ref.at[c], i2_ref.at[c])])
        problems = []
        for c in range(ncol):
            s1, s2 = s1_ref[c], s2_ref[c]
            cand = jnp.concatenate([s1[0:1] + s2[0:SUBLANES], s1[0:1] + s2[SUBLANES:]]
                                   + [s1[a:a + 1] + s2[0:SUBLANES] for a in range(1, SUBLANES)]
                                   + [s1[SUBLANES:] + s2[0:1]], axis=0)
            problems.append((cand, cand_ids, float(PEER_TOPK * PEER_TOPK), top_ref.at[c], pos_ref.at[c]))
        _top16(problems)
        for c in range(ncol):
            top, pos = top_ref[c], pos_ref[c].astype(jnp.int32)
            i1, i2 = i1_ref[c].astype(jnp.int32), i2_ref[c].astype(jnp.int32)
            e = _select_rows(pos // PEER_TOPK, i1) * N_KEYS + _select_rows(pos % PEER_TOPK, i2)
            p = jnp.exp(top - top[0:1, :])
            idx_t_ref[pl.ds(row0, PEER_TOPK), c * LANES:(c + 1) * LANES] = e * ROW_SUBLANES
            gate_t_ref[pl.ds(row0, PEER_TOPK), c * LANES:(c + 1) * LANES] = p / jnp.sum(p, axis=0, keepdims=True)
        return carry

    lax.fori_loop(0, PEER_HEADS, head, 0)
    for c in range(ncol):
        idx_ref[c * LANES:(c + 1) * LANES, :] = idx_t_ref[:, c * LANES:(c + 1) * LANES].T
        gate_ref[c * LANES:(c + 1) * LANES, :] = gate_t_ref[:, c * LANES:(c + 1) * LANES].T


def _block_diag_subkeys(subkeys):
    z = jnp.zeros_like(subkeys[:, 0])
    top = jnp.concatenate([subkeys[:, 0], z], axis=-1)
    bot = jnp.concatenate([z, subkeys[:, 1]], axis=-1)
    return jnp.concatenate([top, bot], axis=1)


def peer_route(h, norm_w, wq, subkeys, *, tm=512):
    t, d = h.shape
    sk = _block_diag_subkeys(subkeys)
    return pl.pallas_call(
        functools.partial(_peer_route_body, tm=tm),
        grid=(t // tm,),
        in_specs=[
            pl.BlockSpec((tm, d), lambda i: (i, 0)),
            pl.BlockSpec((1, d), lambda i: (0, 0)),
            pl.BlockSpec(wq.shape, lambda i: (0, 0)),
            pl.BlockSpec(wq.shape, lambda i: (0, 0)),
            pl.BlockSpec(sk.shape, lambda i: (0, 0, 0)),
        ],
        out_specs=[pl.BlockSpec((tm, d), lambda i: (i, 0)),
                   pl.BlockSpec((tm, PEER_SEL), lambda i: (i, 0)),
                   pl.BlockSpec((tm, PEER_SEL), lambda i: (i, 0))],
        out_shape=[jax.ShapeDtypeStruct((t, d), jnp.float32),
                   jax.ShapeDtypeStruct((t, PEER_SEL), jnp.int32),
                   jax.ShapeDtypeStruct((t, PEER_SEL), jnp.float32)],
        scratch_shapes=[pltpu.VMEM((tm, d), jnp.float32), pltpu.VMEM((PEER_SEL, tm), jnp.int32),
                        pltpu.VMEM((PEER_SEL, tm), jnp.float32)]
                       + [pltpu.VMEM((tm // LANES, PEER_TOPK, LANES), jnp.float32),
                          pltpu.VMEM((tm // LANES, PEER_TOPK, LANES), jnp.float32)] * 3,
        compiler_params=pltpu.CompilerParams(dimension_semantics=("arbitrary",), vmem_limit_bytes=VMEM_LIMIT_BYTES),
        name="peer_route",
    )(h, norm_w.reshape(1, d), *_split_bf16(wq), sk)


def pack_table(tab):
    n, d = tab.shape
    b = lax.bitcast_convert_type(tab.astype(jnp.bfloat16), jnp.uint16).astype(jnp.uint32)
    w = b[:, : d // 2] | (b[:, d // 2:] << 16)
    return lax.bitcast_convert_type(w, jnp.int32).reshape(n * ROW_SUBLANES, LANES)


def _unpack(w):
    lo = lax.bitcast_convert_type(w << 16, jnp.float32)
    hi = lax.bitcast_convert_type(w & jnp.int32(-65536), jnp.float32)
    return lo, hi


STAGE_STRIDE = 136
STAGE_ROWS = ROW_SUBLANES * STAGE_STRIDE
STAGE_TOKENS = 8


def _load_table(tab_hbm, tab_vmem, sem, st_first_read):
    @pl.when(pl.program_id(0) == 0)
    def _():
        cp = pltpu.make_async_copy(tab_hbm, tab_vmem, sem)
        cp.start()
        st_first_read[...] = jnp.zeros_like(st_first_read)
        cp.wait()


def _stage_rows(i, k):
    return pl.ds(i * STAGE_ROWS + k, ROW_SUBLANES, stride=STAGE_STRIDE)


def _gather_rows(idx_ref, t, i, tab_vmem, st_ref):
    for k in range(PEER_SEL):
        e = idx_ref[t, k]
        st_ref[_stage_rows(i, k), :] = tab_vmem[pl.ds(pl.multiple_of(e, ROW_SUBLANES), ROW_SUBLANES), :]


def _staged_matrix(st_ref, i):
    r0 = i * STAGE_ROWS
    w = jnp.concatenate([st_ref[r0 + s * STAGE_STRIDE:r0 + s * STAGE_STRIDE + PEER_SEL, :] for s in range(ROW_SUBLANES)],
                        axis=1)
    lo, hi = _unpack(w)
    return jnp.concatenate([lo, hi], axis=1).astype(jnp.bfloat16)


def _mxu_lhs(row):
    hi, lo = _split_bf16(row)
    return jnp.concatenate([hi, lo, jnp.zeros((SUBLANES - 2, row.shape[1]), jnp.bfloat16)], axis=0)


def _staged_pipeline(idx_ref, tab_vmem, st_a, st_b, tb, consume, out_ref):
    n = STAGE_TOKENS

    def half(t_gather, st_fill, t_mxu, st_read):
        rows = []
        for i in range(n):
            _gather_rows(idx_ref, t_gather + i, i, tab_vmem, st_fill)
            rows.append(consume(st_read, i, t_mxu + i))
        out_ref[pl.ds(t_mxu, n), :] = jnp.concatenate(rows, axis=0)

    def two_groups(j, carry):
        t0 = pl.multiple_of(j * 2 * n, 2 * n)
        tp = pl.multiple_of(jnp.maximum(t0 - n, 0), n)
        half(t0, st_a, tp, st_b)
        half(t0 + n, st_b, t0, st_a)
        return carry

    lax.fori_loop(0, tb // (2 * n), two_groups, 0)
    out_ref[tb - n:tb, :] = jnp.concatenate([consume(st_b, i, tb - n + i) for i in range(n)], axis=0)


def _peer_u_body(idx_ref, x_ref, gate_ref, tab_hbm, act_ref, tab_vmem, st_a, st_b, sem, *, tb):
    _load_table(tab_hbm, tab_vmem, sem, st_b)

    def dot_products(st_ref, i, t):
        r = lax.dot_general(_mxu_lhs(x_ref[pl.ds(t, 1), :]), _staged_matrix(st_ref, i), (((1,), (1,)), ((), ())),
                            preferred_element_type=jnp.float32)
        return r[0:1] + r[1:2]

    _staged_pipeline(idx_ref, tab_vmem, st_a, st_b, tb, dot_products, act_ref)
    hp = act_ref[...]
    act_ref[...] = 0.5 * hp * (1.0 + lax.erf(hp * (2.0 ** -0.5))) * gate_ref[...]


def peer_u(idx, x, gate, tab, *, tb=128):
    t, d = x.shape
    stage = pltpu.VMEM((STAGE_TOKENS * STAGE_ROWS, LANES), jnp.int32)
    return pl.pallas_call(
        functools.partial(_peer_u_body, tb=tb),
        grid=(t // tb,),
        in_specs=[
            pl.BlockSpec((tb, PEER_SEL), lambda i: (i, 0), memory_space=pltpu.SMEM),
            pl.BlockSpec((tb, d), lambda i: (i, 0)),
            pl.BlockSpec((tb, PEER_SEL), lambda i: (i, 0)),
            pl.BlockSpec(memory_space=pl.ANY),
        ],
        out_specs=pl.BlockSpec((tb, PEER_SEL), lambda i: (i, 0)),
        out_shape=jax.ShapeDtypeStruct((t, PEER_SEL), jnp.float32),
        scratch_shapes=[pltpu.VMEM(tab.shape, jnp.int32), stage, stage, pltpu.SemaphoreType.DMA],
        compiler_params=pltpu.CompilerParams(dimension_semantics=("arbitrary",), vmem_limit_bytes=VMEM_LIMIT_BYTES),
        name="peer_u",
    )(idx, x, gate, tab)


def _peer_v_body(idx_ref, act_ref, h_ref, nw_ref, tab_hbm, o_ref, tab_vmem, st_a, st_b, sem, *, tb, final_norm):
    _load_table(tab_hbm, tab_vmem, sem, st_b)

    def weighted_sum(st_ref, i, t):
        r = jnp.dot(_mxu_lhs(act_ref[pl.ds(t, 1), :]), _staged_matrix(st_ref, i), preferred_element_type=jnp.float32)
        return r[0:1] + r[1:2]

    _staged_pipeline(idx_ref, tab_vmem, st_a, st_b, tb, weighted_sum, o_ref)
    y = h_ref[...] + o_ref[...]
    if final_norm:
        y = y * lax.rsqrt(jnp.mean(y * y, axis=-1, keepdims=True) + NORM_EPS) * nw_ref[...]
    o_ref[...] = y


def peer_v(idx, act, tab, h, norm_w, *, final_norm, tb=128):
    t, d = h.shape
    stage = pltpu.VMEM((STAGE_TOKENS * STAGE_ROWS, LANES), jnp.int32)
    return pl.pallas_call(
        functools.partial(_peer_v_body, tb=tb, final_norm=final_norm),
        grid=(t // tb,),
        in_specs=[
            pl.BlockSpec((tb, PEER_SEL), lambda i: (i, 0), memory_space=pltpu.SMEM),
            pl.BlockSpec((tb, PEER_SEL), lambda i: (i, 0)),
            pl.BlockSpec((tb, d), lambda i: (i, 0)),
            pl.BlockSpec((1, d), lambda i: (0, 0)),
            pl.BlockSpec(memory_space=pl.ANY),
        ],
        out_specs=pl.BlockSpec((tb, d), lambda i: (i, 0)),
        out_shape=jax.ShapeDtypeStruct((t, d), jnp.float32),
        scratch_shapes=[pltpu.VMEM(tab.shape, jnp.int32), stage, stage, pltpu.SemaphoreType.DMA],
        compiler_params=pltpu.CompilerParams(dimension_semantics=("arbitrary",), vmem_limit_bytes=VMEM_LIMIT_BYTES),
        name="peer_v",
    )(idx, act, h, norm_w.reshape(1, d), tab)


def kernel(x, norm_mix_w, w_in, shift_mu, attn_sinks, decay_w0, decay_w2, iclr_a0, iclr_a2,
           gate_g2, k_k, k_a, r_k, ln_x_w, ln_x_b, proj_attn, proj_rwkv, w_out, norm_ffn_w,
           peer_wq, peer_subkeys, peer_u_tab, peer_v_tab, norm_final_w):
    b, s, d = x.shape
    t = b * s
    h = x.reshape(t, d)
    for l in range(DEPTH):
        w_lay, mu_lay = layout_in_proj(w_in[l], shift_mu[l])
        qkv, zr, gates = in_proj(h, norm_mix_w[l], w_lay)
        y_attn = swa_attention(qkv, attn_sinks[l], b, s)
        r, dec, k, v, kk, nb, g, bonus = rwkv_prep(zr, mu_lay, decay_w0[l], decay_w2[l], iclr_a0[l], iclr_a2[l],
                                                   gate_g2[l], k_k[l], k_a[l], r_k[l].reshape(-1), b, s)
        to_chains = lambda a: jnp.swapaxes(a.reshape(s, b * RWKV_HEADS, RWKV_HEAD), 1, 2)
        y = rwkv_scan(to_chains(r), to_chains(dec), to_chains(k), to_chains(v), to_chains(kk), to_chains(nb))
        y = jnp.swapaxes(y, 1, 2).reshape(s, b * RWKV_DIM)
        h = merge_out(h, y, g, bonus, y_attn, gates, ln_x_w[l], ln_x_b[l], proj_attn[l], proj_rwkv[l], w_out[l])
        xn2, idx, gate = peer_route(h, norm_ffn_w[l], peer_wq[l], peer_subkeys[l])
        act = peer_u(idx, xn2, gate, pack_table(peer_u_tab[l]))
        h = peer_v(idx, act, pack_table(peer_v_tab[l]), h, norm_final_w, final_norm=(l + 1 == DEPTH))
    return h.reshape(b, s, d)
```

```python
import functools

import jax, jax.numpy as jnp
from jax import lax
from jax.experimental import pallas as pl
from jax.experimental.pallas import tpu as pltpu

D_MODEL = 1024
DEPTH = 1

HEAD_DIM = 64
ATTN_HEADS = 8
ATTN_KV_HEADS = 2
ATTN_GROUP = ATTN_HEADS // ATTN_KV_HEADS
ATTN_DIM = ATTN_HEADS * HEAD_DIM
KV_DIM = ATTN_KV_HEADS * HEAD_DIM
WINDOW = 128
BLOCK = WINDOW
ROPE_THETA = 10000.0

RWKV_HEADS = 8
RWKV_HEAD = 64
RWKV_DIM = RWKV_HEADS * RWKV_HEAD
DECAY_LORA = 64
ICLR_LORA = 64
GATE_LORA = 160
RWKV_GN_EPS = 64e-5

PEER_HEADS = 8
N_KEYS = 128
PEER_HALF = 64
PEER_TOPK = 16
PEER_SEL = PEER_HEADS * PEER_TOPK

NORM_EPS = 1e-5
NEG_INF = -1e30

LANES = 128
SUBLANES = 8
VMEM_LIMIT_BYTES = 56 * 1024 * 1024

ROW_WORDS = D_MODEL // 2
ROW_SUBLANES = ROW_WORDS // LANES
NEG_FILL = -3.0e38
HI = lax.Precision.HIGHEST
RWKV_VGROUPS = RWKV_HEAD // SUBLANES

QKV_COLS = ATTN_DIM + 2 * KV_DIM
LORA_PAD = (LANES, LANES, 2 * LANES)
RWKV_COLS = 3 * RWKV_DIM + sum(LORA_PAD)
GATE_COLS = 2 * D_MODEL


def _pad_cols(w, width):
    return jnp.pad(w, ((0, 0), (0, width - w.shape[1])))


def _pad_rows(w, height):
    return jnp.pad(w, ((0, height - w.shape[0]), (0, 0)))


def layout_in_proj(w_in, shift_mu):
    o1 = QKV_COLS
    o_r = o1 + 3 * RWKV_DIM
    o_w = o_r + DECAY_LORA
    o_a = o_w + ICLR_LORA
    o_g = o_a + GATE_LORA
    parts = [w_in[:, :o_r], _pad_cols(w_in[:, o_r:o_w], LORA_PAD[0]), _pad_cols(w_in[:, o_w:o_a], LORA_PAD[1]),
             _pad_cols(w_in[:, o_a:o_g], LORA_PAD[2]), w_in[:, o_g:]]
    mu = shift_mu.reshape(1, -1)
    m_r = 3 * RWKV_DIM
    mu_parts = [mu[:, :m_r], _pad_cols(mu[:, m_r:m_r + DECAY_LORA], LORA_PAD[0]),
                _pad_cols(mu[:, m_r + DECAY_LORA:m_r + DECAY_LORA + ICLR_LORA], LORA_PAD[1]),
                _pad_cols(mu[:, m_r + DECAY_LORA + ICLR_LORA:], LORA_PAD[2])]
    return jnp.concatenate(parts, axis=1), jnp.concatenate(mu_parts, axis=1)


def _in_proj_body(x_ref, nw_ref, w_ref, qkv_ref, zr_ref, gate_ref):
    xf = x_ref[...]
    xn = xf * lax.rsqrt(jnp.mean(xf * xf, axis=-1, keepdims=True) + NORM_EPS) * nw_ref[...]
    z = jnp.dot(xn.astype(jnp.bfloat16), w_ref[...], preferred_element_type=jnp.float32)
    qkv_ref[...] = z[:, :QKV_COLS]
    zr_ref[...] = z[:, QKV_COLS:QKV_COLS + RWKV_COLS]
    gate_ref[...] = jax.nn.sigmoid(z[:, QKV_COLS + RWKV_COLS:]).astype(jnp.bfloat16)


def in_proj(x, norm_w, w_lay, *, tm=256):
    t, d = x.shape
    n = w_lay.shape[1]
    return pl.pallas_call(
        _in_proj_body,
        grid=(t // tm,),
        in_specs=[pl.BlockSpec((tm, d), lambda i: (i, 0)), pl.BlockSpec((1, d), lambda i: (0, 0)),
                  pl.BlockSpec((d, n), lambda i: (0, 0))],
        out_specs=[pl.BlockSpec((tm, QKV_COLS), lambda i: (i, 0)), pl.BlockSpec((tm, RWKV_COLS), lambda i: (i, 0)),
                   pl.BlockSpec((tm, GATE_COLS), lambda i: (i, 0))],
        out_shape=[jax.ShapeDtypeStruct((t, QKV_COLS), jnp.float32), jax.ShapeDtypeStruct((t, RWKV_COLS), jnp.float32),
                   jax.ShapeDtypeStruct((t, GATE_COLS), jnp.bfloat16)],
        compiler_params=pltpu.CompilerParams(dimension_semantics=("arbitrary",), vmem_limit_bytes=VMEM_LIMIT_BYTES),
        name="in_proj",
    )(x, norm_w.reshape(1, d), w_lay.astype(jnp.bfloat16))


def rope_tables(s):
    inv = ROPE_THETA ** (-jnp.arange(0, HEAD_DIM, 2, dtype=jnp.float32) / HEAD_DIM)
    ang = jnp.arange(s, dtype=jnp.float32)[:, None] * inv[None, :]
    cos = jnp.concatenate([jnp.cos(ang), jnp.cos(ang)], axis=-1)
    sin = jnp.concatenate([-jnp.sin(ang), jnp.sin(ang)], axis=-1)
    reps = LANES // HEAD_DIM
    return jnp.tile(cos, (1, reps)), jnp.tile(sin, (1, reps))


def _rope(x, cos, sin, first_half):
    rot = jnp.where(first_half, pltpu.roll(x, LANES - HEAD_DIM // 2, 1), pltpu.roll(x, HEAD_DIM // 2, 1))
    return x * cos + rot * sin


def _attn_body(cur_ref, prev_ref, cos_ref, sin_ref, cosp_ref, sinp_ref, sink_ref, o_ref):
    n = pl.program_id(1)
    lane = lax.broadcasted_iota(jnp.int32, (BLOCK, LANES), 1)
    first_half = (lane % HEAD_DIM) < HEAD_DIM // 2
    low = lane < HEAD_DIM
    cos, sin = cos_ref[...], sin_ref[...]
    cur = cur_ref[0]
    prev = prev_ref[0]
    k_cur = _rope(cur[:, ATTN_DIM:ATTN_DIM + KV_DIM], cos, sin, first_half)
    k_prev = _rope(prev[:, ATTN_DIM:ATTN_DIM + KV_DIM], cosp_ref[...], sinp_ref[...], first_half)
    keys = jnp.concatenate([k_prev, k_cur], axis=0)
    vals = jnp.concatenate([prev[:, ATTN_DIM + KV_DIM:], cur[:, ATTN_DIM + KV_DIM:]], axis=0)
    keys_sw = pltpu.roll(keys, HEAD_DIM, 1)
    vals_sw = pltpu.roll(vals, HEAD_DIM, 1)
    low2 = lax.broadcasted_iota(jnp.int32, (2 * BLOCK, LANES), 1) < HEAD_DIM
    qi = lax.broadcasted_iota(jnp.int32, (BLOCK, 2 * BLOCK), 0) + BLOCK
    kj = lax.broadcasted_iota(jnp.int32, (BLOCK, 2 * BLOCK), 1)
    diff = qi - kj
    mask = (diff >= 0) & (diff < WINDOW) & ((kj >= BLOCK) | (n > 0))
    scale = HEAD_DIM ** -0.5
    outs = []
    for j in range(ATTN_HEADS // 2):
        g = (2 * j) // ATTN_GROUP
        kg = (jnp.where(low2, keys, keys_sw) if g == 0 else jnp.where(low2, keys_sw, keys)).astype(jnp.bfloat16)
        vg = (jnp.where(low2, vals, vals_sw) if g == 0 else jnp.where(low2, vals_sw, vals)).astype(jnp.bfloat16)
        qg = _rope(cur[:, j * LANES:(j + 1) * LANES], cos, sin, first_half) * scale
        halves = []
        for half in range(2):
            qh = jnp.where(low if half == 0 else ~low, qg, 0.0).astype(jnp.bfloat16)
            logits = lax.dot_general(qh, kg, (((1,), (1,)), ((), ())), preferred_element_type=jnp.float32)
            logits = jnp.where(mask, logits, NEG_INF)
            sink = sink_ref[2 * j + half]
            m = jnp.maximum(jnp.max(logits, axis=-1, keepdims=True), sink)
            p = jnp.exp(logits - m)
            denom = jnp.sum(p, axis=-1, keepdims=True) + jnp.exp(sink - m)
            o = jnp.dot(p.astype(jnp.bfloat16), vg, preferred_element_type=jnp.float32)
            halves.append(o / denom)
        outs.append(jnp.where(low, halves[0], halves[1]))
    o_ref[0] = jnp.concatenate(outs, axis=1).astype(o_ref.dtype)


def swa_attention(qkv, sinks, b, s):
    nb = s // BLOCK
    cos, sin = rope_tables(s)
    qkv3 = qkv.reshape(b, s, QKV_COLS)
    cur_spec = pl.BlockSpec((1, BLOCK, QKV_COLS), lambda bi, n: (bi, n, 0))
    prev_spec = pl.BlockSpec((1, BLOCK, QKV_COLS), lambda bi, n: (bi, jnp.maximum(n - 1, 0), 0))
    tab = pl.BlockSpec((BLOCK, LANES), lambda bi, n: (n, 0))
    tabp = pl.BlockSpec((BLOCK, LANES), lambda bi, n: (jnp.maximum(n - 1, 0), 0))
    out = pl.pallas_call(
        _attn_body,
        grid=(b, nb),
        in_specs=[cur_spec, prev_spec, tab, tab, tabp, tabp, pl.BlockSpec(memory_space=pltpu.SMEM)],
        out_specs=pl.BlockSpec((1, BLOCK, ATTN_DIM), lambda bi, n: (bi, n, 0)),
        out_shape=jax.ShapeDtypeStruct((b, s, ATTN_DIM), jnp.bfloat16),
        compiler_params=pltpu.CompilerParams(dimension_semantics=("arbitrary", "arbitrary"),
                                             vmem_limit_bytes=VMEM_LIMIT_BYTES),
        name="swa_attention",
    )(qkv3, qkv3, cos, sin, cos, sin, sinks.astype(jnp.float32))
    return out.reshape(b * s, ATTN_DIM)


def head_block_ones():
    i = jnp.arange(RWKV_DIM) // RWKV_HEAD
    return (i[:, None] == i[None, :]).astype(jnp.bfloat16)


def _softplus(x):
    return jnp.maximum(x, 0.0) + jnp.log1p(jnp.exp(-jnp.abs(x)))


def _split_bf16(x):
    hi = x.astype(jnp.bfloat16)
    return hi, (x - hi.astype(jnp.float32)).astype(jnp.bfloat16)


def _dot3(x, wh_ref, wl_ref):
    xh, xl = _split_bf16(x)
    dot = lambda a, w_ref: jnp.dot(a, w_ref[...], preferred_element_type=jnp.float32)
    return dot(xh, wh_ref) + (dot(xh, wl_ref) + dot(xl, wh_ref))


def _head_sums(x, ones_ref):
    x1, rest = _split_bf16(x)
    x2 = rest
    x3 = (x - x1.astype(jnp.float32) - x2.astype(jnp.float32)).astype(jnp.bfloat16)
    dot = lambda a: jnp.dot(a, ones_ref[...], preferred_element_type=jnp.float32)
    return dot(x1) + (dot(x2) + dot(x3))


def _rwkv_prep_body(z_ref, zp_ref, mu_ref, w0_ref, w2h_ref, w2l_ref, a0_ref, a2h_ref, a2l_ref, g2h_ref, g2l_ref,
                    kk_w_ref, ka_ref, rk_ref, ones_ref,
                    r_ref, dec_ref, k_ref, v_ref, kk_ref, nb_ref, g_ref, bonus_ref, *, ts):
    z = z_ref[0]
    row = lax.broadcasted_iota(jnp.int32, (ts, 1), 0)
    last_prev = jnp.where(pl.program_id(1) > 0, zp_ref[0][SUBLANES - 1:SUBLANES, :], 0.0)
    z_prev = jnp.where(row == 0, last_prev, pltpu.roll(z, 1, 0))
    zs = z + mu_ref[...] * (z_prev - z)
    c = RWKV_DIM
    r, k, v = zs[:, :c], zs[:, c:2 * c], zs[:, 2 * c:3 * c]
    o = 3 * c
    w_lo = zs[:, o:o + LORA_PAD[0]]
    a_lo = zs[:, o + LORA_PAD[0]:o + LORA_PAD[0] + LORA_PAD[1]]
    g_lo = zs[:, o + LORA_PAD[0] + LORA_PAD[1]:]
    w = -_softplus(-(w0_ref[...] + _dot3(jnp.tanh(w_lo), w2h_ref, w2l_ref))) - 0.5
    dec_ref[...] = jnp.exp(-jnp.exp(w))
    a = jax.nn.sigmoid(a0_ref[...] + _dot3(a_lo, a2h_ref, a2l_ref))
    g_ref[...] = _dot3(jax.nn.sigmoid(g_lo), g2h_ref, g2l_ref)
    kk = k * kk_w_ref[...]
    kk = kk / jnp.maximum(jnp.sqrt(_head_sums(kk * kk, ones_ref)), 1e-12)
    kmod = k * (1.0 + (a - 1.0) * ka_ref[...])
    r_ref[...] = r
    k_ref[...] = kmod
    v_ref[...] = v
    kk_ref[...] = kk
    nb_ref[...] = -kk * a
    bonus_ref[...] = _head_sums(r * kmod * rk_ref[...], ones_ref) * v


def rwkv_prep(zr, mu_lay, w0, w2, a0, a2, g2, k_k, k_a, r_k, b, s, *, ts=256):
    t = b * s
    zr3 = zr.reshape(b, s, RWKV_COLS)
    row = lambda x: x.reshape(1, -1)
    w2p, a2p, g2p = _pad_rows(w2, LORA_PAD[0]), _pad_rows(a2, LORA_PAD[1]), _pad_rows(g2, LORA_PAD[2])
    const = lambda shape: pl.BlockSpec(shape, lambda bi, si: (0,) * len(shape))
    nblk = s // ts
    tok_spec = pl.BlockSpec((ts, RWKV_DIM), lambda bi, si: (bi * nblk + si, 0))
    time_spec = pl.BlockSpec((ts, RWKV_DIM), lambda bi, si: (si, bi))
    tok_shape = jax.ShapeDtypeStruct((t, RWKV_DIM), jnp.float32)
    time_shape = jax.ShapeDtypeStruct((s, b * RWKV_DIM), jnp.float32)
    return pl.pallas_call(
        functools.partial(_rwkv_prep_body, ts=ts),
        grid=(b, nblk),
        in_specs=[pl.BlockSpec((1, ts, RWKV_COLS), lambda bi, si: (bi, si, 0)),
                  pl.BlockSpec((1, SUBLANES, RWKV_COLS), lambda bi, si: (bi, jnp.maximum(si * (ts // SUBLANES) - 1, 0), 0)),
                  const((1, RWKV_COLS)), const((1, RWKV_DIM)), const(w2p.shape), const(w2p.shape), const((1, RWKV_DIM)),
                  const(a2p.shape), const(a2p.shape), const(g2p.shape), const(g2p.shape),
                  const((1, RWKV_DIM)), const((1, RWKV_DIM)), const((1, RWKV_DIM)), const((RWKV_DIM, RWKV_DIM))],
        out_specs=[time_spec] * 6 + [tok_spec] * 2,
        out_shape=[time_shape] * 6 + [tok_shape] * 2,
        compiler_params=pltpu.CompilerParams(dimension_semantics=("arbitrary", "arbitrary"),
                                             vmem_limit_bytes=VMEM_LIMIT_BYTES),
        name="rwkv_prep",
    )(zr3, zr3, mu_lay, row(w0), *_split_bf16(w2p), row(a0), *_split_bf16(a2p), *_split_bf16(g2p),
      row(k_k), row(k_a), row(r_k), head_block_ones())


SCAN_PARTIALS = 4


def _tree_sum(parts):
    while len(parts) > 1:
        parts = [parts[i] + parts[i + 1] for i in range(0, len(parts), 2)]
    return parts[0]


def _scan_body(r_ref, w_ref, k_ref, v_ref, kk_ref, nb_ref, y_ref, *s_refs, tc):
    @pl.when(pl.program_id(1) == 0)
    def _():
        for s_ref in s_refs:
            s_ref[...] = jnp.zeros_like(s_ref)

    def step(t, carry):
        for g, s_ref in enumerate(s_refs):
            rows = slice(g * SUBLANES, (g + 1) * SUBLANES)
            acc = [None] * SCAN_PARTIALS
            for k in range(RWKV_HEAD):
                p = s_ref[k] * kk_ref[t, pl.ds(k, 1), :]
                acc[k % SCAN_PARTIALS] = p if acc[k % SCAN_PARTIALS] is None else acc[k % SCAN_PARTIALS] + p
            sa = _tree_sum(acc)
            vv = v_ref[t, rows, :]
            yacc = [None] * SCAN_PARTIALS
            for k in range(RWKV_HEAD):
                s_new = s_ref[k] * w_ref[t, pl.ds(k, 1), :] + sa * nb_ref[t, pl.ds(k, 1), :] + vv * k_ref[t, pl.ds(k, 1), :]
                s_ref[k] = s_new
                p = s_new * r_ref[t, pl.ds(k, 1), :]
                yacc[k % SCAN_PARTIALS] = p if yacc[k % SCAN_PARTIALS] is None else yacc[k % SCAN_PARTIALS] + p
            y_ref[t, rows, :] = _tree_sum(yacc)
        return carry

    lax.fori_loop(0, tc, step, 0)


def rwkv_scan(r, w, k, v, kk, nb, *, tc=32):
    s, n, c = r.shape
    spec = pl.BlockSpec((tc, n, LANES), lambda ci, ti: (ti, 0, ci))
    return pl.pallas_call(
        functools.partial(_scan_body, tc=tc),
        grid=(c // LANES, s // tc),
        in_specs=[spec] * 6,
        out_specs=spec,
        out_shape=jax.ShapeDtypeStruct((s, n, c), jnp.float32),
        scratch_shapes=[pltpu.VMEM((n, SUBLANES, LANES), jnp.float32)] * RWKV_VGROUPS,
        compiler_params=pltpu.CompilerParams(dimension_semantics=("arbitrary", "arbitrary"),
                                             vmem_limit_bytes=VMEM_LIMIT_BYTES),
        name="rwkv_scan",
    )(r, w, k, v, kk, nb)


def _merge_body(x_ref, y_ref, g_ref, bonus_ref, ya_ref, gate_ref, lnw_ref, lnb_ref, ones_ref, pa_ref, pb_ref, wo_ref, h_ref):
    dot_bf = lambda x, w: jnp.dot(x.astype(jnp.bfloat16), w, preferred_element_type=jnp.float32)
    y = y_ref[...]
    inv_n = 1.0 / RWKV_HEAD
    mean = _head_sums(y, ones_ref) * inv_n
    yc = y - mean
    var = _head_sums(yc * yc, ones_ref) * inv_n
    yn = yc * lax.rsqrt(var + RWKV_GN_EPS) * lnw_ref[...] + lnb_ref[...] + bonus_ref[...]
    y_rwkv = yn * g_ref[...]
    gates = gate_ref[...].astype(jnp.float32)
    merged = gates[:, :D_MODEL] * dot_bf(ya_ref[...], pa_ref[...]) + gates[:, D_MODEL:] * dot_bf(y_rwkv, pb_ref[...])
    h_ref[...] = x_ref[...] + dot_bf(merged, wo_ref[...])


def merge_out(x, y, g, bonus, y_attn, gates, ln_w, ln_b, proj_attn, proj_rwkv, w_out, *, tm=256):
    t, d = x.shape
    s = y.shape[0]
    nblk = s // tm
    row = lambda v: v.reshape(1, -1)
    tok = lambda w: pl.BlockSpec((tm, w), lambda bi, si: (bi * nblk + si, 0))
    const = lambda shape: pl.BlockSpec(shape, lambda bi, si: (0,) * len(shape))
    bf = lambda w: w.astype(jnp.bfloat16)
    return pl.pallas_call(
        _merge_body,
        grid=(t // s, nblk),
        in_specs=[tok(d), pl.BlockSpec((tm, RWKV_DIM), lambda bi, si: (si, bi)), tok(RWKV_DIM), tok(RWKV_DIM),
                  tok(ATTN_DIM), tok(GATE_COLS),
                  const((1, RWKV_DIM)), const((1, RWKV_DIM)), const((RWKV_DIM, RWKV_DIM)),
                  const(proj_attn.shape), const(proj_rwkv.shape), const(w_out.shape)],
        out_specs=tok(d),
        out_shape=jax.ShapeDtypeStruct((t, d), jnp.float32),
        compiler_params=pltpu.CompilerParams(dimension_semantics=("arbitrary", "arbitrary"),
                                             vmem_limit_bytes=VMEM_LIMIT_BYTES),
        name="merge_out",
    )(x, y, g, bonus, y_attn, gates, row(ln_w), row(ln_b), head_block_ones(), bf(proj_attn), bf(proj_rwkv), bf(w_out))


def _top16(problems):
    arrays = [p[0] for p in problems]
    for i in range(PEER_TOPK):
        for j, (_, ids, big, vals_ref, ids_ref) in enumerate(problems):
            a = arrays[j]
            m = jnp.max(a, axis=0, keepdims=True)
            r = jnp.min(jnp.where(a == m, ids, big), axis=0, keepdims=True)
            vals_ref[i:i + 1, :] = m
            ids_ref[i:i + 1, :] = r
            arrays[j] = jnp.where(ids == r, NEG_FILL, a)


def _select_rows(pos, table):
    out = jnp.zeros(pos.shape, table.dtype)
    for a in range(PEER_TOPK):
        out = jnp.where(pos == a, table[a:a + 1, :], out)
    return out


def _peer_route_body(h_ref, nw_ref, wqh_ref, wql_ref, sk_ref, xn_ref, idx_ref, gate_ref, q_ref, idx_t_ref, gate_t_ref,
                     s1_ref, i1_ref, s2_ref, i2_ref, top_ref, pos_ref, *, tm):
    xf = h_ref[...]
    xn = xf * lax.rsqrt(jnp.mean(xf * xf, axis=-1, keepdims=True) + NORM_EPS) * nw_ref[...]
    xn_ref[...] = xn
    q_ref[...] = _dot3(xn, wqh_ref, wql_ref)
    key_ids = lax.broadcasted_iota(jnp.int32, (N_KEYS, LANES), 0).astype(jnp.float32)
    sub = lax.broadcasted_iota(jnp.int32, (SUBLANES, LANES), 0)
    cand_ids = jnp.concatenate([sub, sub + SUBLANES] + [sub + a * PEER_TOPK for a in range(1, SUBLANES)]
                               + [(sub + SUBLANES) * PEER_TOPK], axis=0).astype(jnp.float32)
    ncol = tm // LANES

    def head(h, carry):
        col0 = pl.multiple_of(h * 2 * PEER_HALF, 2 * PEER_HALF)
        row0 = pl.multiple_of(h * PEER_TOPK, PEER_TOPK)
        for c in range(ncol):
            qh = q_ref[c * LANES:(c + 1) * LANES, pl.ds(col0, 2 * PEER_HALF)]
            sc = lax.dot_general(sk_ref[h], qh, (((1,), (1,)), ((), ())),
                                 preferred_element_type=jnp.float32, precision=HI)
            _top16([(sc[:N_KEYS], key_ids, float(N_KEYS), s1_ref.at[c], i1_ref.at[c]),
                    (sc[N_KEYS:], key_ids, float(N_KEYS), s2_ref.at[c], i2_ref.at[c])])
        problems = []
        for c in range(ncol):
            s1, s2 = s1_ref[c], s2_ref[c]
            cand = jnp.concatenate([s1[0:1] + s2[0:SUBLANES], s1[0:1] + s2[SUBLANES:]]
                                   + [s1[a:a + 1] + s2[0:SUBLANES] for a in range(1, SUBLANES)]
                                   + [s1[SUBLANES:] + s2[0:1]], axis=0)
            problems.append((cand, cand_ids, float(PEER_TOPK * PEER_TOPK), top_ref.at[c], pos_ref.at[c]))
        _top16(problems)
        for c in range(ncol):
            top, pos = top_ref[c], pos_ref[c].astype(jnp.int32)
            i1, i2 = i1_ref[c].astype(jnp.int32), i2_ref[c].astype(jnp.int32)
            e = _select_rows(pos // PEER_TOPK, i1) * N_KEYS + _select_rows(pos % PEER_TOPK, i2)
            p = jnp.exp(top - top[0:1, :])
            idx_t_ref[pl.ds(row0, PEER_TOPK), c * LANES:(c + 1) * LANES] = e * ROW_SUBLANES
            gate_t_ref[pl.ds(row0, PEER_TOPK), c * LANES:(c + 1) * LANES] = p / jnp.sum(p, axis=0, keepdims=True)
        return carry

    lax.fori_loop(0, PEER_HEADS, head, 0)
    for c in range(ncol):
        idx_ref[c * LANES:(c + 1) * LANES, :] = idx_t_ref[:, c * LANES:(c + 1) * LANES].T
        gate_ref[c * LANES:(c + 1) * LANES, :] = gate_t_ref[:, c * LANES:(c + 1) * LANES].T


def _block_diag_subkeys(subkeys):
    z = jnp.zeros_like(subkeys[:, 0])
    top = jnp.concatenate([subkeys[:, 0], z], axis=-1)
    bot = jnp.concatenate([z, subkeys[:, 1]], axis=-1)
    return jnp.concatenate([top, bot], axis=1)


def peer_route(h, norm_w, wq, subkeys, *, tm=1024):
    t, d = h.shape
    sk = _block_diag_subkeys(subkeys)
    return pl.pallas_call(
        functools.partial(_peer_route_body, tm=tm),
        grid=(t // tm,),
        in_specs=[
            pl.BlockSpec((tm, d), lambda i: (i, 0)),
            pl.BlockSpec((1, d), lambda i: (0, 0)),
            pl.BlockSpec(wq.shape, lambda i: (0, 0)),
            pl.BlockSpec(wq.shape, lambda i: (0, 0)),
            pl.BlockSpec(sk.shape, lambda i: (0, 0, 0)),
        ],
        out_specs=[pl.BlockSpec((tm, d), lambda i: (i, 0)),
                   pl.BlockSpec((tm, PEER_SEL), lambda i: (i, 0)),
                   pl.BlockSpec((tm, PEER_SEL), lambda i: (i, 0))],
        out_shape=[jax.ShapeDtypeStruct((t, d), jnp.float32),
                   jax.ShapeDtypeStruct((t, PEER_SEL), jnp.int32),
                   jax.ShapeDtypeStruct((t, PEER_SEL), jnp.float32)],
        scratch_shapes=[pltpu.VMEM((tm, d), jnp.float32), pltpu.VMEM((PEER_SEL, tm), jnp.int32),
                        pltpu.VMEM((PEER_SEL, tm), jnp.float32)]
                       + [pltpu.VMEM((tm // LANES, PEER_TOPK, LANES), jnp.float32),
                          pltpu.VMEM((tm // LANES, PEER_TOPK, LANES), jnp.float32)] * 3,
        compiler_params=pltpu.CompilerParams(dimension_semantics=("arbitrary",), vmem_limit_bytes=VMEM_LIMIT_BYTES),
        name="peer_route",
    )(h, norm_w.reshape(1, d), *_split_bf16(wq), sk)


def pack_table(tab):
    n, d = tab.shape
    b = lax.bitcast_convert_type(tab.astype(jnp.bfloat16), jnp.uint16).astype(jnp.uint32)
    w = b[:, : d // 2] | (b[:, d // 2:] << 16)
    return lax.bitcast_convert_type(w, jnp.int32).reshape(n * ROW_SUBLANES, LANES)


def _unpack(w):
    lo = lax.bitcast_convert_type(w << 16, jnp.float32)
    hi = lax.bitcast_convert_type(w & jnp.int32(-65536), jnp.float32)
    return lo, hi


STAGE_STRIDE = 136
STAGE_ROWS = ROW_SUBLANES * STAGE_STRIDE
STAGE_TOKENS = 8


def _load_table(tab_hbm, tab_vmem, sem, st_first_read):
    @pl.when(pl.program_id(0) == 0)
    def _():
        cp = pltpu.make_async_copy(tab_hbm, tab_vmem, sem)
        cp.start()
        st_first_read[...] = jnp.zeros_like(st_first_read)
        cp.wait()


def _stage_rows(i, k):
    return pl.ds(i * STAGE_ROWS + k, ROW_SUBLANES, stride=STAGE_STRIDE)


def _gather_rows(idx_ref, t, i, tab_vmem, st_ref):
    for k in range(PEER_SEL):
        e = idx_ref[t, k]
        st_ref[_stage_rows(i, k), :] = tab_vmem[pl.ds(pl.multiple_of(e, ROW_SUBLANES), ROW_SUBLANES), :]


def _staged_matrix(st_ref, i):
    r0 = i * STAGE_ROWS
    w = jnp.concatenate([st_ref[r0 + s * STAGE_STRIDE:r0 + s * STAGE_STRIDE + PEER_SEL, :] for s in range(ROW_SUBLANES)],
                        axis=1)
    lo, hi = _unpack(w)
    return jnp.concatenate([lo, hi], axis=1).astype(jnp.bfloat16)


def _mxu_lhs(row):
    hi, lo = _split_bf16(row)
    return jnp.concatenate([hi, lo, jnp.zeros((SUBLANES - 2, row.shape[1]), jnp.bfloat16)], axis=0)


def _staged_pipeline(idx_ref, tab_vmem, st_a, st_b, tb, consume, out_ref):
    n = STAGE_TOKENS

    def half(t_gather, st_fill, t_mxu, st_read):
        rows = []
        for i in range(n):
            _gather_rows(idx_ref, t_gather + i, i, tab_vmem, st_fill)
            rows.append(consume(st_read, i, t_mxu + i))
        out_ref[pl.ds(t_mxu, n), :] = jnp.concatenate(rows, axis=0)

    def two_groups(j, carry):
        t0 = pl.multiple_of(j * 2 * n, 2 * n)
        tp = pl.multiple_of(jnp.maximum(t0 - n, 0), n)
        half(t0, st_a, tp, st_b)
        half(t0 + n, st_b, t0, st_a)
        return carry

    lax.fori_loop(0, tb // (2 * n), two_groups, 0)
    out_ref[tb - n:tb, :] = jnp.concatenate([consume(st_b, i, tb - n + i) for i in range(n)], axis=0)


def _peer_u_body(idx_ref, x_ref, gate_ref, tab_hbm, act_ref, tab_vmem, st_a, st_b, sem, *, tb):
    _load_table(tab_hbm, tab_vmem, sem, st_b)

    def dot_products(st_ref, i, t):
        r = lax.dot_general(_mxu_lhs(x_ref[pl.ds(t, 1), :]), _staged_matrix(st_ref, i), (((1,), (1,)), ((), ())),
                            preferred_element_type=jnp.float32)
        return r[0:1] + r[1:2]

    _staged_pipeline(idx_ref, tab_vmem, st_a, st_b, tb, dot_products, act_ref)
    hp = act_ref[...]
    act_ref[...] = 0.5 * hp * (1.0 + lax.erf(hp * (2.0 ** -0.5))) * gate_ref[...]


def peer_u(idx, x, gate, tab, *, tb=256):
    t, d = x.shape
    stage = pltpu.VMEM((STAGE_TOKENS * STAGE_ROWS, LANES), jnp.int32)
    return pl.pallas_call(
        functools.partial(_peer_u_body, tb=tb),
        grid=(t // tb,),
        in_specs=[
            pl.BlockSpec((tb, PEER_SEL), lambda i: (i, 0), memory_space=pltpu.SMEM),
            pl.BlockSpec((tb, d), lambda i: (i, 0)),
            pl.BlockSpec((tb, PEER_SEL), lambda i: (i, 0)),
            pl.BlockSpec(memory_space=pl.ANY),
        ],
        out_specs=pl.BlockSpec((tb, PEER_SEL), lambda i: (i, 0)),
        out_shape=jax.ShapeDtypeStruct((t, PEER_SEL), jnp.float32),
        scratch_shapes=[pltpu.VMEM(tab.shape, jnp.int32), stage, stage, pltpu.SemaphoreType.DMA],
        compiler_params=pltpu.CompilerParams(dimension_semantics=("arbitrary",), vmem_limit_bytes=VMEM_LIMIT_BYTES),
        name="peer_u",
    )(idx, x, gate, tab)


def _peer_v_body(idx_ref, act_ref, h_ref, nw_ref, tab_hbm, o_ref, tab_vmem, st_a, st_b, sem, *, tb, final_norm):
    _load_table(tab_hbm, tab_vmem, sem, st_b)

    def weighted_sum(st_ref, i, t):
        r = jnp.dot(_mxu_lhs(act_ref[pl.ds(t, 1), :]), _staged_matrix(st_ref, i), preferred_element_type=jnp.float32)
        return r[0:1] + r[1:2]

    _staged_pipeline(idx_ref, tab_vmem, st_a, st_b, tb, weighted_sum, o_ref)
    y = h_ref[...] + o_ref[...]
    if final_norm:
        y = y * lax.rsqrt(jnp.mean(y * y, axis=-1, keepdims=True) + NORM_EPS) * nw_ref[...]
    o_ref[...] = y


def peer_v(idx, act, tab, h, norm_w, *, final_norm, tb=256):
    t, d = h.shape
    stage = pltpu.VMEM((STAGE_TOKENS * STAGE_ROWS, LANES), jnp.int32)
    return pl.pallas_call(
        functools.partial(_peer_v_body, tb=tb, final_norm=final_norm),
        grid=(t // tb,),
        in_specs=[
            pl.BlockSpec((tb, PEER_SEL), lambda i: (i, 0), memory_space=pltpu.SMEM),
            pl.BlockSpec((tb, PEER_SEL), lambda i: (i, 0)),
            pl.BlockSpec((tb, d), lambda i: (i, 0)),
            pl.BlockSpec((1, d), lambda i: (0, 0)),
            pl.BlockSpec(memory_space=pl.ANY),
        ],
        out_specs=pl.BlockSpec((tb, d), lambda i: (i, 0)),
        out_shape=jax.ShapeDtypeStruct((t, d), jnp.float32),
        scratch_shapes=[pltpu.VMEM(tab.shape, jnp.int32), stage, stage, pltpu.SemaphoreType.DMA],
        compiler_params=pltpu.CompilerParams(dimension_semantics=("arbitrary",), vmem_limit_bytes=VMEM_LIMIT_BYTES),
        name="peer_v",
    )(idx, act, h, norm_w.reshape(1, d), tab)


def kernel(x, norm_mix_w, w_in, shift_mu, attn_sinks, decay_w0, decay_w2, iclr_a0, iclr_a2,
           gate_g2, k_k, k_a, r_k, ln_x_w, ln_x_b, proj_attn, proj_rwkv, w_out, norm_ffn_w,
           peer_wq, peer_subkeys, peer_u_tab, peer_v_tab, norm_final_w):
    b, s, d = x.shape
    t = b * s
    h = x.reshape(t, d)
    for l in range(DEPTH):
        w_lay, mu_lay = layout_in_proj(w_in[l], shift_mu[l])
        qkv, zr, gates = in_proj(h, norm_mix_w[l], w_lay)
        y_attn = swa_attention(qkv, attn_sinks[l], b, s)
        r, dec, k, v, kk, nb, g, bonus = rwkv_prep(zr, mu_lay, decay_w0[l], decay_w2[l], iclr_a0[l], iclr_a2[l],
                                                   gate_g2[l], k_k[l], k_a[l], r_k[l].reshape(-1), b, s)
        to_chains = lambda a: jnp.swapaxes(a.reshape(s, b * RWKV_HEADS, RWKV_HEAD), 1, 2)
        y = rwkv_scan(to_chains(r), to_chains(dec), to_chains(k), to_chains(v), to_chains(kk), to_chains(nb))
        y = jnp.swapaxes(y, 1, 2).reshape(s, b * RWKV_DIM)
        h = merge_out(h, y, g, bonus, y_attn, gates, ln_x_w[l], ln_x_b[l], proj_attn[l], proj_rwkv[l], w_out[l])
        xn2, idx, gate = peer_route(h, norm_ffn_w[l], peer_wq[l], peer_subkeys[l])
        act = peer_u(idx, xn2, gate, pack_table(peer_u_tab[l]))
        h = peer_v(idx, act, pack_table(peer_v_tab[l]), h, norm_final_w, final_norm=(l + 1 == DEPTH))
    return h.reshape(b, s, d)
```

```python
import functools

import jax, jax.numpy as jnp
from jax import lax
from jax.experimental import pallas as pl
from jax.experimental.pallas import tpu as pltpu

D_MODEL = 1024
DEPTH = 1

HEAD_DIM = 64
ATTN_HEADS = 8
ATTN_KV_HEADS = 2
ATTN_GROUP = ATTN_HEADS // ATTN_KV_HEADS
ATTN_DIM = ATTN_HEADS * HEAD_DIM
KV_DIM = ATTN_KV_HEADS * HEAD_DIM
WINDOW = 128
BLOCK = WINDOW
ROPE_THETA = 10000.0

RWKV_HEADS = 8
RWKV_HEAD = 64
RWKV_DIM = RWKV_HEADS * RWKV_HEAD
DECAY_LORA = 64
ICLR_LORA = 64
GATE_LORA = 160
RWKV_GN_EPS = 64e-5

PEER_HEADS = 8
N_KEYS = 128
PEER_HALF = 64
PEER_TOPK = 16
PEER_SEL = PEER_HEADS * PEER_TOPK

NORM_EPS = 1e-5
NEG_INF = -1e30

LANES = 128
SUBLANES = 8
VMEM_LIMIT_BYTES = 56 * 1024 * 1024

ROW_WORDS = D_MODEL // 2
ROW_SUBLANES = ROW_WORDS // LANES
NEG_FILL = float("-inf")
HI = lax.Precision.HIGHEST
RWKV_VGROUPS = RWKV_HEAD // SUBLANES

QKV_COLS = ATTN_DIM + 2 * KV_DIM
LORA_PAD = (LANES, LANES, 2 * LANES)
RWKV_COLS = 3 * RWKV_DIM + sum(LORA_PAD)
GATE_COLS = 2 * D_MODEL


def _pad_cols(w, width):
    return jnp.pad(w, ((0, 0), (0, width - w.shape[1])))


def _pad_rows(w, height):
    return jnp.pad(w, ((0, height - w.shape[0]), (0, 0)))


def layout_in_proj(w_in, shift_mu):
    o1 = QKV_COLS
    o_r = o1 + 3 * RWKV_DIM
    o_w = o_r + DECAY_LORA
    o_a = o_w + ICLR_LORA
    o_g = o_a + GATE_LORA
    parts = [w_in[:, :o_r], _pad_cols(w_in[:, o_r:o_w], LORA_PAD[0]), _pad_cols(w_in[:, o_w:o_a], LORA_PAD[1]),
             _pad_cols(w_in[:, o_a:o_g], LORA_PAD[2]), w_in[:, o_g:]]
    mu = shift_mu.reshape(1, -1)
    m_r = 3 * RWKV_DIM
    mu_parts = [mu[:, :m_r], _pad_cols(mu[:, m_r:m_r + DECAY_LORA], LORA_PAD[0]),
                _pad_cols(mu[:, m_r + DECAY_LORA:m_r + DECAY_LORA + ICLR_LORA], LORA_PAD[1]),
                _pad_cols(mu[:, m_r + DECAY_LORA + ICLR_LORA:], LORA_PAD[2])]
    return jnp.concatenate(parts, axis=1), jnp.concatenate(mu_parts, axis=1)


def _in_proj_body(x_ref, nw_ref, w_ref, qkv_ref, zr_ref, gate_ref):
    xf = x_ref[...]
    xn = xf * lax.rsqrt(jnp.mean(xf * xf, axis=-1, keepdims=True) + NORM_EPS) * nw_ref[...]
    z = jnp.dot(xn.astype(jnp.bfloat16), w_ref[...], preferred_element_type=jnp.float32)
    qkv_ref[...] = z[:, :QKV_COLS]
    zr_ref[...] = z[:, QKV_COLS:QKV_COLS + RWKV_COLS]
    gate_ref[...] = jax.nn.sigmoid(z[:, QKV_COLS + RWKV_COLS:]).astype(jnp.bfloat16)


def in_proj(x, norm_w, w_lay, *, tm=256):
    t, d = x.shape
    n = w_lay.shape[1]
    assert t % tm == 0 and d == D_MODEL
    return pl.pallas_call(
        _in_proj_body,
        grid=(t // tm,),
        in_specs=[pl.BlockSpec((tm, d), lambda i: (i, 0)), pl.BlockSpec((1, d), lambda i: (0, 0)),
                  pl.BlockSpec((d, n), lambda i: (0, 0))],
        out_specs=[pl.BlockSpec((tm, QKV_COLS), lambda i: (i, 0)), pl.BlockSpec((tm, RWKV_COLS), lambda i: (i, 0)),
                   pl.BlockSpec((tm, GATE_COLS), lambda i: (i, 0))],
        out_shape=[jax.ShapeDtypeStruct((t, QKV_COLS), jnp.float32), jax.ShapeDtypeStruct((t, RWKV_COLS), jnp.float32),
                   jax.ShapeDtypeStruct((t, GATE_COLS), jnp.bfloat16)],
        compiler_params=pltpu.CompilerParams(dimension_semantics=("arbitrary",), vmem_limit_bytes=VMEM_LIMIT_BYTES),
        name="in_proj",
    )(x, norm_w.reshape(1, d), w_lay.astype(jnp.bfloat16))


def rope_tables(s):
    inv = ROPE_THETA ** (-jnp.arange(0, HEAD_DIM, 2, dtype=jnp.float32) / HEAD_DIM)
    ang = jnp.arange(s, dtype=jnp.float32)[:, None] * inv[None, :]
    cos = jnp.concatenate([jnp.cos(ang), jnp.cos(ang)], axis=-1)
    sin = jnp.concatenate([-jnp.sin(ang), jnp.sin(ang)], axis=-1)
    reps = LANES // HEAD_DIM
    return jnp.tile(cos, (1, reps)), jnp.tile(sin, (1, reps))


def _rope(x, cos, sin, first_half):
    rot = jnp.where(first_half, pltpu.roll(x, LANES - HEAD_DIM // 2, 1), pltpu.roll(x, HEAD_DIM // 2, 1))
    return x * cos + rot * sin


def _attn_body(cur_ref, prev_ref, cos_ref, sin_ref, cosp_ref, sinp_ref, sink_ref, o_ref):
    n = pl.program_id(1)
    lane = lax.broadcasted_iota(jnp.int32, (BLOCK, LANES), 1)
    first_half = (lane % HEAD_DIM) < HEAD_DIM // 2
    low = lane < HEAD_DIM
    cos, sin = cos_ref[...], sin_ref[...]
    cur = cur_ref[0]
    prev = prev_ref[0]
    k_cur = _rope(cur[:, ATTN_DIM:ATTN_DIM + KV_DIM], cos, sin, first_half)
    k_prev = _rope(prev[:, ATTN_DIM:ATTN_DIM + KV_DIM], cosp_ref[...], sinp_ref[...], first_half)
    keys = jnp.concatenate([k_prev, k_cur], axis=0)
    vals = jnp.concatenate([prev[:, ATTN_DIM + KV_DIM:], cur[:, ATTN_DIM + KV_DIM:]], axis=0)
    keys_sw = pltpu.roll(keys, HEAD_DIM, 1)
    vals_sw = pltpu.roll(vals, HEAD_DIM, 1)
    low2 = lax.broadcasted_iota(jnp.int32, (2 * BLOCK, LANES), 1) < HEAD_DIM
    qi = lax.broadcasted_iota(jnp.int32, (BLOCK, 2 * BLOCK), 0) + BLOCK
    kj = lax.broadcasted_iota(jnp.int32, (BLOCK, 2 * BLOCK), 1)
    diff = qi - kj
    mask = (diff >= 0) & (diff < WINDOW) & ((kj >= BLOCK) | (n > 0))
    scale = HEAD_DIM ** -0.5
    outs = []
    for j in range(ATTN_HEADS // 2):
        g = (2 * j) // ATTN_GROUP
        kg = (jnp.where(low2, keys, keys_sw) if g == 0 else jnp.where(low2, keys_sw, keys)).astype(jnp.bfloat16)
        vg = (jnp.where(low2, vals, vals_sw) if g == 0 else jnp.where(low2, vals_sw, vals)).astype(jnp.bfloat16)
        qg = _rope(cur[:, j * LANES:(j + 1) * LANES], cos, sin, first_half) * scale
        halves = []
        for half in range(2):
            qh = jnp.where(low if half == 0 else ~low, qg, 0.0).astype(jnp.bfloat16)
            logits = lax.dot_general(qh, kg, (((1,), (1,)), ((), ())), preferred_element_type=jnp.float32)
            logits = jnp.where(mask, logits, NEG_INF)
            sink = sink_ref[2 * j + half]
            m = jnp.maximum(jnp.max(logits, axis=-1, keepdims=True), sink)
            p = jnp.exp(logits - m)
            denom = jnp.sum(p, axis=-1, keepdims=True) + jnp.exp(sink - m)
            o = jnp.dot(p.astype(jnp.bfloat16), vg, preferred_element_type=jnp.float32)
            halves.append(o / denom)
        outs.append(jnp.where(low, halves[0], halves[1]))
    o_ref[0] = jnp.concatenate(outs, axis=1).astype(o_ref.dtype)


def swa_attention(qkv, sinks, b, s):
    assert s % BLOCK == 0
    nb = s // BLOCK
    cos, sin = rope_tables(s)
    qkv3 = qkv.reshape(b, s, QKV_COLS)
    cur_spec = pl.BlockSpec((1, BLOCK, QKV_COLS), lambda bi, n: (bi, n, 0))
    prev_spec = pl.BlockSpec((1, BLOCK, QKV_COLS), lambda bi, n: (bi, jnp.maximum(n - 1, 0), 0))
    tab = pl.BlockSpec((BLOCK, LANES), lambda bi, n: (n, 0))
    tabp = pl.BlockSpec((BLOCK, LANES), lambda bi, n: (jnp.maximum(n - 1, 0), 0))
    out = pl.pallas_call(
        _attn_body,
        grid=(b, nb),
        in_specs=[cur_spec, prev_spec, tab, tab, tabp, tabp, pl.BlockSpec(memory_space=pltpu.SMEM)],
        out_specs=pl.BlockSpec((1, BLOCK, ATTN_DIM), lambda bi, n: (bi, n, 0)),
        out_shape=jax.ShapeDtypeStruct((b, s, ATTN_DIM), jnp.bfloat16),
        compiler_params=pltpu.CompilerParams(dimension_semantics=("arbitrary", "arbitrary"),
                                             vmem_limit_bytes=VMEM_LIMIT_BYTES),
        name="swa_attention",
    )(qkv3, qkv3, cos, sin, cos, sin, sinks.astype(jnp.float32))
    return out.reshape(b * s, ATTN_DIM)


def head_block_ones():
    i = jnp.arange(RWKV_DIM) // RWKV_HEAD
    return (i[:, None] == i[None, :]).astype(jnp.bfloat16)


def _softplus(x):
    return jnp.maximum(x, 0.0) + jnp.log1p(jnp.exp(-jnp.abs(x)))


def _split_bf16(x):
    hi = x.astype(jnp.bfloat16)
    return hi, (x - hi.astype(jnp.float32)).astype(jnp.bfloat16)


def _dot3(x, wh_ref, wl_ref):
    xh, xl = _split_bf16(x)
    dot = lambda a, w_ref: jnp.dot(a, w_ref[...], preferred_element_type=jnp.float32)
    return dot(xh, wh_ref) + (dot(xh, wl_ref) + dot(xl, wh_ref))


def _head_sums(x, ones_ref):
    x1, rest = _split_bf16(x)
    x2 = rest
    x3 = (x - x1.astype(jnp.float32) - x2.astype(jnp.float32)).astype(jnp.bfloat16)
    dot = lambda a: jnp.dot(a, ones_ref[...], preferred_element_type=jnp.float32)
    return dot(x1) + (dot(x2) + dot(x3))


def _rwkv_prep_body(z_ref, zp_ref, mu_ref, w0_ref, w2h_ref, w2l_ref, a0_ref, a2h_ref, a2l_ref, g2h_ref, g2l_ref,
                    kk_w_ref, ka_ref, rk_ref, ones_ref,
                    r_ref, dec_ref, k_ref, v_ref, kk_ref, nb_ref, g_ref, bonus_ref, *, ts):
    z = z_ref[0]
    row = lax.broadcasted_iota(jnp.int32, (ts, 1), 0)
    last_prev = jnp.where(pl.program_id(1) > 0, zp_ref[0][SUBLANES - 1:SUBLANES, :], 0.0)
    z_prev = jnp.where(row == 0, last_prev, pltpu.roll(z, 1, 0))
    zs = z + mu_ref[...] * (z_prev - z)
    c = RWKV_DIM
    r, k, v = zs[:, :c], zs[:, c:2 * c], zs[:, 2 * c:3 * c]
    o = 3 * c
    w_lo = zs[:, o:o + LORA_PAD[0]]
    a_lo = zs[:, o + LORA_PAD[0]:o + LORA_PAD[0] + LORA_PAD[1]]
    g_lo = zs[:, o + LORA_PAD[0] + LORA_PAD[1]:]
    w = -_softplus(-(w0_ref[...] + _dot3(jnp.tanh(w_lo), w2h_ref, w2l_ref))) - 0.5
    dec_ref[...] = jnp.exp(-jnp.exp(w))
    a = jax.nn.sigmoid(a0_ref[...] + _dot3(a_lo, a2h_ref, a2l_ref))
    g_ref[...] = _dot3(jax.nn.sigmoid(g_lo), g2h_ref, g2l_ref)
    kk = k * kk_w_ref[...]
    kk = kk / jnp.maximum(jnp.sqrt(_head_sums(kk * kk, ones_ref)), 1e-12)
    kmod = k * (1.0 + (a - 1.0) * ka_ref[...])
    r_ref[...] = r
    k_ref[...] = kmod
    v_ref[...] = v
    kk_ref[...] = kk
    nb_ref[...] = -kk * a
    bonus_ref[...] = _head_sums(r * kmod * rk_ref[...], ones_ref) * v


def rwkv_prep(zr, mu_lay, w0, w2, a0, a2, g2, k_k, k_a, r_k, b, s, *, ts=256):
    t = b * s
    zr3 = zr.reshape(b, s, RWKV_COLS)
    row = lambda x: x.reshape(1, -1)
    w2p, a2p, g2p = _pad_rows(w2, LORA_PAD[0]), _pad_rows(a2, LORA_PAD[1]), _pad_rows(g2, LORA_PAD[2])
    const = lambda shape: pl.BlockSpec(shape, lambda bi, si: (0,) * len(shape))
    assert s % ts == 0
    nblk = s // ts
    tok_spec = pl.BlockSpec((ts, RWKV_DIM), lambda bi, si: (bi * nblk + si, 0))
    time_spec = pl.BlockSpec((ts, RWKV_DIM), lambda bi, si: (si, bi))
    tok_shape = jax.ShapeDtypeStruct((t, RWKV_DIM), jnp.float32)
    time_shape = jax.ShapeDtypeStruct((s, b * RWKV_DIM), jnp.float32)
    return pl.pallas_call(
        functools.partial(_rwkv_prep_body, ts=ts),
        grid=(b, nblk),
        in_specs=[pl.BlockSpec((1, ts, RWKV_COLS), lambda bi, si: (bi, si, 0)),
                  pl.BlockSpec((1, SUBLANES, RWKV_COLS), lambda bi, si: (bi, jnp.maximum(si * (ts // SUBLANES) - 1, 0), 0)),
                  const((1, RWKV_COLS)), const((1, RWKV_DIM)), const(w2p.shape), const(w2p.shape), const((1, RWKV_DIM)),
                  const(a2p.shape), const(a2p.shape), const(g2p.shape), const(g2p.shape),
                  const((1, RWKV_DIM)), const((1, RWKV_DIM)), const((1, RWKV_DIM)), const((RWKV_DIM, RWKV_DIM))],
        out_specs=[time_spec] * 6 + [tok_spec] * 2,
        out_shape=[time_shape] * 6 + [tok_shape] * 2,
        compiler_params=pltpu.CompilerParams(dimension_semantics=("arbitrary", "arbitrary"),
                                             vmem_limit_bytes=VMEM_LIMIT_BYTES),
        name="rwkv_prep",
    )(zr3, zr3, mu_lay, row(w0), *_split_bf16(w2p), row(a0), *_split_bf16(a2p), *_split_bf16(g2p),
      row(k_k), row(k_a), row(r_k), head_block_ones())


SCAN_PARTIALS = 4


def _tree_sum(parts):
    while len(parts) > 1:
        parts = [parts[i] + parts[i + 1] for i in range(0, len(parts), 2)]
    return parts[0]


def _scan_body(r_ref, w_ref, k_ref, v_ref, kk_ref, nb_ref, y_ref, *s_refs, tc):
    @pl.when(pl.program_id(1) == 0)
    def _():
        for s_ref in s_refs:
            s_ref[...] = jnp.zeros_like(s_ref)

    def step(t, carry):
        for g, s_ref in enumerate(s_refs):
            rows = slice(g * SUBLANES, (g + 1) * SUBLANES)
            acc = [None] * SCAN_PARTIALS
            for k in range(RWKV_HEAD):
                p = s_ref[k] * kk_ref[t, pl.ds(k, 1), :]
                acc[k % SCAN_PARTIALS] = p if acc[k % SCAN_PARTIALS] is None else acc[k % SCAN_PARTIALS] + p
            sa = _tree_sum(acc)
            vv = v_ref[t, rows, :]
            yacc = [None] * SCAN_PARTIALS
            for k in range(RWKV_HEAD):
                s_new = s_ref[k] * w_ref[t, pl.ds(k, 1), :] + sa * nb_ref[t, pl.ds(k, 1), :] + vv * k_ref[t, pl.ds(k, 1), :]
                s_ref[k] = s_new
                p = s_new * r_ref[t, pl.ds(k, 1), :]
                yacc[k % SCAN_PARTIALS] = p if yacc[k % SCAN_PARTIALS] is None else yacc[k % SCAN_PARTIALS] + p
            y_ref[t, rows, :] = _tree_sum(yacc)
        return carry

    lax.fori_loop(0, tc, step, 0)


def rwkv_scan(r, w, k, v, kk, nb, *, tc=32):
    s, n, c = r.shape
    assert s % tc == 0 and c % LANES == 0 and n == RWKV_HEAD
    spec = pl.BlockSpec((tc, n, LANES), lambda ci, ti: (ti, 0, ci))
    return pl.pallas_call(
        functools.partial(_scan_body, tc=tc),
        grid=(c // LANES, s // tc),
        in_specs=[spec] * 6,
        out_specs=spec,
        out_shape=jax.ShapeDtypeStruct((s, n, c), jnp.float32),
        scratch_shapes=[pltpu.VMEM((n, SUBLANES, LANES), jnp.float32)] * RWKV_VGROUPS,
        compiler_params=pltpu.CompilerParams(dimension_semantics=("arbitrary", "arbitrary"),
                                             vmem_limit_bytes=VMEM_LIMIT_BYTES),
        name="rwkv_scan",
    )(r, w, k, v, kk, nb)


def _merge_body(x_ref, y_ref, g_ref, bonus_ref, ya_ref, gate_ref, lnw_ref, lnb_ref, ones_ref, pa_ref, pb_ref, wo_ref, h_ref):
    dot_bf = lambda x, w: jnp.dot(x.astype(jnp.bfloat16), w, preferred_element_type=jnp.float32)
    y = y_ref[...]
    inv_n = 1.0 / RWKV_HEAD
    mean = _head_sums(y, ones_ref) * inv_n
    yc = y - mean
    var = _head_sums(yc * yc, ones_ref) * inv_n
    yn = yc * lax.rsqrt(var + RWKV_GN_EPS) * lnw_ref[...] + lnb_ref[...] + bonus_ref[...]
    y_rwkv = yn * g_ref[...]
    gates = gate_ref[...].astype(jnp.float32)
    merged = gates[:, :D_MODEL] * dot_bf(ya_ref[...], pa_ref[...]) + gates[:, D_MODEL:] * dot_bf(y_rwkv, pb_ref[...])
    h_ref[...] = x_ref[...] + dot_bf(merged, wo_ref[...])


def merge_out(x, y, g, bonus, y_attn, gates, ln_w, ln_b, proj_attn, proj_rwkv, w_out, *, tm=256):
    t, d = x.shape
    s = y.shape[0]
    assert s % tm == 0 and t % s == 0
    nblk = s // tm
    row = lambda v: v.reshape(1, -1)
    tok = lambda w: pl.BlockSpec((tm, w), lambda bi, si: (bi * nblk + si, 0))
    const = lambda shape: pl.BlockSpec(shape, lambda bi, si: (0,) * len(shape))
    bf = lambda w: w.astype(jnp.bfloat16)
    return pl.pallas_call(
        _merge_body,
        grid=(t // s, nblk),
        in_specs=[tok(d), pl.BlockSpec((tm, RWKV_DIM), lambda bi, si: (si, bi)), tok(RWKV_DIM), tok(RWKV_DIM),
                  tok(ATTN_DIM), tok(GATE_COLS),
                  const((1, RWKV_DIM)), const((1, RWKV_DIM)), const((RWKV_DIM, RWKV_DIM)),
                  const(proj_attn.shape), const(proj_rwkv.shape), const(w_out.shape)],
        out_specs=tok(d),
        out_shape=jax.ShapeDtypeStruct((t, d), jnp.float32),
        compiler_params=pltpu.CompilerParams(dimension_semantics=("arbitrary", "arbitrary"),
                                             vmem_limit_bytes=VMEM_LIMIT_BYTES),
        name="merge_out",
    )(x, y, g, bonus, y_attn, gates, row(ln_w), row(ln_b), head_block_ones(), bf(proj_attn), bf(proj_rwkv), bf(w_out))


def _top16(problems):
    arrays = [p[0] for p in problems]
    for i in range(PEER_TOPK):
        for j, (_, ids, big, vals_ref, ids_ref) in enumerate(problems):
            a = arrays[j]
            m = jnp.max(a, axis=0, keepdims=True)
            r = jnp.min(jnp.where(a == m, ids, big), axis=0, keepdims=True)
            vals_ref[i:i + 1, :] = m
            ids_ref[i:i + 1, :] = r
            arrays[j] = jnp.where(ids == r, NEG_FILL, a)


def _select_rows(pos, table):
    out = jnp.zeros(pos.shape, table.dtype)
    for a in range(PEER_TOPK):
        out = jnp.where(pos == a, table[a:a + 1, :], out)
    return out


def _peer_route_body(h_ref, nw_ref, wqh_ref, wql_ref, sk_ref, xn_ref, idx_ref, gate_ref, q_ref, idx_t_ref, gate_t_ref,
                     s1_ref, i1_ref, s2_ref, i2_ref, top_ref, pos_ref, *, tm):
    xf = h_ref[...]
    xn = xf * lax.rsqrt(jnp.mean(xf * xf, axis=-1, keepdims=True) + NORM_EPS) * nw_ref[...]
    xn_ref[...] = xn
    q_ref[...] = _dot3(xn, wqh_ref, wql_ref)
    key_ids = lax.broadcasted_iota(jnp.int32, (N_KEYS, LANES), 0).astype(jnp.float32)
    sub = lax.broadcasted_iota(jnp.int32, (SUBLANES, LANES), 0)
    cand_ids = jnp.concatenate([sub, sub + SUBLANES] + [sub + a * PEER_TOPK for a in range(1, SUBLANES)]
                               + [(sub + SUBLANES) * PEER_TOPK], axis=0).astype(jnp.float32)
    ncol = tm // LANES

    def head(h, carry):
        col0 = pl.multiple_of(h * 2 * PEER_HALF, 2 * PEER_HALF)
        row0 = pl.multiple_of(h * PEER_TOPK, PEER_TOPK)
        for c in range(ncol):
            qh = q_ref[c * LANES:(c + 1) * LANES, pl.ds(col0, 2 * PEER_HALF)]
            sc = lax.dot_general(sk_ref[h], qh, (((1,), (1,)), ((), ())),
                                 preferred_element_type=jnp.float32, precision=HI)
            _top16([(sc[:N_KEYS], key_ids, float(N_KEYS), s1_ref.at[c], i1_ref.at[c]),
                    (sc[N_KEYS:], key_ids, float(N_KEYS), s2_ref.at[c], i2_ref.at[c])])
        problems = []
        for c in range(ncol):
            s1, s2 = s1_ref[c], s2_ref[c]
            cand = jnp.concatenate([s1[0:1] + s2[0:SUBLANES], s1[0:1] + s2[SUBLANES:]]
                                   + [s1[a:a + 1] + s2[0:SUBLANES] for a in range(1, SUBLANES)]
                                   + [s1[SUBLANES:] + s2[0:1]], axis=0)
            problems.append((cand, cand_ids, float(PEER_TOPK * PEER_TOPK), top_ref.at[c], pos_ref.at[c]))
        _top16(problems)
        for c in range(ncol):
            top, pos = top_ref[c], pos_ref[c].astype(jnp.int32)
            i1, i2 = i1_ref[c].astype(jnp.int32), i2_ref[c].astype(jnp.int32)
            e = _select_rows(pos // PEER_TOPK, i1) * N_KEYS + _select_rows(pos % PEER_TOPK, i2)
            p = jnp.exp(top - top[0:1, :])
            idx_t_ref[pl.ds(row0, PEER_TOPK), c * LANES:(c + 1) * LANES] = e * ROW_SUBLANES
            gate_t_ref[pl.ds(row0, PEER_TOPK), c * LANES:(c + 1) * LANES] = p / jnp.sum(p, axis=0, keepdims=True)
        return carry

    lax.fori_loop(0, PEER_HEADS, head, 0)
    for c in range(ncol):
        idx_ref[c * LANES:(c + 1) * LANES, :] = idx_t_ref[:, c * LANES:(c + 1) * LANES].T
        gate_ref[c * LANES:(c + 1) * LANES, :] = gate_t_ref[:, c * LANES:(c + 1) * LANES].T


def _block_diag_subkeys(subkeys):
    z = jnp.zeros_like(subkeys[:, 0])
    top = jnp.concatenate([subkeys[:, 0], z], axis=-1)
    bot = jnp.concatenate([z, subkeys[:, 1]], axis=-1)
    return jnp.concatenate([top, bot], axis=1)


def peer_route(h, norm_w, wq, subkeys, *, tm=1024):
    t, d = h.shape
    assert t % tm == 0 and tm % LANES == 0
    sk = _block_diag_subkeys(subkeys)
    return pl.pallas_call(
        functools.partial(_peer_route_body, tm=tm),
        grid=(t // tm,),
        in_specs=[
            pl.BlockSpec((tm, d), lambda i: (i, 0)),
            pl.BlockSpec((1, d), lambda i: (0, 0)),
            pl.BlockSpec(wq.shape, lambda i: (0, 0)),
            pl.BlockSpec(wq.shape, lambda i: (0, 0)),
            pl.BlockSpec(sk.shape, lambda i: (0, 0, 0)),
        ],
        out_specs=[pl.BlockSpec((tm, d), lambda i: (i, 0)),
                   pl.BlockSpec((tm, PEER_SEL), lambda i: (i, 0)),
                   pl.BlockSpec((tm, PEER_SEL), lambda i: (i, 0))],
        out_shape=[jax.ShapeDtypeStruct((t, d), jnp.float32),
                   jax.ShapeDtypeStruct((t, PEER_SEL), jnp.int32),
                   jax.ShapeDtypeStruct((t, PEER_SEL), jnp.float32)],
        scratch_shapes=[pltpu.VMEM((tm, d), jnp.float32), pltpu.VMEM((PEER_SEL, tm), jnp.int32),
                        pltpu.VMEM((PEER_SEL, tm), jnp.float32)]
                       + [pltpu.VMEM((tm // LANES, PEER_TOPK, LANES), jnp.float32),
                          pltpu.VMEM((tm // LANES, PEER_TOPK, LANES), jnp.float32)] * 3,
        compiler_params=pltpu.CompilerParams(dimension_semantics=("arbitrary",), vmem_limit_bytes=VMEM_LIMIT_BYTES),
        name="peer_route",
    )(h, norm_w.reshape(1, d), *_split_bf16(wq), sk)


def pack_table(tab):
    n, d = tab.shape
    b = lax.bitcast_convert_type(tab.astype(jnp.bfloat16), jnp.uint16).astype(jnp.uint32)
    w = b[:, : d // 2] | (b[:, d // 2:] << 16)
    return lax.bitcast_convert_type(w, jnp.int32).reshape(n * ROW_SUBLANES, LANES)


def _unpack(w):
    lo = lax.bitcast_convert_type(w << 16, jnp.float32)
    hi = lax.bitcast_convert_type(w & jnp.int32(-65536), jnp.float32)
    return lo, hi


STAGE_STRIDE = 136
STAGE_ROWS = ROW_SUBLANES * STAGE_STRIDE
STAGE_TOKENS = 8


def _load_table(tab_hbm, tab_vmem, sem, st_first_read):
    @pl.when(pl.program_id(0) == 0)
    def _():
        cp = pltpu.make_async_copy(tab_hbm, tab_vmem, sem)
        cp.start()
        st_first_read[...] = jnp.zeros_like(st_first_read)
        cp.wait()


def _stage_rows(i, k):
    return pl.ds(i * STAGE_ROWS + k, ROW_SUBLANES, stride=STAGE_STRIDE)


def _gather_rows(idx_ref, t, i, tab_vmem, st_ref):
    for k in range(PEER_SEL):
        e = idx_ref[t, k]
        st_ref[_stage_rows(i, k), :] = tab_vmem[pl.ds(pl.multiple_of(e, ROW_SUBLANES), ROW_SUBLANES), :]


def _staged_matrix(st_ref, i):
    r0 = i * STAGE_ROWS
    w = jnp.concatenate([st_ref[r0 + s * STAGE_STRIDE:r0 + s * STAGE_STRIDE + PEER_SEL, :] for s in range(ROW_SUBLANES)],
                        axis=1)
    lo, hi = _unpack(w)
    return jnp.concatenate([lo, hi], axis=1).astype(jnp.bfloat16)


def _mxu_lhs(row):
    hi, lo = _split_bf16(row)
    return jnp.concatenate([hi, lo, jnp.zeros((SUBLANES - 2, row.shape[1]), jnp.bfloat16)], axis=0)


def _staged_pipeline(idx_ref, tab_vmem, st_a, st_b, tb, consume, out_ref):
    n = STAGE_TOKENS

    def half(t_gather, st_fill, t_mxu, st_read):
        rows = []
        for i in range(n):
            _gather_rows(idx_ref, t_gather + i, i, tab_vmem, st_fill)
            rows.append(consume(st_read, i, t_mxu + i))
        out_ref[pl.ds(t_mxu, n), :] = jnp.concatenate(rows, axis=0)

    def two_groups(j, carry):
        t0 = pl.multiple_of(j * 2 * n, 2 * n)
        tp = pl.multiple_of(jnp.maximum(t0 - n, 0), n)
        half(t0, st_a, tp, st_b)
        half(t0 + n, st_b, t0, st_a)
        return carry

    lax.fori_loop(0, tb // (2 * n), two_groups, 0)
    out_ref[tb - n:tb, :] = jnp.concatenate([consume(st_b, i, tb - n + i) for i in range(n)], axis=0)


def _peer_u_body(idx_ref, x_ref, gate_ref, tab_hbm, act_ref, tab_vmem, st_a, st_b, sem, *, tb):
    _load_table(tab_hbm, tab_vmem, sem, st_b)

    def dot_products(st_ref, i, t):
        r = lax.dot_general(_mxu_lhs(x_ref[pl.ds(t, 1), :]), _staged_matrix(st_ref, i), (((1,), (1,)), ((), ())),
                            preferred_element_type=jnp.float32)
        return r[0:1] + r[1:2]

    _staged_pipeline(idx_ref, tab_vmem, st_a, st_b, tb, dot_products, act_ref)
    hp = act_ref[...]
    act_ref[...] = 0.5 * hp * (1.0 + lax.erf(hp * (2.0 ** -0.5))) * gate_ref[...]


def peer_u(idx, x, gate, tab, *, tb=256):
    t, d = x.shape
    assert t % tb == 0 and tb % (2 * STAGE_TOKENS) == 0 and d == D_MODEL
    stage = pltpu.VMEM((STAGE_TOKENS * STAGE_ROWS, LANES), jnp.int32)
    return pl.pallas_call(
        functools.partial(_peer_u_body, tb=tb),
        grid=(t // tb,),
        in_specs=[
            pl.BlockSpec((tb, PEER_SEL), lambda i: (i, 0), memory_space=pltpu.SMEM),
            pl.BlockSpec((tb, d), lambda i: (i, 0)),
            pl.BlockSpec((tb, PEER_SEL), lambda i: (i, 0)),
            pl.BlockSpec(memory_space=pl.ANY),
        ],
        out_specs=pl.BlockSpec((tb, PEER_SEL), lambda i: (i, 0)),
        out_shape=jax.ShapeDtypeStruct((t, PEER_SEL), jnp.float32),
        scratch_shapes=[pltpu.VMEM(tab.shape, jnp.int32), stage, stage, pltpu.SemaphoreType.DMA],
        compiler_params=pltpu.CompilerParams(dimension_semantics=("arbitrary",), vmem_limit_bytes=VMEM_LIMIT_BYTES),
        name="peer_u",
    )(idx, x, gate, tab)


def _peer_v_body(idx_ref, act_ref, h_ref, nw_ref, tab_hbm, o_ref, tab_vmem, st_a, st_b, sem, *, tb, final_norm):
    _load_table(tab_hbm, tab_vmem, sem, st_b)

    def weighted_sum(st_ref, i, t):
        r = jnp.dot(_mxu_lhs(act_ref[pl.ds(t, 1), :]), _staged_matrix(st_ref, i), preferred_element_type=jnp.float32)
        return r[0:1] + r[1:2]

    _staged_pipeline(idx_ref, tab_vmem, st_a, st_b, tb, weighted_sum, o_ref)
    y = h_ref[...] + o_ref[...]
    if final_norm:
        y = y * lax.rsqrt(jnp.mean(y * y, axis=-1, keepdims=True) + NORM_EPS) * nw_ref[...]
    o_ref[...] = y


def peer_v(idx, act, tab, h, norm_w, *, final_norm, tb=256):
    t, d = h.shape
    assert t % tb == 0 and tb % (2 * STAGE_TOKENS) == 0 and d == D_MODEL
    stage = pltpu.VMEM((STAGE_TOKENS * STAGE_ROWS, LANES), jnp.int32)
    return pl.pallas_call(
        functools.partial(_peer_v_body, tb=tb, final_norm=final_norm),
        grid=(t // tb,),
        in_specs=[
            pl.BlockSpec((tb, PEER_SEL), lambda i: (i, 0), memory_space=pltpu.SMEM),
            pl.BlockSpec((tb, PEER_SEL), lambda i: (i, 0)),
            pl.BlockSpec((tb, d), lambda i: (i, 0)),
            pl.BlockSpec((1, d), lambda i: (0, 0)),
            pl.BlockSpec(memory_space=pl.ANY),
        ],
        out_specs=pl.BlockSpec((tb, d), lambda i: (i, 0)),
        out_shape=jax.ShapeDtypeStruct((t, d), jnp.float32),
        scratch_shapes=[pltpu.VMEM(tab.shape, jnp.int32), stage, stage, pltpu.SemaphoreType.DMA],
        compiler_params=pltpu.CompilerParams(dimension_semantics=("arbitrary",), vmem_limit_bytes=VMEM_LIMIT_BYTES),
        name="peer_v",
    )(idx, act, h, norm_w.reshape(1, d), tab)


def kernel(x, norm_mix_w, w_in, shift_mu, attn_sinks, decay_w0, decay_w2, iclr_a0, iclr_a2,
           gate_g2, k_k, k_a, r_k, ln_x_w, ln_x_b, proj_attn, proj_rwkv, w_out, norm_ffn_w,
           peer_wq, peer_subkeys, peer_u_tab, peer_v_tab, norm_final_w):
    b, s, d = x.shape
    t = b * s
    h = x.reshape(t, d)
    for l in range(DEPTH):
        w_lay, mu_lay = layout_in_proj(w_in[l], shift_mu[l])
        qkv, zr, gates = in_proj(h, norm_mix_w[l], w_lay)
        y_attn = swa_attention(qkv, attn_sinks[l], b, s)
        r, dec, k, v, kk, nb, g, bonus = rwkv_prep(zr, mu_lay, decay_w0[l], decay_w2[l], iclr_a0[l], iclr_a2[l],
                                                   gate_g2[l], k_k[l], k_a[l], r_k[l].reshape(-1), b, s)
        to_chains = lambda a: jnp.swapaxes(a.reshape(s, b * RWKV_HEADS, RWKV_HEAD), 1, 2)
        y = rwkv_scan(to_chains(r), to_chains(dec), to_chains(k), to_chains(v), to_chains(kk), to_chains(nb))
        y = jnp.swapaxes(y, 1, 2).reshape(s, b * RWKV_DIM)
        h = merge_out(h, y, g, bonus, y_attn, gates, ln_x_w[l], ln_x_b[l], proj_attn[l], proj_rwkv[l], w_out[l])
        xn2, idx, gate = peer_route(h, norm_ffn_w[l], peer_wq[l], peer_subkeys[l])
        act = peer_u(idx, xn2, gate, pack_table(peer_u_tab[l]))
        h = peer_v(idx, act, pack_table(peer_v_tab[l]), h, norm_final_w, final_norm=(l + 1 == DEPTH))
    return h.reshape(b, s, d)
```

```python
import functools

import jax, jax.numpy as jnp
from jax import lax
from jax.experimental import pallas as pl
from jax.experimental.pallas import tpu as pltpu

D_MODEL = 1024
DEPTH = 1

HEAD_DIM = 64
ATTN_HEADS = 8
ATTN_KV_HEADS = 2
ATTN_GROUP = ATTN_HEADS // ATTN_KV_HEADS
ATTN_DIM = ATTN_HEADS * HEAD_DIM
KV_DIM = ATTN_KV_HEADS * HEAD_DIM
WINDOW = 128
BLOCK = WINDOW
ROPE_THETA = 10000.0

RWKV_HEADS = 8
RWKV_HEAD = 64
RWKV_DIM = RWKV_HEADS * RWKV_HEAD
DECAY_LORA = 64
ICLR_LORA = 64
GATE_LORA = 160
RWKV_GN_EPS = 64e-5

PEER_HEADS = 8
N_KEYS = 128
PEER_HALF = 64
PEER_TOPK = 16
PEER_SEL = PEER_HEADS * PEER_TOPK

NORM_EPS = 1e-5
NEG_INF = -1e30

LANES = 128
SUBLANES = 8
VMEM_LIMIT_BYTES = 56 * 1024 * 1024

ROW_WORDS = D_MODEL // 2
ROW_SUBLANES = ROW_WORDS // LANES
NEG_FILL = float("-inf")
HI = lax.Precision.HIGHEST
RWKV_VGROUPS = RWKV_HEAD // SUBLANES

QKV_COLS = ATTN_DIM + 2 * KV_DIM
LORA_PAD = (LANES, LANES, 2 * LANES)
RWKV_COLS = 3 * RWKV_DIM + sum(LORA_PAD)
GATE_COLS = 2 * D_MODEL


def _pad_cols(w, width):
    return jnp.pad(w, ((0, 0), (0, width - w.shape[1])))


def _pad_rows(w, height):
    return jnp.pad(w, ((0, height - w.shape[0]), (0, 0)))


def layout_in_proj(w_in, shift_mu):
    o1 = QKV_COLS
    o_r = o1 + 3 * RWKV_DIM
    o_w = o_r + DECAY_LORA
    o_a = o_w + ICLR_LORA
    o_g = o_a + GATE_LORA
    parts = [w_in[:, :o_r], _pad_cols(w_in[:, o_r:o_w], LORA_PAD[0]), _pad_cols(w_in[:, o_w:o_a], LORA_PAD[1]),
             _pad_cols(w_in[:, o_a:o_g], LORA_PAD[2]), w_in[:, o_g:]]
    mu = shift_mu.reshape(1, -1)
    m_r = 3 * RWKV_DIM
    mu_parts = [mu[:, :m_r], _pad_cols(mu[:, m_r:m_r + DECAY_LORA], LORA_PAD[0]),
                _pad_cols(mu[:, m_r + DECAY_LORA:m_r + DECAY_LORA + ICLR_LORA], LORA_PAD[1]),
                _pad_cols(mu[:, m_r + DECAY_LORA + ICLR_LORA:], LORA_PAD[2])]
    return jnp.concatenate(parts, axis=1), jnp.concatenate(mu_parts, axis=1)


def _in_proj_body(x_ref, nw_ref, w_ref, qkv_ref, zr_ref, gate_ref):
    xf = x_ref[...]
    xn = xf * lax.rsqrt(jnp.mean(xf * xf, axis=-1, keepdims=True) + NORM_EPS) * nw_ref[...]
    z = jnp.dot(xn.astype(jnp.bfloat16), w_ref[...], preferred_element_type=jnp.float32)
    qkv_ref[...] = z[:, :QKV_COLS]
    zr_ref[...] = z[:, QKV_COLS:QKV_COLS + RWKV_COLS]
    gate_ref[...] = jax.nn.sigmoid(z[:, QKV_COLS + RWKV_COLS:]).astype(jnp.bfloat16)


def in_proj(x, norm_w, w_lay, *, tm=256):
    t, d = x.shape
    n = w_lay.shape[1]
    assert t % tm == 0 and d == D_MODEL
    return pl.pallas_call(
        _in_proj_body,
        grid=(t // tm,),
        in_specs=[pl.BlockSpec((tm, d), lambda i: (i, 0)), pl.BlockSpec((1, d), lambda i: (0, 0)),
                  pl.BlockSpec((d, n), lambda i: (0, 0))],
        out_specs=[pl.BlockSpec((tm, QKV_COLS), lambda i: (i, 0)), pl.BlockSpec((tm, RWKV_COLS), lambda i: (i, 0)),
                   pl.BlockSpec((tm, GATE_COLS), lambda i: (i, 0))],
        out_shape=[jax.ShapeDtypeStruct((t, QKV_COLS), jnp.float32), jax.ShapeDtypeStruct((t, RWKV_COLS), jnp.float32),
                   jax.ShapeDtypeStruct((t, GATE_COLS), jnp.bfloat16)],
        compiler_params=pltpu.CompilerParams(dimension_semantics=("arbitrary",), vmem_limit_bytes=VMEM_LIMIT_BYTES),
        name="in_proj",
    )(x, norm_w.reshape(1, d), w_lay.astype(jnp.bfloat16))


def rope_tables(s):
    inv = ROPE_THETA ** (-jnp.arange(0, HEAD_DIM, 2, dtype=jnp.float32) / HEAD_DIM)
    ang = jnp.arange(s, dtype=jnp.float32)[:, None] * inv[None, :]
    cos = jnp.concatenate([jnp.cos(ang), jnp.cos(ang)], axis=-1)
    sin = jnp.concatenate([-jnp.sin(ang), jnp.sin(ang)], axis=-1)
    reps = LANES // HEAD_DIM
    return jnp.tile(cos, (1, reps)), jnp.tile(sin, (1, reps))


def _rope(x, cos, sin, first_half):
    rot = jnp.where(first_half, pltpu.roll(x, LANES - HEAD_DIM // 2, 1), pltpu.roll(x, HEAD_DIM // 2, 1))
    return x * cos + rot * sin


def _attn_body(cur_ref, prev_ref, cos_ref, sin_ref, cosp_ref, sinp_ref, sink_ref, o_ref):
    n = pl.program_id(1)
    lane = lax.broadcasted_iota(jnp.int32, (BLOCK, LANES), 1)
    first_half = (lane % HEAD_DIM) < HEAD_DIM // 2
    low = lane < HEAD_DIM
    cos, sin = cos_ref[...], sin_ref[...]
    cur = cur_ref[0]
    prev = prev_ref[0]
    k_cur = _rope(cur[:, ATTN_DIM:ATTN_DIM + KV_DIM], cos, sin, first_half)
    k_prev = _rope(prev[:, ATTN_DIM:ATTN_DIM + KV_DIM], cosp_ref[...], sinp_ref[...], first_half)
    keys = jnp.concatenate([k_prev, k_cur], axis=0)
    vals = jnp.concatenate([prev[:, ATTN_DIM + KV_DIM:], cur[:, ATTN_DIM + KV_DIM:]], axis=0)
    keys_sw = pltpu.roll(keys, HEAD_DIM, 1)
    vals_sw = pltpu.roll(vals, HEAD_DIM, 1)
    low2 = lax.broadcasted_iota(jnp.int32, (2 * BLOCK, LANES), 1) < HEAD_DIM
    qi = lax.broadcasted_iota(jnp.int32, (BLOCK, 2 * BLOCK), 0) + BLOCK
    kj = lax.broadcasted_iota(jnp.int32, (BLOCK, 2 * BLOCK), 1)
    diff = qi - kj
    mask = (diff >= 0) & (diff < WINDOW) & ((kj >= BLOCK) | (n > 0))
    scale = HEAD_DIM ** -0.5
    outs = []
    for j in range(ATTN_HEADS // 2):
        g = (2 * j) // ATTN_GROUP
        kg = (jnp.where(low2, keys, keys_sw) if g == 0 else jnp.where(low2, keys_sw, keys)).astype(jnp.bfloat16)
        vg = (jnp.where(low2, vals, vals_sw) if g == 0 else jnp.where(low2, vals_sw, vals)).astype(jnp.bfloat16)
        qg = _rope(cur[:, j * LANES:(j + 1) * LANES], cos, sin, first_half) * scale
        halves = []
        for half in range(2):
            qh = jnp.where(low if half == 0 else ~low, qg, 0.0).astype(jnp.bfloat16)
            logits = lax.dot_general(qh, kg, (((1,), (1,)), ((), ())), preferred_element_type=jnp.float32)
            logits = jnp.where(mask, logits, NEG_INF)
            sink = sink_ref[2 * j + half]
            m = jnp.maximum(jnp.max(logits, axis=-1, keepdims=True), sink)
            p = jnp.exp(logits - m)
            denom = jnp.sum(p, axis=-1, keepdims=True) + jnp.exp(sink - m)
            o = jnp.dot(p.astype(jnp.bfloat16), vg, preferred_element_type=jnp.float32)
            halves.append(o / denom)
        outs.append(jnp.where(low, halves[0], halves[1]))
    o_ref[0] = jnp.concatenate(outs, axis=1).astype(o_ref.dtype)


def swa_attention(qkv, sinks, b, s):
    assert s % BLOCK == 0
    nb = s // BLOCK
    cos, sin = rope_tables(s)
    qkv3 = qkv.reshape(b, s, QKV_COLS)
    cur_spec = pl.BlockSpec((1, BLOCK, QKV_COLS), lambda bi, n: (bi, n, 0))
    prev_spec = pl.BlockSpec((1, BLOCK, QKV_COLS), lambda bi, n: (bi, jnp.maximum(n - 1, 0), 0))
    tab = pl.BlockSpec((BLOCK, LANES), lambda bi, n: (n, 0))
    tabp = pl.BlockSpec((BLOCK, LANES), lambda bi, n: (jnp.maximum(n - 1, 0), 0))
    out = pl.pallas_call(
        _attn_body,
        grid=(b, nb),
        in_specs=[cur_spec, prev_spec, tab, tab, tabp, tabp, pl.BlockSpec(memory_space=pltpu.SMEM)],
        out_specs=pl.BlockSpec((1, BLOCK, ATTN_DIM), lambda bi, n: (bi, n, 0)),
        out_shape=jax.ShapeDtypeStruct((b, s, ATTN_DIM), jnp.bfloat16),
        compiler_params=pltpu.CompilerParams(dimension_semantics=("arbitrary", "arbitrary"),
                                             vmem_limit_bytes=VMEM_LIMIT_BYTES),
        name="swa_attention",
    )(qkv3, qkv3, cos, sin, cos, sin, sinks.astype(jnp.float32))
    return out.reshape(b * s, ATTN_DIM)


def head_block_ones():
    i = jnp.arange(RWKV_DIM) // RWKV_HEAD
    return (i[:, None] == i[None, :]).astype(jnp.bfloat16)


def _softplus(x):
    return jnp.maximum(x, 0.0) + jnp.log1p(jnp.exp(-jnp.abs(x)))


def _split_bf16(x):
    hi = x.astype(jnp.bfloat16)
    return hi, (x - hi.astype(jnp.float32)).astype(jnp.bfloat16)


def _dot3(x, wh_ref, wl_ref):
    xh, xl = _split_bf16(x)
    dot = lambda a, w_ref: jnp.dot(a, w_ref[...], preferred_element_type=jnp.float32)
    return dot(xh, wh_ref) + (dot(xh, wl_ref) + dot(xl, wh_ref))


def _head_sums(x, ones_ref):
    x1, rest = _split_bf16(x)
    x2 = rest
    x3 = (x - x1.astype(jnp.float32) - x2.astype(jnp.float32)).astype(jnp.bfloat16)
    dot = lambda a: jnp.dot(a, ones_ref[...], preferred_element_type=jnp.float32)
    return dot(x1) + (dot(x2) + dot(x3))


def _rwkv_prep_body(z_ref, zp_ref, mu_ref, w0_ref, w2h_ref, w2l_ref, a0_ref, a2h_ref, a2l_ref, g2h_ref, g2l_ref,
                    kk_w_ref, ka_ref, rk_ref, ones_ref,
                    r_ref, dec_ref, k_ref, v_ref, kk_ref, nb_ref, g_ref, bonus_ref, *, ts):
    z = z_ref[0]
    row = lax.broadcasted_iota(jnp.int32, (ts, 1), 0)
    last_prev = jnp.where(pl.program_id(1) > 0, zp_ref[0][SUBLANES - 1:SUBLANES, :], 0.0)
    z_prev = jnp.where(row == 0, last_prev, pltpu.roll(z, 1, 0))
    zs = z + mu_ref[...] * (z_prev - z)
    c = RWKV_DIM
    r, k, v = zs[:, :c], zs[:, c:2 * c], zs[:, 2 * c:3 * c]
    o = 3 * c
    w_lo = zs[:, o:o + LORA_PAD[0]]
    a_lo = zs[:, o + LORA_PAD[0]:o + LORA_PAD[0] + LORA_PAD[1]]
    g_lo = zs[:, o + LORA_PAD[0] + LORA_PAD[1]:]
    w = -_softplus(-(w0_ref[...] + _dot3(jnp.tanh(w_lo), w2h_ref, w2l_ref))) - 0.5
    dec_ref[...] = jnp.exp(-jnp.exp(w))
    a = jax.nn.sigmoid(a0_ref[...] + _dot3(a_lo, a2h_ref, a2l_ref))
    g_ref[...] = _dot3(jax.nn.sigmoid(g_lo), g2h_ref, g2l_ref)
    kk = k * kk_w_ref[...]
    kk = kk / jnp.maximum(jnp.sqrt(_head_sums(kk * kk, ones_ref)), 1e-12)
    kmod = k * (1.0 + (a - 1.0) * ka_ref[...])
    r_ref[...] = r
    k_ref[...] = kmod
    v_ref[...] = v
    kk_ref[...] = kk
    nb_ref[...] = -kk * a
    bonus_ref[...] = _head_sums(r * kmod * rk_ref[...], ones_ref) * v


def rwkv_prep(zr, mu_lay, w0, w2, a0, a2, g2, k_k, k_a, r_k, b, s, *, ts=256):
    t = b * s
    zr3 = zr.reshape(b, s, RWKV_COLS)
    row = lambda x: x.reshape(1, -1)
    w2p, a2p, g2p = _pad_rows(w2, LORA_PAD[0]), _pad_rows(a2, LORA_PAD[1]), _pad_rows(g2, LORA_PAD[2])
    const = lambda shape: pl.BlockSpec(shape, lambda bi, si: (0,) * len(shape))
    assert s % ts == 0
    nblk = s // ts
    tok_spec = pl.BlockSpec((ts, RWKV_DIM), lambda bi, si: (bi * nblk + si, 0))
    time_spec = pl.BlockSpec((ts, RWKV_DIM), lambda bi, si: (si, bi))
    tok_shape = jax.ShapeDtypeStruct((t, RWKV_DIM), jnp.float32)
    time_shape = jax.ShapeDtypeStruct((s, b * RWKV_DIM), jnp.float32)
    return pl.pallas_call(
        functools.partial(_rwkv_prep_body, ts=ts),
        grid=(b, nblk),
        in_specs=[pl.BlockSpec((1, ts, RWKV_COLS), lambda bi, si: (bi, si, 0)),
                  pl.BlockSpec((1, SUBLANES, RWKV_COLS), lambda bi, si: (bi, jnp.maximum(si * (ts // SUBLANES) - 1, 0), 0)),
                  const((1, RWKV_COLS)), const((1, RWKV_DIM)), const(w2p.shape), const(w2p.shape), const((1, RWKV_DIM)),
                  const(a2p.shape), const(a2p.shape), const(g2p.shape), const(g2p.shape),
                  const((1, RWKV_DIM)), const((1, RWKV_DIM)), const((1, RWKV_DIM)), const((RWKV_DIM, RWKV_DIM))],
        out_specs=[time_spec] * 6 + [tok_spec] * 2,
        out_shape=[time_shape] * 6 + [tok_shape] * 2,
        compiler_params=pltpu.CompilerParams(dimension_semantics=("arbitrary", "arbitrary"),
                                             vmem_limit_bytes=VMEM_LIMIT_BYTES),
        name="rwkv_prep",
    )(zr3, zr3, mu_lay, row(w0), *_split_bf16(w2p), row(a0), *_split_bf16(a2p), *_split_bf16(g2p),
      row(k_k), row(k_a), row(r_k), head_block_ones())


SCAN_PARTIALS = 4


def _tree_sum(parts):
    while len(parts) > 1:
        parts = [parts[i] + parts[i + 1] for i in range(0, len(parts), 2)]
    return parts[0]


def _scan_body(r_ref, w_ref, k_ref, v_ref, kk_ref, nb_ref, y_ref, *s_refs, tc):
    @pl.when(pl.program_id(1) == 0)
    def _():
        for s_ref in s_refs:
            s_ref[...] = jnp.zeros_like(s_ref)

    def step(t, carry):
        for g, s_ref in enumerate(s_refs):
            rows = slice(g * SUBLANES, (g + 1) * SUBLANES)
            acc = [None] * SCAN_PARTIALS
            for k in range(RWKV_HEAD):
                p = s_ref[k] * kk_ref[t, pl.ds(k, 1), :]
                acc[k % SCAN_PARTIALS] = p if acc[k % SCAN_PARTIALS] is None else acc[k % SCAN_PARTIALS] + p
            sa = _tree_sum(acc)
            vv = v_ref[t, rows, :]
            yacc = [None] * SCAN_PARTIALS
            for k in range(RWKV_HEAD):
                s_new = s_ref[k] * w_ref[t, pl.ds(k, 1), :] + sa * nb_ref[t, pl.ds(k, 1), :] + vv * k_ref[t, pl.ds(k, 1), :]
                s_ref[k] = s_new
                p = s_new * r_ref[t, pl.ds(k, 1), :]
                yacc[k % SCAN_PARTIALS] = p if yacc[k % SCAN_PARTIALS] is None else yacc[k % SCAN_PARTIALS] + p
            y_ref[t, rows, :] = _tree_sum(yacc)
        return carry

    lax.fori_loop(0, tc, step, 0)


def rwkv_scan(r, w, k, v, kk, nb, *, tc=32):
    s, n, c = r.shape
    assert s % tc == 0 and c % LANES == 0 and n == RWKV_HEAD
    spec = pl.BlockSpec((tc, n, LANES), lambda ci, ti: (ti, 0, ci))
    return pl.pallas_call(
        functools.partial(_scan_body, tc=tc),
        grid=(c // LANES, s // tc),
        in_specs=[spec] * 6,
        out_specs=spec,
        out_shape=jax.ShapeDtypeStruct((s, n, c), jnp.float32),
        scratch_shapes=[pltpu.VMEM((n, SUBLANES, LANES), jnp.float32)] * RWKV_VGROUPS,
        compiler_params=pltpu.CompilerParams(dimension_semantics=("arbitrary", "arbitrary"),
                                             vmem_limit_bytes=VMEM_LIMIT_BYTES),
        name="rwkv_scan",
    )(r, w, k, v, kk, nb)


def _merge_body(x_ref, y_ref, g_ref, bonus_ref, ya_ref, gate_ref, lnw_ref, lnb_ref, ones_ref, pa_ref, pb_ref, wo_ref, h_ref):
    dot_bf = lambda x, w: jnp.dot(x.astype(jnp.bfloat16), w, preferred_element_type=jnp.float32)
    y = y_ref[...]
    inv_n = 1.0 / RWKV_HEAD
    mean = _head_sums(y, ones_ref) * inv_n
    yc = y - mean
    var = _head_sums(yc * yc, ones_ref) * inv_n
    yn = yc * lax.rsqrt(var + RWKV_GN_EPS) * lnw_ref[...] + lnb_ref[...] + bonus_ref[...]
    y_rwkv = yn * g_ref[...]
    gates = gate_ref[...].astype(jnp.float32)
    merged = gates[:, :D_MODEL] * dot_bf(ya_ref[...], pa_ref[...]) + gates[:, D_MODEL:] * dot_bf(y_rwkv, pb_ref[...])
    h_ref[...] = x_ref[...] + dot_bf(merged, wo_ref[...])


def merge_out(x, y, g, bonus, y_attn, gates, ln_w, ln_b, proj_attn, proj_rwkv, w_out, *, tm=256):
    t, d = x.shape
    s = y.shape[0]
    assert s % tm == 0 and t % s == 0
    nblk = s // tm
    row = lambda v: v.reshape(1, -1)
    tok = lambda w: pl.BlockSpec((tm, w), lambda bi, si: (bi * nblk + si, 0))
    const = lambda shape: pl.BlockSpec(shape, lambda bi, si: (0,) * len(shape))
    bf = lambda w: w.astype(jnp.bfloat16)
    return pl.pallas_call(
        _merge_body,
        grid=(t // s, nblk),
        in_specs=[tok(d), pl.BlockSpec((tm, RWKV_DIM), lambda bi, si: (si, bi)), tok(RWKV_DIM), tok(RWKV_DIM),
                  tok(ATTN_DIM), tok(GATE_COLS),
                  const((1, RWKV_DIM)), const((1, RWKV_DIM)), const((RWKV_DIM, RWKV_DIM)),
                  const(proj_attn.shape), const(proj_rwkv.shape), const(w_out.shape)],
        out_specs=tok(d),
        out_shape=jax.ShapeDtypeStruct((t, d), jnp.float32),
        compiler_params=pltpu.CompilerParams(dimension_semantics=("arbitrary", "arbitrary"),
                                             vmem_limit_bytes=VMEM_LIMIT_BYTES),
        name="merge_out",
    )(x, y, g, bonus, y_attn, gates, row(ln_w), row(ln_b), head_block_ones(), bf(proj_attn), bf(proj_rwkv), bf(w_out))


def _top16(problems):
    arrays = [p[0] for p in problems]
    for i in range(PEER_TOPK):
        for j, (_, ids, big, vals_ref, ids_ref) in enumerate(problems):
            a = arrays[j]
            m = jnp.max(a, axis=0, keepdims=True)
            r = jnp.min(jnp.where(a == m, ids, big), axis=0, keepdims=True)
            vals_ref[i:i + 1, :] = m
            ids_ref[i:i + 1, :] = r
            arrays[j] = jnp.where(ids == r, NEG_FILL, a)


def _select_rows(pos, table):
    out = jnp.zeros(pos.shape, table.dtype)
    for a in range(PEER_TOPK):
        out = jnp.where(pos == a, table[a:a + 1, :], out)
    return out


def _peer_route_body(h_ref, nw_ref, wqh_ref, wql_ref, sk_ref, xn_ref, idx_ref, gate_ref, q_ref, idx_t_ref, gate_t_ref,
                     s1_ref, i1_ref, s2_ref, i2_ref, top_ref, pos_ref, *, tm):
    xf = h_ref[...]
    xn = xf * lax.rsqrt(jnp.mean(xf * xf, axis=-1, keepdims=True) + NORM_EPS) * nw_ref[...]
    xn_ref[...] = xn
    q_ref[...] = _dot3(xn, wqh_ref, wql_ref)
    key_ids = lax.broadcasted_iota(jnp.int32, (N_KEYS, LANES), 0).astype(jnp.float32)
    sub = lax.broadcasted_iota(jnp.int32, (SUBLANES, LANES), 0)
    cand_ids = jnp.concatenate([sub, sub + SUBLANES] + [sub + a * PEER_TOPK for a in range(1, SUBLANES)]
                               + [(sub + SUBLANES) * PEER_TOPK], axis=0).astype(jnp.float32)
    ncol = tm // LANES

    def head(h, carry):
        col0 = pl.multiple_of(h * 2 * PEER_HALF, 2 * PEER_HALF)
        row0 = pl.multiple_of(h * PEER_TOPK, PEER_TOPK)
        for c in range(ncol):
            qh = q_ref[c * LANES:(c + 1) * LANES, pl.ds(col0, 2 * PEER_HALF)]
            sc = lax.dot_general(sk_ref[h], qh, (((1,), (1,)), ((), ())),
                                 preferred_element_type=jnp.float32, precision=HI)
            _top16([(sc[:N_KEYS], key_ids, float(N_KEYS), s1_ref.at[c], i1_ref.at[c]),
                    (sc[N_KEYS:], key_ids, float(N_KEYS), s2_ref.at[c], i2_ref.at[c])])
        problems = []
        for c in range(ncol):
            s1, s2 = s1_ref[c], s2_ref[c]
            cand = jnp.concatenate([s1[0:1] + s2[0:SUBLANES], s1[0:1] + s2[SUBLANES:]]
                                   + [s1[a:a + 1] + s2[0:SUBLANES] for a in range(1, SUBLANES)]
                                   + [s1[SUBLANES:] + s2[0:1]], axis=0)
            problems.append((cand, cand_ids, float(PEER_TOPK * PEER_TOPK), top_ref.at[c], pos_ref.at[c]))
        _top16(problems)
        for c in range(ncol):
            top, pos = top_ref[c], pos_ref[c].astype(jnp.int32)
            i1, i2 = i1_ref[c].astype(jnp.int32), i2_ref[c].astype(jnp.int32)
            e = _select_rows(pos // PEER_TOPK, i1) * N_KEYS + _select_rows(pos % PEER_TOPK, i2)
            p = jnp.exp(top - top[0:1, :])
            idx_t_ref[pl.ds(row0, PEER_TOPK), c * LANES:(c + 1) * LANES] = e * ROW_SUBLANES
            gate_t_ref[pl.ds(row0, PEER_TOPK), c * LANES:(c + 1) * LANES] = p / jnp.sum(p, axis=0, keepdims=True)
        return carry

    lax.fori_loop(0, PEER_HEADS, head, 0)
    for c in range(ncol):
        idx_ref[c * LANES:(c + 1) * LANES, :] = idx_t_ref[:, c * LANES:(c + 1) * LANES].T
        gate_ref[c * LANES:(c + 1) * LANES, :] = gate_t_ref[:, c * LANES:(c + 1) * LANES].T


def _block_diag_subkeys(subkeys):
    z = jnp.zeros_like(subkeys[:, 0])
    top = jnp.concatenate([subkeys[:, 0], z], axis=-1)
    bot = jnp.concatenate([z, subkeys[:, 1]], axis=-1)
    return jnp.concatenate([top, bot], axis=1)


def peer_route(h, norm_w, wq, subkeys, *, tm=1024):
    t, d = h.shape
    assert t % tm == 0 and tm % LANES == 0
    sk = _block_diag_subkeys(subkeys)
    return pl.pallas_call(
        functools.partial(_peer_route_body, tm=tm),
        grid=(t // tm,),
        in_specs=[
            pl.BlockSpec((tm, d), lambda i: (i, 0)),
            pl.BlockSpec((1, d), lambda i: (0, 0)),
            pl.BlockSpec(wq.shape, lambda i: (0, 0)),
            pl.BlockSpec(wq.shape, lambda i: (0, 0)),
            pl.BlockSpec(sk.shape, lambda i: (0, 0, 0)),
        ],
        out_specs=[pl.BlockSpec((tm, d), lambda i: (i, 0)),
                   pl.BlockSpec((tm, PEER_SEL), lambda i: (i, 0)),
                   pl.BlockSpec((tm, PEER_SEL), lambda i: (i, 0))],
        out_shape=[jax.ShapeDtypeStruct((t, d), jnp.float32),
                   jax.ShapeDtypeStruct((t, PEER_SEL), jnp.int32),
                   jax.ShapeDtypeStruct((t, PEER_SEL), jnp.float32)],
        scratch_shapes=[pltpu.VMEM((tm, d), jnp.float32), pltpu.VMEM((PEER_SEL, tm), jnp.int32),
                        pltpu.VMEM((PEER_SEL, tm), jnp.float32)]
                       + [pltpu.VMEM((tm // LANES, PEER_TOPK, LANES), jnp.float32),
                          pltpu.VMEM((tm // LANES, PEER_TOPK, LANES), jnp.float32)] * 3,
        compiler_params=pltpu.CompilerParams(dimension_semantics=("arbitrary",), vmem_limit_bytes=VMEM_LIMIT_BYTES),
        name="peer_route",
    )(h, norm_w.reshape(1, d), *_split_bf16(wq), sk)


def _pack_body(x_ref, o_ref, *, te):
    bits = lax.bitcast_convert_type(x_ref[...].astype(jnp.bfloat16).astype(jnp.float32), jnp.int32)
    w = lax.shift_right_logical(bits[:, :ROW_WORDS], 16) | (bits[:, ROW_WORDS:] & jnp.int32(-65536))
    for s in range(ROW_SUBLANES):
        o_ref[pl.ds(s, te, stride=ROW_SUBLANES), :] = w[:, s * LANES:(s + 1) * LANES]


def pack_table(tab, *, te=512):
    n, d = tab.shape
    assert n % te == 0 and d == D_MODEL
    return pl.pallas_call(
        functools.partial(_pack_body, te=te),
        grid=(n // te,),
        in_specs=[pl.BlockSpec((te, d), lambda i: (i, 0))],
        out_specs=pl.BlockSpec((te * ROW_SUBLANES, LANES), lambda i: (i, 0)),
        out_shape=jax.ShapeDtypeStruct((n * ROW_SUBLANES, LANES), jnp.int32),
        compiler_params=pltpu.CompilerParams(dimension_semantics=("arbitrary",)),
        name="pack_table",
    )(tab)


def _unpack(w):
    lo = lax.bitcast_convert_type(w << 16, jnp.float32)
    hi = lax.bitcast_convert_type(w & jnp.int32(-65536), jnp.float32)
    return lo, hi


STAGE_STRIDE = 136
STAGE_ROWS = ROW_SUBLANES * STAGE_STRIDE
STAGE_TOKENS = 8


def _load_table(tab_hbm, tab_vmem, sem, st_first_read):
    @pl.when(pl.program_id(0) == 0)
    def _():
        cp = pltpu.make_async_copy(tab_hbm, tab_vmem, sem)
        cp.start()
        st_first_read[...] = jnp.zeros_like(st_first_read)
        cp.wait()


def _stage_rows(i, k):
    return pl.ds(i * STAGE_ROWS + k, ROW_SUBLANES, stride=STAGE_STRIDE)


def _gather_rows(idx_ref, t, i, tab_vmem, st_ref):
    for k in range(PEER_SEL):
        e = idx_ref[t, k]
        st_ref[_stage_rows(i, k), :] = tab_vmem[pl.ds(pl.multiple_of(e, ROW_SUBLANES), ROW_SUBLANES), :]


def _staged_matrix(st_ref, i):
    r0 = i * STAGE_ROWS
    w = jnp.concatenate([st_ref[r0 + s * STAGE_STRIDE:r0 + s * STAGE_STRIDE + PEER_SEL, :] for s in range(ROW_SUBLANES)],
                        axis=1)
    lo, hi = _unpack(w)
    return jnp.concatenate([lo, hi], axis=1).astype(jnp.bfloat16)


def _mxu_lhs(row):
    hi, lo = _split_bf16(row)
    return jnp.concatenate([hi, lo, jnp.zeros((SUBLANES - 2, row.shape[1]), jnp.bfloat16)], axis=0)


def _staged_pipeline(idx_ref, tab_vmem, st_a, st_b, tb, consume, out_ref):
    n = STAGE_TOKENS

    def half(t_gather, st_fill, t_mxu, st_read):
        rows = []
        for i in range(n):
            _gather_rows(idx_ref, t_gather + i, i, tab_vmem, st_fill)
            rows.append(consume(st_read, i, t_mxu + i))
        out_ref[pl.ds(t_mxu, n), :] = jnp.concatenate(rows, axis=0)

    def two_groups(j, carry):
        t0 = pl.multiple_of(j * 2 * n, 2 * n)
        tp = pl.multiple_of(jnp.maximum(t0 - n, 0), n)
        half(t0, st_a, tp, st_b)
        half(t0 + n, st_b, t0, st_a)
        return carry

    lax.fori_loop(0, tb // (2 * n), two_groups, 0)
    out_ref[tb - n:tb, :] = jnp.concatenate([consume(st_b, i, tb - n + i) for i in range(n)], axis=0)


def _peer_u_body(idx_ref, x_ref, gate_ref, tab_hbm, act_ref, tab_vmem, st_a, st_b, sem, *, tb):
    _load_table(tab_hbm, tab_vmem, sem, st_b)

    def dot_products(st_ref, i, t):
        r = lax.dot_general(_mxu_lhs(x_ref[pl.ds(t, 1), :]), _staged_matrix(st_ref, i), (((1,), (1,)), ((), ())),
                            preferred_element_type=jnp.float32)
        return r[0:1] + r[1:2]

    _staged_pipeline(idx_ref, tab_vmem, st_a, st_b, tb, dot_products, act_ref)
    hp = act_ref[...]
    act_ref[...] = 0.5 * hp * (1.0 + lax.erf(hp * (2.0 ** -0.5))) * gate_ref[...]


def peer_u(idx, x, gate, tab, *, tb=256):
    t, d = x.shape
    assert t % tb == 0 and tb % (2 * STAGE_TOKENS) == 0 and d == D_MODEL
    stage = pltpu.VMEM((STAGE_TOKENS * STAGE_ROWS, LANES), jnp.int32)
    return pl.pallas_call(
        functools.partial(_peer_u_body, tb=tb),
        grid=(t // tb,),
        in_specs=[
            pl.BlockSpec((tb, PEER_SEL), lambda i: (i, 0), memory_space=pltpu.SMEM),
            pl.BlockSpec((tb, d), lambda i: (i, 0)),
            pl.BlockSpec((tb, PEER_SEL), lambda i: (i, 0)),
            pl.BlockSpec(memory_space=pl.ANY),
        ],
        out_specs=pl.BlockSpec((tb, PEER_SEL), lambda i: (i, 0)),
        out_shape=jax.ShapeDtypeStruct((t, PEER_SEL), jnp.float32),
        scratch_shapes=[pltpu.VMEM(tab.shape, jnp.int32), stage, stage, pltpu.SemaphoreType.DMA],
        compiler_params=pltpu.CompilerParams(dimension_semantics=("arbitrary",), vmem_limit_bytes=VMEM_LIMIT_BYTES),
        name="peer_u",
    )(idx, x, gate, tab)


def _peer_v_body(idx_ref, act_ref, h_ref, nw_ref, tab_hbm, o_ref, tab_vmem, st_a, st_b, sem, *, tb, final_norm):
    _load_table(tab_hbm, tab_vmem, sem, st_b)

    def weighted_sum(st_ref, i, t):
        r = jnp.dot(_mxu_lhs(act_ref[pl.ds(t, 1), :]), _staged_matrix(st_ref, i), preferred_element_type=jnp.float32)
        return r[0:1] + r[1:2]

    _staged_pipeline(idx_ref, tab_vmem, st_a, st_b, tb, weighted_sum, o_ref)
    y = h_ref[...] + o_ref[...]
    if final_norm:
        y = y * lax.rsqrt(jnp.mean(y * y, axis=-1, keepdims=True) + NORM_EPS) * nw_ref[...]
    o_ref[...] = y


def peer_v(idx, act, tab, h, norm_w, *, final_norm, tb=256):
    t, d = h.shape
    assert t % tb == 0 and tb % (2 * STAGE_TOKENS) == 0 and d == D_MODEL
    stage = pltpu.VMEM((STAGE_TOKENS * STAGE_ROWS, LANES), jnp.int32)
    return pl.pallas_call(
        functools.partial(_peer_v_body, tb=tb, final_norm=final_norm),
        grid=(t // tb,),
        in_specs=[
            pl.BlockSpec((tb, PEER_SEL), lambda i: (i, 0), memory_space=pltpu.SMEM),
            pl.BlockSpec((tb, PEER_SEL), lambda i: (i, 0)),
            pl.BlockSpec((tb, d), lambda i: (i, 0)),
            pl.BlockSpec((1, d), lambda i: (0, 0)),
            pl.BlockSpec(memory_space=pl.ANY),
        ],
        out_specs=pl.BlockSpec((tb, d), lambda i: (i, 0)),
        out_shape=jax.ShapeDtypeStruct((t, d), jnp.float32),
        scratch_shapes=[pltpu.VMEM(tab.shape, jnp.int32), stage, stage, pltpu.SemaphoreType.DMA],
        compiler_params=pltpu.CompilerParams(dimension_semantics=("arbitrary",), vmem_limit_bytes=VMEM_LIMIT_BYTES),
        name="peer_v",
    )(idx, act, h, norm_w.reshape(1, d), tab)


def kernel(x, norm_mix_w, w_in, shift_mu, attn_sinks, decay_w0, decay_w2, iclr_a0, iclr_a2,
           gate_g2, k_k, k_a, r_k, ln_x_w, ln_x_b, proj_attn, proj_rwkv, w_out, norm_ffn_w,
           peer_wq, peer_subkeys, peer_u_tab, peer_v_tab, norm_final_w):
    b, s, d = x.shape
    t = b * s
    h = x.reshape(t, d)
    for l in range(DEPTH):
        w_lay, mu_lay = layout_in_proj(w_in[l], shift_mu[l])
        qkv, zr, gates = in_proj(h, norm_mix_w[l], w_lay)
        y_attn = swa_attention(qkv, attn_sinks[l], b, s)
        r, dec, k, v, kk, nb, g, bonus = rwkv_prep(zr, mu_lay, decay_w0[l], decay_w2[l], iclr_a0[l], iclr_a2[l],
                                                   gate_g2[l], k_k[l], k_a[l], r_k[l].reshape(-1), b, s)
        to_chains = lambda a: jnp.swapaxes(a.reshape(s, b * RWKV_HEADS, RWKV_HEAD), 1, 2)
        y = rwkv_scan(to_chains(r), to_chains(dec), to_chains(k), to_chains(v), to_chains(kk), to_chains(nb))
        y = jnp.swapaxes(y, 1, 2).reshape(s, b * RWKV_DIM)
        h = merge_out(h, y, g, bonus, y_attn, gates, ln_x_w[l], ln_x_b[l], proj_attn[l], proj_rwkv[l], w_out[l])
        xn2, idx, gate = peer_route(h, norm_ffn_w[l], peer_wq[l], peer_subkeys[l])
        act = peer_u(idx, xn2, gate, pack_table(peer_u_tab[l]))
        h = peer_v(idx, act, pack_table(peer_v_tab[l]), h, norm_final_w, final_norm=(l + 1 == DEPTH))
    return h.reshape(b, s, d)
```

```python
import functools

import jax, jax.numpy as jnp
from jax import lax
from jax.experimental import pallas as pl
from jax.experimental.pallas import tpu as pltpu

D_MODEL = 1024
DEPTH = 1

HEAD_DIM = 64
ATTN_HEADS = 8
ATTN_KV_HEADS = 2
ATTN_GROUP = ATTN_HEADS // ATTN_KV_HEADS
ATTN_DIM = ATTN_HEADS * HEAD_DIM
KV_DIM = ATTN_KV_HEADS * HEAD_DIM
WINDOW = 128
BLOCK = WINDOW
ROPE_THETA = 10000.0

RWKV_HEADS = 8
RWKV_HEAD = 64
RWKV_DIM = RWKV_HEADS * RWKV_HEAD
DECAY_LORA = 64
ICLR_LORA = 64
GATE_LORA = 160
RWKV_GN_EPS = 64e-5

PEER_HEADS = 8
N_KEYS = 128
PEER_HALF = 64
PEER_TOPK = 16
PEER_SEL = PEER_HEADS * PEER_TOPK

NORM_EPS = 1e-5
NEG_INF = -1e30

LANES = 128
SUBLANES = 8
VMEM_LIMIT_BYTES = 56 * 1024 * 1024

ROW_WORDS = D_MODEL // 2
ROW_SUBLANES = ROW_WORDS // LANES
NEG_FILL = float("-inf")
HI = lax.Precision.HIGHEST
RWKV_VGROUPS = RWKV_HEAD // SUBLANES

QKV_COLS = ATTN_DIM + 2 * KV_DIM
LORA_PAD = (LANES, LANES, 2 * LANES)
RWKV_COLS = 3 * RWKV_DIM + sum(LORA_PAD)
GATE_COLS = 2 * D_MODEL


def _pad_cols(w, width):
    return jnp.pad(w, ((0, 0), (0, width - w.shape[1])))


def _pad_rows(w, height):
    return jnp.pad(w, ((0, height - w.shape[0]), (0, 0)))


def layout_in_proj(w_in, shift_mu):
    o1 = QKV_COLS
    o_r = o1 + 3 * RWKV_DIM
    o_w = o_r + DECAY_LORA
    o_a = o_w + ICLR_LORA
    o_g = o_a + GATE_LORA
    parts = [w_in[:, :o_r], _pad_cols(w_in[:, o_r:o_w], LORA_PAD[0]), _pad_cols(w_in[:, o_w:o_a], LORA_PAD[1]),
             _pad_cols(w_in[:, o_a:o_g], LORA_PAD[2]), w_in[:, o_g:]]
    mu = shift_mu.reshape(1, -1)
    m_r = 3 * RWKV_DIM
    mu_parts = [mu[:, :m_r], _pad_cols(mu[:, m_r:m_r + DECAY_LORA], LORA_PAD[0]),
                _pad_cols(mu[:, m_r + DECAY_LORA:m_r + DECAY_LORA + ICLR_LORA], LORA_PAD[1]),
                _pad_cols(mu[:, m_r + DECAY_LORA + ICLR_LORA:], LORA_PAD[2])]
    return jnp.concatenate(parts, axis=1), jnp.concatenate(mu_parts, axis=1)


def _in_proj_body(x_ref, nw_ref, w_ref, qkv_ref, zr_ref, gate_ref):
    xf = x_ref[...]
    xn = xf * lax.rsqrt(jnp.mean(xf * xf, axis=-1, keepdims=True) + NORM_EPS) * nw_ref[...]
    z = jnp.dot(xn.astype(jnp.bfloat16), w_ref[...], preferred_element_type=jnp.float32)
    qkv_ref[...] = z[:, :QKV_COLS]
    zr_ref[...] = z[:, QKV_COLS:QKV_COLS + RWKV_COLS]
    gate_ref[...] = jax.nn.sigmoid(z[:, QKV_COLS + RWKV_COLS:]).astype(jnp.bfloat16)


def in_proj(x, norm_w, w_lay, *, tm=512):
    t, d = x.shape
    n = w_lay.shape[1]
    assert t % tm == 0 and d == D_MODEL
    return pl.pallas_call(
        _in_proj_body,
        grid=(t // tm,),
        in_specs=[pl.BlockSpec((tm, d), lambda i: (i, 0)), pl.BlockSpec((1, d), lambda i: (0, 0)),
                  pl.BlockSpec((d, n), lambda i: (0, 0))],
        out_specs=[pl.BlockSpec((tm, QKV_COLS), lambda i: (i, 0)), pl.BlockSpec((tm, RWKV_COLS), lambda i: (i, 0)),
                   pl.BlockSpec((tm, GATE_COLS), lambda i: (i, 0))],
        out_shape=[jax.ShapeDtypeStruct((t, QKV_COLS), jnp.float32), jax.ShapeDtypeStruct((t, RWKV_COLS), jnp.float32),
                   jax.ShapeDtypeStruct((t, GATE_COLS), jnp.bfloat16)],
        compiler_params=pltpu.CompilerParams(dimension_semantics=("arbitrary",), vmem_limit_bytes=VMEM_LIMIT_BYTES),
        name="in_proj",
    )(x, norm_w.reshape(1, d), w_lay.astype(jnp.bfloat16))


def rope_tables(s):
    inv = ROPE_THETA ** (-jnp.arange(0, HEAD_DIM, 2, dtype=jnp.float32) / HEAD_DIM)
    ang = jnp.arange(s, dtype=jnp.float32)[:, None] * inv[None, :]
    cos = jnp.concatenate([jnp.cos(ang), jnp.cos(ang)], axis=-1)
    sin = jnp.concatenate([-jnp.sin(ang), jnp.sin(ang)], axis=-1)
    reps = LANES // HEAD_DIM
    return jnp.tile(cos, (1, reps)), jnp.tile(sin, (1, reps))


def _rope(x, cos, sin, first_half):
    rot = jnp.where(first_half, pltpu.roll(x, LANES - HEAD_DIM // 2, 1), pltpu.roll(x, HEAD_DIM // 2, 1))
    return x * cos + rot * sin


def _attn_body(cur_ref, prev_ref, cos_ref, sin_ref, cosp_ref, sinp_ref, sink_ref, o_ref):
    n = pl.program_id(1)
    lane = lax.broadcasted_iota(jnp.int32, (BLOCK, LANES), 1)
    first_half = (lane % HEAD_DIM) < HEAD_DIM // 2
    low = lane < HEAD_DIM
    cos, sin = cos_ref[...], sin_ref[...]
    cur = cur_ref[0]
    prev = prev_ref[0]
    k_cur = _rope(cur[:, ATTN_DIM:ATTN_DIM + KV_DIM], cos, sin, first_half)
    k_prev = _rope(prev[:, ATTN_DIM:ATTN_DIM + KV_DIM], cosp_ref[...], sinp_ref[...], first_half)
    keys = jnp.concatenate([k_prev, k_cur], axis=0)
    vals = jnp.concatenate([prev[:, ATTN_DIM + KV_DIM:], cur[:, ATTN_DIM + KV_DIM:]], axis=0)
    keys_sw = pltpu.roll(keys, HEAD_DIM, 1)
    vals_sw = pltpu.roll(vals, HEAD_DIM, 1)
    low2 = lax.broadcasted_iota(jnp.int32, (2 * BLOCK, LANES), 1) < HEAD_DIM
    qi = lax.broadcasted_iota(jnp.int32, (BLOCK, 2 * BLOCK), 0) + BLOCK
    kj = lax.broadcasted_iota(jnp.int32, (BLOCK, 2 * BLOCK), 1)
    diff = qi - kj
    mask = (diff >= 0) & (diff < WINDOW) & ((kj >= BLOCK) | (n > 0))
    scale = HEAD_DIM ** -0.5
    outs = []
    for j in range(ATTN_HEADS // 2):
        g = (2 * j) // ATTN_GROUP
        kg = (jnp.where(low2, keys, keys_sw) if g == 0 else jnp.where(low2, keys_sw, keys)).astype(jnp.bfloat16)
        vg = (jnp.where(low2, vals, vals_sw) if g == 0 else jnp.where(low2, vals_sw, vals)).astype(jnp.bfloat16)
        qg = _rope(cur[:, j * LANES:(j + 1) * LANES], cos, sin, first_half) * scale
        halves = []
        for half in range(2):
            qh = jnp.where(low if half == 0 else ~low, qg, 0.0).astype(jnp.bfloat16)
            logits = lax.dot_general(qh, kg, (((1,), (1,)), ((), ())), preferred_element_type=jnp.float32)
            logits = jnp.where(mask, logits, NEG_INF)
            sink = sink_ref[2 * j + half]
            m = jnp.maximum(jnp.max(logits, axis=-1, keepdims=True), sink)
            p = jnp.exp(logits - m)
            denom = jnp.sum(p, axis=-1, keepdims=True) + jnp.exp(sink - m)
            o = jnp.dot(p.astype(jnp.bfloat16), vg, preferred_element_type=jnp.float32)
            halves.append(o / denom)
        outs.append(jnp.where(low, halves[0], halves[1]))
    o_ref[0] = jnp.concatenate(outs, axis=1).astype(o_ref.dtype)


def swa_attention(qkv, sinks, b, s):
    assert s % BLOCK == 0
    nb = s // BLOCK
    cos, sin = rope_tables(s)
    qkv3 = qkv.reshape(b, s, QKV_COLS)
    cur_spec = pl.BlockSpec((1, BLOCK, QKV_COLS), lambda bi, n: (bi, n, 0))
    prev_spec = pl.BlockSpec((1, BLOCK, QKV_COLS), lambda bi, n: (bi, jnp.maximum(n - 1, 0), 0))
    tab = pl.BlockSpec((BLOCK, LANES), lambda bi, n: (n, 0))
    tabp = pl.BlockSpec((BLOCK, LANES), lambda bi, n: (jnp.maximum(n - 1, 0), 0))
    out = pl.pallas_call(
        _attn_body,
        grid=(b, nb),
        in_specs=[cur_spec, prev_spec, tab, tab, tabp, tabp, pl.BlockSpec(memory_space=pltpu.SMEM)],
        out_specs=pl.BlockSpec((1, BLOCK, ATTN_DIM), lambda bi, n: (bi, n, 0)),
        out_shape=jax.ShapeDtypeStruct((b, s, ATTN_DIM), jnp.bfloat16),
        compiler_params=pltpu.CompilerParams(dimension_semantics=("arbitrary", "arbitrary"),
                                             vmem_limit_bytes=VMEM_LIMIT_BYTES),
        name="swa_attention",
    )(qkv3, qkv3, cos, sin, cos, sin, sinks.astype(jnp.float32))
    return out.reshape(b * s, ATTN_DIM)


def head_block_ones():
    i = jnp.arange(RWKV_DIM) // RWKV_HEAD
    return (i[:, None] == i[None, :]).astype(jnp.bfloat16)


def _softplus(x):
    return jnp.maximum(x, 0.0) + jnp.log1p(jnp.exp(-jnp.abs(x)))


def _split_bf16(x):
    hi = x.astype(jnp.bfloat16)
    return hi, (x - hi.astype(jnp.float32)).astype(jnp.bfloat16)


def _dot3(x, wh_ref, wl_ref):
    xh, xl = _split_bf16(x)
    dot = lambda a, w_ref: jnp.dot(a, w_ref[...], preferred_element_type=jnp.float32)
    return dot(xh, wh_ref) + (dot(xh, wl_ref) + dot(xl, wh_ref))


def _head_sums(x, ones_ref):
    x1, rest = _split_bf16(x)
    x2 = rest
    x3 = (x - x1.astype(jnp.float32) - x2.astype(jnp.float32)).astype(jnp.bfloat16)
    dot = lambda a: jnp.dot(a, ones_ref[...], preferred_element_type=jnp.float32)
    return dot(x1) + (dot(x2) + dot(x3))


def _rwkv_prep_body(z_ref, zp_ref, mu_ref, w0_ref, w2h_ref, w2l_ref, a0_ref, a2h_ref, a2l_ref, g2h_ref, g2l_ref,
                    kk_w_ref, ka_ref, rk_ref, ones_ref,
                    r_ref, dec_ref, k_ref, v_ref, kk_ref, nb_ref, g_ref, bonus_ref, *, ts):
    z = z_ref[0]
    row = lax.broadcasted_iota(jnp.int32, (ts, 1), 0)
    last_prev = jnp.where(pl.program_id(1) > 0, zp_ref[0][SUBLANES - 1:SUBLANES, :], 0.0)
    z_prev = jnp.where(row == 0, last_prev, pltpu.roll(z, 1, 0))
    zs = z + mu_ref[...] * (z_prev - z)
    c = RWKV_DIM
    r, k, v = zs[:, :c], zs[:, c:2 * c], zs[:, 2 * c:3 * c]
    o = 3 * c
    w_lo = zs[:, o:o + LORA_PAD[0]]
    a_lo = zs[:, o + LORA_PAD[0]:o + LORA_PAD[0] + LORA_PAD[1]]
    g_lo = zs[:, o + LORA_PAD[0] + LORA_PAD[1]:]
    w = -_softplus(-(w0_ref[...] + _dot3(jnp.tanh(w_lo), w2h_ref, w2l_ref))) - 0.5
    dec_ref[...] = jnp.exp(-jnp.exp(w))
    a = jax.nn.sigmoid(a0_ref[...] + _dot3(a_lo, a2h_ref, a2l_ref))
    g_ref[...] = _dot3(jax.nn.sigmoid(g_lo), g2h_ref, g2l_ref)
    kk = k * kk_w_ref[...]
    kk = kk / jnp.maximum(jnp.sqrt(_head_sums(kk * kk, ones_ref)), 1e-12)
    kmod = k * (1.0 + (a - 1.0) * ka_ref[...])
    r_ref[...] = r
    k_ref[...] = kmod
    v_ref[...] = v
    kk_ref[...] = kk
    nb_ref[...] = -kk * a
    bonus_ref[...] = _head_sums(r * kmod * rk_ref[...], ones_ref) * v


def rwkv_prep(zr, mu_lay, w0, w2, a0, a2, g2, k_k, k_a, r_k, b, s, *, ts=512):
    t = b * s
    zr3 = zr.reshape(b, s, RWKV_COLS)
    row = lambda x: x.reshape(1, -1)
    w2p, a2p, g2p = _pad_rows(w2, LORA_PAD[0]), _pad_rows(a2, LORA_PAD[1]), _pad_rows(g2, LORA_PAD[2])
    const = lambda shape: pl.BlockSpec(shape, lambda bi, si: (0,) * len(shape))
    assert s % ts == 0
    nblk = s // ts
    tok_spec = pl.BlockSpec((ts, RWKV_DIM), lambda bi, si: (bi * nblk + si, 0))
    time_spec = pl.BlockSpec((ts, RWKV_DIM), lambda bi, si: (si, bi))
    tok_shape = jax.ShapeDtypeStruct((t, RWKV_DIM), jnp.float32)
    time_shape = jax.ShapeDtypeStruct((s, b * RWKV_DIM), jnp.float32)
    return pl.pallas_call(
        functools.partial(_rwkv_prep_body, ts=ts),
        grid=(b, nblk),
        in_specs=[pl.BlockSpec((1, ts, RWKV_COLS), lambda bi, si: (bi, si, 0)),
                  pl.BlockSpec((1, SUBLANES, RWKV_COLS), lambda bi, si: (bi, jnp.maximum(si * (ts // SUBLANES) - 1, 0), 0)),
                  const((1, RWKV_COLS)), const((1, RWKV_DIM)), const(w2p.shape), const(w2p.shape), const((1, RWKV_DIM)),
                  const(a2p.shape), const(a2p.shape), const(g2p.shape), const(g2p.shape),
                  const((1, RWKV_DIM)), const((1, RWKV_DIM)), const((1, RWKV_DIM)), const((RWKV_DIM, RWKV_DIM))],
        out_specs=[time_spec] * 6 + [tok_spec] * 2,
        out_shape=[time_shape] * 6 + [tok_shape] * 2,
        compiler_params=pltpu.CompilerParams(dimension_semantics=("arbitrary", "arbitrary"),
                                             vmem_limit_bytes=VMEM_LIMIT_BYTES),
        name="rwkv_prep",
    )(zr3, zr3, mu_lay, row(w0), *_split_bf16(w2p), row(a0), *_split_bf16(a2p), *_split_bf16(g2p),
      row(k_k), row(k_a), row(r_k), head_block_ones())


SCAN_PARTIALS = 4


def _tree_sum(parts):
    while len(parts) > 1:
        parts = [parts[i] + parts[i + 1] for i in range(0, len(parts), 2)]
    return parts[0]


def _scan_body(r_ref, w_ref, k_ref, v_ref, kk_ref, nb_ref, y_ref, *s_refs, tc):
    @pl.when(pl.program_id(1) == 0)
    def _():
        for s_ref in s_refs:
            s_ref[...] = jnp.zeros_like(s_ref)

    def step(t, carry):
        for g, s_ref in enumerate(s_refs):
            rows = slice(g * SUBLANES, (g + 1) * SUBLANES)
            acc = [None] * SCAN_PARTIALS
            for k in range(RWKV_HEAD):
                p = s_ref[k] * kk_ref[t, pl.ds(k, 1), :]
                acc[k % SCAN_PARTIALS] = p if acc[k % SCAN_PARTIALS] is None else acc[k % SCAN_PARTIALS] + p
            sa = _tree_sum(acc)
            vv = v_ref[t, rows, :]
            yacc = [None] * SCAN_PARTIALS
            for k in range(RWKV_HEAD):
                s_new = s_ref[k] * w_ref[t, pl.ds(k, 1), :] + sa * nb_ref[t, pl.ds(k, 1), :] + vv * k_ref[t, pl.ds(k, 1), :]
                s_ref[k] = s_new
                p = s_new * r_ref[t, pl.ds(k, 1), :]
                yacc[k % SCAN_PARTIALS] = p if yacc[k % SCAN_PARTIALS] is None else yacc[k % SCAN_PARTIALS] + p
            y_ref[t, rows, :] = _tree_sum(yacc)
        return carry

    lax.fori_loop(0, tc, step, 0)


def rwkv_scan(r, w, k, v, kk, nb, *, tc=32):
    s, n, c = r.shape
    assert s % tc == 0 and c % LANES == 0 and n == RWKV_HEAD
    spec = pl.BlockSpec((tc, n, LANES), lambda ci, ti: (ti, 0, ci))
    return pl.pallas_call(
        functools.partial(_scan_body, tc=tc),
        grid=(c // LANES, s // tc),
        in_specs=[spec] * 6,
        out_specs=spec,
        out_shape=jax.ShapeDtypeStruct((s, n, c), jnp.float32),
        scratch_shapes=[pltpu.VMEM((n, SUBLANES, LANES), jnp.float32)] * RWKV_VGROUPS,
        compiler_params=pltpu.CompilerParams(dimension_semantics=("arbitrary", "arbitrary"),
                                             vmem_limit_bytes=VMEM_LIMIT_BYTES),
        name="rwkv_scan",
    )(r, w, k, v, kk, nb)


def _merge_body(x_ref, y_ref, g_ref, bonus_ref, ya_ref, gate_ref, lnw_ref, lnb_ref, ones_ref, pa_ref, pb_ref, wo_ref, h_ref):
    dot_bf = lambda x, w: jnp.dot(x.astype(jnp.bfloat16), w, preferred_element_type=jnp.float32)
    y = y_ref[...]
    inv_n = 1.0 / RWKV_HEAD
    mean = _head_sums(y, ones_ref) * inv_n
    yc = y - mean
    var = _head_sums(yc * yc, ones_ref) * inv_n
    yn = yc * lax.rsqrt(var + RWKV_GN_EPS) * lnw_ref[...] + lnb_ref[...] + bonus_ref[...]
    y_rwkv = yn * g_ref[...]
    gates = gate_ref[...].astype(jnp.float32)
    merged = gates[:, :D_MODEL] * dot_bf(ya_ref[...], pa_ref[...]) + gates[:, D_MODEL:] * dot_bf(y_rwkv, pb_ref[...])
    h_ref[...] = x_ref[...] + dot_bf(merged, wo_ref[...])


def merge_out(x, y, g, bonus, y_attn, gates, ln_w, ln_b, proj_attn, proj_rwkv, w_out, *, tm=512):
    t, d = x.shape
    s = y.shape[0]
    assert s % tm == 0 and t % s == 0
    nblk = s // tm
    row = lambda v: v.reshape(1, -1)
    tok = lambda w: pl.BlockSpec((tm, w), lambda bi, si: (bi * nblk + si, 0))
    const = lambda shape: pl.BlockSpec(shape, lambda bi, si: (0,) * len(shape))
    bf = lambda w: w.astype(jnp.bfloat16)
    return pl.pallas_call(
        _merge_body,
        grid=(t // s, nblk),
        in_specs=[tok(d), pl.BlockSpec((tm, RWKV_DIM), lambda bi, si: (si, bi)), tok(RWKV_DIM), tok(RWKV_DIM),
                  tok(ATTN_DIM), tok(GATE_COLS),
                  const((1, RWKV_DIM)), const((1, RWKV_DIM)), const((RWKV_DIM, RWKV_DIM)),
                  const(proj_attn.shape), const(proj_rwkv.shape), const(w_out.shape)],
        out_specs=tok(d),
        out_shape=jax.ShapeDtypeStruct((t, d), jnp.float32),
        compiler_params=pltpu.CompilerParams(dimension_semantics=("arbitrary", "arbitrary"),
                                             vmem_limit_bytes=VMEM_LIMIT_BYTES),
        name="merge_out",
    )(x, y, g, bonus, y_attn, gates, row(ln_w), row(ln_b), head_block_ones(), bf(proj_attn), bf(proj_rwkv), bf(w_out))


def _top16(problems):
    arrays = [p[0] for p in problems]
    for i in range(PEER_TOPK):
        for j, (_, ids, big, vals_ref, ids_ref) in enumerate(problems):
            a = arrays[j]
            m = jnp.max(a, axis=0, keepdims=True)
            r = jnp.min(jnp.where(a == m, ids, big), axis=0, keepdims=True)
            vals_ref[i:i + 1, :] = m
            ids_ref[i:i + 1, :] = r
            arrays[j] = jnp.where(ids == r, NEG_FILL, a)


def _select_rows(pos, table):
    out = jnp.zeros(pos.shape, table.dtype)
    for a in range(PEER_TOPK):
        out = jnp.where(pos == a, table[a:a + 1, :], out)
    return out


def _peer_route_body(h_ref, nw_ref, wqh_ref, wql_ref, sk_ref, xn_ref, idx_ref, gate_ref, q_ref, idx_t_ref, gate_t_ref,
                     s1_ref, i1_ref, s2_ref, i2_ref, top_ref, pos_ref, *, tm):
    xf = h_ref[...]
    xn = xf * lax.rsqrt(jnp.mean(xf * xf, axis=-1, keepdims=True) + NORM_EPS) * nw_ref[...]
    xn_ref[...] = xn
    q_ref[...] = _dot3(xn, wqh_ref, wql_ref)
    key_ids = lax.broadcasted_iota(jnp.int32, (N_KEYS, LANES), 0).astype(jnp.float32)
    sub = lax.broadcasted_iota(jnp.int32, (SUBLANES, LANES), 0)
    cand_ids = jnp.concatenate([sub, sub + SUBLANES] + [sub + a * PEER_TOPK for a in range(1, SUBLANES)]
                               + [(sub + SUBLANES) * PEER_TOPK], axis=0).astype(jnp.float32)
    ncol = tm // LANES

    def head(h, carry):
        col0 = pl.multiple_of(h * 2 * PEER_HALF, 2 * PEER_HALF)
        row0 = pl.multiple_of(h * PEER_TOPK, PEER_TOPK)
        for c in range(ncol):
            qh = q_ref[c * LANES:(c + 1) * LANES, pl.ds(col0, 2 * PEER_HALF)]
            sc = lax.dot_general(sk_ref[h], qh, (((1,), (1,)), ((), ())),
                                 preferred_element_type=jnp.float32, precision=HI)
            _top16([(sc[:N_KEYS], key_ids, float(N_KEYS), s1_ref.at[c], i1_ref.at[c]),
                    (sc[N_KEYS:], key_ids, float(N_KEYS), s2_ref.at[c], i2_ref.at[c])])
        problems = []
        for c in range(ncol):
            s1, s2 = s1_ref[c], s2_ref[c]
            cand = jnp.concatenate([s1[0:1] + s2[0:SUBLANES], s1[0:1] + s2[SUBLANES:]]
                                   + [s1[a:a + 1] + s2[0:SUBLANES] for a in range(1, SUBLANES)]
                                   + [s1[SUBLANES:] + s2[0:1]], axis=0)
            problems.append((cand, cand_ids, float(PEER_TOPK * PEER_TOPK), top_ref.at[c], pos_ref.at[c]))
        _top16(problems)
        for c in range(ncol):
            top, pos = top_ref[c], pos_ref[c].astype(jnp.int32)
            i1, i2 = i1_ref[c].astype(jnp.int32), i2_ref[c].astype(jnp.int32)
            e = _select_rows(pos // PEER_TOPK, i1) * N_KEYS + _select_rows(pos % PEER_TOPK, i2)
            p = jnp.exp(top - top[0:1, :])
            idx_t_ref[pl.ds(row0, PEER_TOPK), c * LANES:(c + 1) * LANES] = e * ROW_SUBLANES
            gate_t_ref[pl.ds(row0, PEER_TOPK), c * LANES:(c + 1) * LANES] = p / jnp.sum(p, axis=0, keepdims=True)
        return carry

    lax.fori_loop(0, PEER_HEADS, head, 0)
    for c in range(ncol):
        idx_ref[c * LANES:(c + 1) * LANES, :] = idx_t_ref[:, c * LANES:(c + 1) * LANES].T
        gate_ref[c * LANES:(c + 1) * LANES, :] = gate_t_ref[:, c * LANES:(c + 1) * LANES].T


def _block_diag_subkeys(subkeys):
    z = jnp.zeros_like(subkeys[:, 0])
    top = jnp.concatenate([subkeys[:, 0], z], axis=-1)
    bot = jnp.concatenate([z, subkeys[:, 1]], axis=-1)
    return jnp.concatenate([top, bot], axis=1)


def peer_route(h, norm_w, wq, subkeys, *, tm=1024):
    t, d = h.shape
    assert t % tm == 0 and tm % LANES == 0
    sk = _block_diag_subkeys(subkeys)
    return pl.pallas_call(
        functools.partial(_peer_route_body, tm=tm),
        grid=(t // tm,),
        in_specs=[
            pl.BlockSpec((tm, d), lambda i: (i, 0)),
            pl.BlockSpec((1, d), lambda i: (0, 0)),
            pl.BlockSpec(wq.shape, lambda i: (0, 0)),
            pl.BlockSpec(wq.shape, lambda i: (0, 0)),
            pl.BlockSpec(sk.shape, lambda i: (0, 0, 0)),
        ],
        out_specs=[pl.BlockSpec((tm, d), lambda i: (i, 0)),
                   pl.BlockSpec((tm, PEER_SEL), lambda i: (i, 0)),
                   pl.BlockSpec((tm, PEER_SEL), lambda i: (i, 0))],
        out_shape=[jax.ShapeDtypeStruct((t, d), jnp.float32),
                   jax.ShapeDtypeStruct((t, PEER_SEL), jnp.int32),
                   jax.ShapeDtypeStruct((t, PEER_SEL), jnp.float32)],
        scratch_shapes=[pltpu.VMEM((tm, d), jnp.float32), pltpu.VMEM((PEER_SEL, tm), jnp.int32),
                        pltpu.VMEM((PEER_SEL, tm), jnp.float32)]
                       + [pltpu.VMEM((tm // LANES, PEER_TOPK, LANES), jnp.float32),
                          pltpu.VMEM((tm // LANES, PEER_TOPK, LANES), jnp.float32)] * 3,
        compiler_params=pltpu.CompilerParams(dimension_semantics=("arbitrary",), vmem_limit_bytes=VMEM_LIMIT_BYTES),
        name="peer_route",
    )(h, norm_w.reshape(1, d), *_split_bf16(wq), sk)


def _pack_body(x_ref, o_ref, *, te):
    bits = lax.bitcast_convert_type(x_ref[...].astype(jnp.bfloat16).astype(jnp.float32), jnp.int32)
    w = lax.shift_right_logical(bits[:, :ROW_WORDS], 16) | (bits[:, ROW_WORDS:] & jnp.int32(-65536))
    for s in range(ROW_SUBLANES):
        o_ref[pl.ds(s, te, stride=ROW_SUBLANES), :] = w[:, s * LANES:(s + 1) * LANES]


def pack_table(tab, *, te=512):
    n, d = tab.shape
    assert n % te == 0 and d == D_MODEL
    return pl.pallas_call(
        functools.partial(_pack_body, te=te),
        grid=(n // te,),
        in_specs=[pl.BlockSpec((te, d), lambda i: (i, 0))],
        out_specs=pl.BlockSpec((te * ROW_SUBLANES, LANES), lambda i: (i, 0)),
        out_shape=jax.ShapeDtypeStruct((n * ROW_SUBLANES, LANES), jnp.int32),
        compiler_params=pltpu.CompilerParams(dimension_semantics=("arbitrary",)),
        name="pack_table",
    )(tab)


def _unpack(w):
    lo = lax.bitcast_convert_type(w << 16, jnp.float32)
    hi = lax.bitcast_convert_type(w & jnp.int32(-65536), jnp.float32)
    return lo, hi


STAGE_STRIDE = 136
STAGE_ROWS = ROW_SUBLANES * STAGE_STRIDE
STAGE_TOKENS = 8


def _load_table(tab_hbm, tab_vmem, sem, st_first_read):
    @pl.when(pl.program_id(0) == 0)
    def _():
        cp = pltpu.make_async_copy(tab_hbm, tab_vmem, sem)
        cp.start()
        st_first_read[...] = jnp.zeros_like(st_first_read)
        cp.wait()


def _stage_rows(i, k):
    return pl.ds(i * STAGE_ROWS + k, ROW_SUBLANES, stride=STAGE_STRIDE)


def _gather_rows(idx_ref, t, i, tab_vmem, st_ref):
    for k in range(PEER_SEL):
        e = idx_ref[t, k]
        st_ref[_stage_rows(i, k), :] = tab_vmem[pl.ds(pl.multiple_of(e, ROW_SUBLANES), ROW_SUBLANES), :]


def _staged_matrix(st_ref, i):
    r0 = i * STAGE_ROWS
    w = jnp.concatenate([st_ref[r0 + s * STAGE_STRIDE:r0 + s * STAGE_STRIDE + PEER_SEL, :] for s in range(ROW_SUBLANES)],
                        axis=1)
    lo, hi = _unpack(w)
    return jnp.concatenate([lo, hi], axis=1).astype(jnp.bfloat16)


def _mxu_lhs(row):
    hi, lo = _split_bf16(row)
    return jnp.concatenate([hi, lo, jnp.zeros((SUBLANES - 2, row.shape[1]), jnp.bfloat16)], axis=0)


def _staged_pipeline(idx_ref, tab_vmem, st_a, st_b, tb, consume, out_ref):
    n = STAGE_TOKENS

    def half(t_gather, st_fill, t_mxu, st_read):
        rows = []
        for i in range(n):
            _gather_rows(idx_ref, t_gather + i, i, tab_vmem, st_fill)
            rows.append(consume(st_read, i, t_mxu + i))
        out_ref[pl.ds(t_mxu, n), :] = jnp.concatenate(rows, axis=0)

    def two_groups(j, carry):
        t0 = pl.multiple_of(j * 2 * n, 2 * n)
        tp = pl.multiple_of(jnp.maximum(t0 - n, 0), n)
        half(t0, st_a, tp, st_b)
        half(t0 + n, st_b, t0, st_a)
        return carry

    lax.fori_loop(0, tb // (2 * n), two_groups, 0)
    out_ref[tb - n:tb, :] = jnp.concatenate([consume(st_b, i, tb - n + i) for i in range(n)], axis=0)


def _peer_u_body(idx_ref, x_ref, gate_ref, tab_hbm, act_ref, tab_vmem, st_a, st_b, sem, *, tb):
    _load_table(tab_hbm, tab_vmem, sem, st_b)

    def dot_products(st_ref, i, t):
        r = lax.dot_general(_mxu_lhs(x_ref[pl.ds(t, 1), :]), _staged_matrix(st_ref, i), (((1,), (1,)), ((), ())),
                            preferred_element_type=jnp.float32)
        return r[0:1] + r[1:2]

    _staged_pipeline(idx_ref, tab_vmem, st_a, st_b, tb, dot_products, act_ref)
    hp = act_ref[...]
    act_ref[...] = 0.5 * hp * (1.0 + lax.erf(hp * (2.0 ** -0.5))) * gate_ref[...]


def peer_u(idx, x, gate, tab, *, tb=512):
    t, d = x.shape
    assert t % tb == 0 and tb % (2 * STAGE_TOKENS) == 0 and d == D_MODEL
    stage = pltpu.VMEM((STAGE_TOKENS * STAGE_ROWS, LANES), jnp.int32)
    return pl.pallas_call(
        functools.partial(_peer_u_body, tb=tb),
        grid=(t // tb,),
        in_specs=[
            pl.BlockSpec((tb, PEER_SEL), lambda i: (i, 0), memory_space=pltpu.SMEM),
            pl.BlockSpec((tb, d), lambda i: (i, 0)),
            pl.BlockSpec((tb, PEER_SEL), lambda i: (i, 0)),
            pl.BlockSpec(memory_space=pl.ANY),
        ],
        out_specs=pl.BlockSpec((tb, PEER_SEL), lambda i: (i, 0)),
        out_shape=jax.ShapeDtypeStruct((t, PEER_SEL), jnp.float32),
        scratch_shapes=[pltpu.VMEM(tab.shape, jnp.int32), stage, stage, pltpu.SemaphoreType.DMA],
        compiler_params=pltpu.CompilerParams(dimension_semantics=("arbitrary",), vmem_limit_bytes=VMEM_LIMIT_BYTES),
        name="peer_u",
    )(idx, x, gate, tab)


def _peer_v_body(idx_ref, act_ref, h_ref, nw_ref, tab_hbm, o_ref, tab_vmem, st_a, st_b, sem, *, tb, final_norm):
    _load_table(tab_hbm, tab_vmem, sem, st_b)

    def weighted_sum(st_ref, i, t):
        r = jnp.dot(_mxu_lhs(act_ref[pl.ds(t, 1), :]), _staged_matrix(st_ref, i), preferred_element_type=jnp.float32)
        return r[0:1] + r[1:2]

    _staged_pipeline(idx_ref, tab_vmem, st_a, st_b, tb, weighted_sum, o_ref)
    y = h_ref[...] + o_ref[...]
    if final_norm:
        y = y * lax.rsqrt(jnp.mean(y * y, axis=-1, keepdims=True) + NORM_EPS) * nw_ref[...]
    o_ref[...] = y


def peer_v(idx, act, tab, h, norm_w, *, final_norm, tb=512):
    t, d = h.shape
    assert t % tb == 0 and tb % (2 * STAGE_TOKENS) == 0 and d == D_MODEL
    stage = pltpu.VMEM((STAGE_TOKENS * STAGE_ROWS, LANES), jnp.int32)
    return pl.pallas_call(
        functools.partial(_peer_v_body, tb=tb, final_norm=final_norm),
        grid=(t // tb,),
        in_specs=[
            pl.BlockSpec((tb, PEER_SEL), lambda i: (i, 0), memory_space=pltpu.SMEM),
            pl.BlockSpec((tb, PEER_SEL), lambda i: (i, 0)),
            pl.BlockSpec((tb, d), lambda i: (i, 0)),
            pl.BlockSpec((1, d), lambda i: (0, 0)),
            pl.BlockSpec(memory_space=pl.ANY),
        ],
        out_specs=pl.BlockSpec((tb, d), lambda i: (i, 0)),
        out_shape=jax.ShapeDtypeStruct((t, d), jnp.float32),
        scratch_shapes=[pltpu.VMEM(tab.shape, jnp.int32), stage, stage, pltpu.SemaphoreType.DMA],
        compiler_params=pltpu.CompilerParams(dimension_semantics=("arbitrary",), vmem_limit_bytes=VMEM_LIMIT_BYTES),
        name="peer_v",
    )(idx, act, h, norm_w.reshape(1, d), tab)


def kernel(x, norm_mix_w, w_in, shift_mu, attn_sinks, decay_w0, decay_w2, iclr_a0, iclr_a2,
           gate_g2, k_k, k_a, r_k, ln_x_w, ln_x_b, proj_attn, proj_rwkv, w_out, norm_ffn_w,
           peer_wq, peer_subkeys, peer_u_tab, peer_v_tab, norm_final_w):
    b, s, d = x.shape
    t = b * s
    h = x.reshape(t, d)
    for l in range(DEPTH):
        w_lay, mu_lay = layout_in_proj(w_in[l], shift_mu[l])
        qkv, zr, gates = in_proj(h, norm_mix_w[l], w_lay)
        y_attn = swa_attention(qkv, attn_sinks[l], b, s)
        r, dec, k, v, kk, nb, g, bonus = rwkv_prep(zr, mu_lay, decay_w0[l], decay_w2[l], iclr_a0[l], iclr_a2[l],
                                                   gate_g2[l], k_k[l], k_a[l], r_k[l].reshape(-1), b, s)
        to_chains = lambda a: jnp.swapaxes(a.reshape(s, b * RWKV_HEADS, RWKV_HEAD), 1, 2)
        y = rwkv_scan(to_chains(r), to_chains(dec), to_chains(k), to_chains(v), to_chains(kk), to_chains(nb))
        y = jnp.swapaxes(y, 1, 2).reshape(s, b * RWKV_DIM)
        h = merge_out(h, y, g, bonus, y_attn, gates, ln_x_w[l], ln_x_b[l], proj_attn[l], proj_rwkv[l], w_out[l])
        xn2, idx, gate = peer_route(h, norm_ffn_w[l], peer_wq[l], peer_subkeys[l])
        act = peer_u(idx, xn2, gate, pack_table(peer_u_tab[l]))
        h = peer_v(idx, act, pack_table(peer_v_tab[l]), h, norm_final_w, final_norm=(l + 1 == DEPTH))
    return h.reshape(b, s, d)
```
